```python
import math
import jax, jax.numpy as jnp
from jax import lax
import numpy as np

D_MODEL = 1024
BATCH = 8
SEQ = 4096
DEPTH = 4

GRID_W = 64
CTX_LEN = 256
N_MIXERS = 3
N_MOD = 6
CONV_WIDTH = 31
CONV_PAD = (CONV_WIDTH - 1) // 2
F_GROUPS = 8
F_GROUP_DIM = D_MODEL // F_GROUPS
DA_HEADS = 8
DA_QK_DIM = D_MODEL // DA_HEADS // 2
DA_V_DIM = 2 * DA_QK_DIM
ROPE_AXIS_DIM = DA_QK_DIM // 2
ROPE_FREQS = ROPE_AXIS_DIM // 2
ROPE_BASE = 10000.0
Q_BLOCK = 128
D_FF = 3584
N_EXPERTS = 8
TOP_K = 2
NORM_EPS = 1e-6
N_CONV = (DEPTH + 2) // 3
N_FOURIER = (DEPTH + 1) // 3
N_ATTN = DEPTH // 3
N_DENSE = (DEPTH + 1) // 2
N_MOE = DEPTH // 2

kernel_name = 'hybrid_conv_fnet_diffattn_moe_dit'


def rms_norm(x, g, eps=NORM_EPS):
    xf = x.astype(jnp.float32)
    y = xf * lax.rsqrt(jnp.mean(jnp.square(xf), axis=-1, keepdims=True) + eps)
    return (y * g.astype(jnp.float32)).astype(x.dtype)


def layer_norm(x, g, b, eps=1e-5):
    xf = x.astype(jnp.float32)
    mu = jnp.mean(xf, axis=-1, keepdims=True)
    var = jnp.mean(jnp.square(xf - mu), axis=-1, keepdims=True)
    y = (xf - mu) * lax.rsqrt(var + eps)
    return (y * g.astype(jnp.float32) + b.astype(jnp.float32)).astype(x.dtype)


def axial_rope_tables(n_tokens, dtype):
    rows = n_tokens // GRID_W
    row = jnp.repeat(jnp.arange(rows, dtype=jnp.float32), GRID_W)
    col = jnp.tile(jnp.arange(GRID_W, dtype=jnp.float32), rows)
    inv_freq = 1.0 / (ROPE_BASE ** (jnp.arange(ROPE_FREQS, dtype=jnp.float32) * 2.0 / ROPE_AXIS_DIM))
    ang = jnp.stack([row[:, None] * inv_freq, col[:, None] * inv_freq], axis=1)
    ang = jnp.stack([ang, ang], axis=2).reshape(n_tokens, DA_QK_DIM)
    return jnp.cos(ang).astype(dtype), jnp.sin(ang).astype(dtype)


def rope_2d(x, cos, sin):
    xs = x.reshape(x.shape[:-1] + (2, 2, ROPE_FREQS))
    rot = jnp.stack([-xs[..., 1, :], xs[..., 0, :]], axis=-2).reshape(x.shape)
    return x * cos + rot * sin


def conformer_conv(h, w_in, b_in, w_dw, b_dw, ln_g, ln_b, w_out, b_out):
    a, g = jnp.split(h @ w_in + b_in, 2, axis=-1)
    u = a * jax.nn.sigmoid(g)
    u = lax.conv_general_dilated(u, w_dw[:, None, :], window_strides=(1,),
                                 padding=[(CONV_PAD, CONV_PAD)],
                                 dimension_numbers=('NWC', 'WIO', 'NWC'),
                                 feature_group_count=D_MODEL) + b_dw
    u = jax.nn.silu(layer_norm(u, ln_g, ln_b))
    return u @ w_out + b_out


def fourier_mix(h, w_f, b_f):
    b_, n, _ = h.shape
    hg = h.astype(jnp.float32).reshape(b_, n, F_GROUPS, F_GROUP_DIM)
    f = jnp.fft.fft2(hg, axes=(1, 3), norm='ortho').real.astype(h.dtype)
    return f.reshape(b_, n, D_MODEL) @ w_f + b_f


def diff_attn_core(q, k, v, lam):
    s = jnp.einsum('bqhmd,bkhmd->bhmqk', q, k).astype(jnp.float32) * (DA_QK_DIM ** -0.5)
    p = jax.nn.softmax(s, axis=-1)
    a = p[:, :, 0] - lam * p[:, :, 1]
    return jnp.einsum('bhqk,bkhd->bqhd', a.astype(v.dtype), v)


def diff_attention(hx, hc, w_qkv, lam_vecs, subln_g, w_o, lam_init, need_ctx):
    b_, n, _ = hx.shape

    def project(h):
        q, k, v = jnp.split(h @ w_qkv, 3, axis=-1)
        m = h.shape[1]
        return (q.reshape(b_, m, DA_HEADS, 2, DA_QK_DIM),
                k.reshape(b_, m, DA_HEADS, 2, DA_QK_DIM),
                v.reshape(b_, m, DA_HEADS, DA_V_DIM))

    qx, kx, vx = project(hx)
    qc, kc, vc = project(hc)
    cos, sin = axial_rope_tables(n, hx.dtype)
    cos = cos[None, :, None, None, :]
    sin = sin[None, :, None, None, :]
    qx = rope_2d(qx, cos, sin)
    kx = rope_2d(kx, cos, sin)
    lv = lam_vecs.astype(jnp.float32)
    lam = jnp.exp(jnp.sum(lv[0] * lv[1])) - jnp.exp(jnp.sum(lv[2] * lv[3])) + lam_init

    k_all = jnp.concatenate([kc, kx], axis=1)
    v_all = jnp.concatenate([vc, vx], axis=1)
    n_blk = n // Q_BLOCK
    qb = qx.reshape(b_, n_blk, Q_BLOCK, DA_HEADS, 2, DA_QK_DIM).swapaxes(0, 1)
    ox = lax.map(lambda qblk: diff_attn_core(qblk, k_all, v_all, lam), qb)
    ox = ox.swapaxes(0, 1).reshape(b_, n, DA_HEADS, DA_V_DIM)

    def finish(o):
        o = rms_norm(o, subln_g) * (1.0 - lam_init)
        return o.reshape(o.shape[0], o.shape[1], D_MODEL) @ w_o

    out_x = finish(ox)
    out_c = finish(diff_attn_core(qc, kc, vc, lam)) if need_ctx else None
    return out_x, out_c


def swiglu(h, w_gate, w_up, w_down):
    return (jax.nn.silu(h @ w_gate) * (h @ w_up)) @ w_down


def moe_swiglu(h, w_router, w_gate, w_up, w_down):
    b_, n, d = h.shape
    t = h.reshape(-1, d)
    logits = (t @ w_router).astype(jnp.float32)
    top_val, top_idx = lax.top_k(logits, TOP_K)
    top_w = jax.nn.softmax(top_val, axis=-1)
    gates = jnp.sum(jax.nn.one_hot(top_idx, N_EXPERTS, dtype=jnp.float32) * top_w[..., None],
                    axis=1).astype(h.dtype)
    y = jnp.zeros_like(t)
    for e in range(N_EXPERTS):
        y = y + gates[:, e:e + 1] * swiglu(t, w_gate[e], w_up[e], w_down[e])
    return y.reshape(b_, n, d)


def setup_inputs(seed: int = 0) -> dict:
    key = jax.random.key(seed)
    ks = iter(jax.random.split(key, 32))
    D = D_MODEL

    def nrm(shape, scale):
        return jax.random.normal(next(ks), shape, jnp.float32) * scale

    return {
        'x': nrm((BATCH, SEQ, D), 1.0),
        'c': nrm((BATCH, D), 1.0),
        'ctx': nrm((BATCH, CTX_LEN, D), 1.0),
        'c_ctx': nrm((D,), 1.0),
        'w_mod': nrm((DEPTH, D, N_MOD * D), 0.5 * D ** -0.5),
        'b_mod': nrm((DEPTH, N_MOD * D), 0.01),
        'norm_g': 1.0 + nrm((DEPTH, 2, D), 0.05),
        'conv_w_in': nrm((N_CONV, D, 2 * D), D ** -0.5),
        'conv_b_in': nrm((N_CONV, 2 * D), 0.01),
        'conv_w_dw': nrm((N_CONV, CONV_WIDTH, D), CONV_WIDTH ** -0.5),
        'conv_b_dw': nrm((N_CONV, D), 0.01),
        'conv_ln_g': 1.0 + nrm((N_CONV, D), 0.05),
        'conv_ln_b': nrm((N_CONV, D), 0.01),
        'conv_w_out': nrm((N_CONV, D, D), D ** -0.5),
        'conv_b_out': nrm((N_CONV, D), 0.01),
        'fnet_w': nrm((N_FOURIER, D, D), D ** -0.5),
        'fnet_b': nrm((N_FOURIER, D), 0.01),
        'attn_w_qkv': nrm((N_ATTN, D, 3 * D), D ** -0.5),
        'attn_lambda': nrm((N_ATTN, 4, DA_QK_DIM), 0.1),
        'attn_subln_g': 1.0 + nrm((N_ATTN, DA_V_DIM), 0.05),
        'attn_w_o': nrm((N_ATTN, D, D), D ** -0.5),
        'ffn_w_gate': nrm((N_DENSE, D, D_FF), D ** -0.5),
        'ffn_w_up': nrm((N_DENSE, D, D_FF), D ** -0.5),
        'ffn_w_down': nrm((N_DENSE, D_FF, D), D_FF ** -0.5),
        'moe_w_router': nrm((N_MOE, D, N_EXPERTS), D ** -0.5),
        'moe_w_gate': nrm((N_MOE, N_EXPERTS, D, D_FF), D ** -0.5),
        'moe_w_up': nrm((N_MOE, N_EXPERTS, D, D_FF), D ** -0.5),
        'moe_w_down': nrm((N_MOE, N_EXPERTS, D_FF, D), D_FF ** -0.5),
        'final_g': 1.0 + nrm((D,), 0.05),
    }


def reference(x, c, ctx, c_ctx, w_mod, b_mod, norm_g,
              conv_w_in, conv_b_in, conv_w_dw, conv_b_dw, conv_ln_g, conv_ln_b, conv_w_out, conv_b_out,
              fnet_w, fnet_b, attn_w_qkv, attn_lambda, attn_subln_g, attn_w_o,
              ffn_w_gate, ffn_w_up, ffn_w_down,
              moe_w_router, moe_w_gate, moe_w_up, moe_w_down, final_g):
    b_, n, d = x.shape
    n_ctx = ctx.shape[1]
    cx = ctx
    sc = jax.nn.silu(c)
    scc = jax.nn.silu(c_ctx)
    for i in range(DEPTH):
        need_ctx = i < DEPTH - 1
        kind = i % N_MIXERS
        mx = (sc @ w_mod[i] + b_mod[i]).reshape(b_, N_MOD, 1, d)
        mc = (scc @ w_mod[i] + b_mod[i]).reshape(N_MOD, d)

        hx = rms_norm(x, norm_g[i, 0]) * (1 + mx[:, 1]) + mx[:, 0]
        if need_ctx or kind == 2:
            hc = rms_norm(cx, norm_g[i, 0]) * (1 + mc[1]) + mc[0]
        j = i // N_MIXERS
        if kind == 0:
            p = (conv_w_in[j], conv_b_in[j], conv_w_dw[j], conv_b_dw[j],
                 conv_ln_g[j], conv_ln_b[j], conv_w_out[j], conv_b_out[j])
            ox = conformer_conv(hx, *p)
            oc = conformer_conv(hc, *p) if need_ctx else None
        elif kind == 1:
            ox = fourier_mix(hx, fnet_w[j], fnet_b[j])
            oc = fourier_mix(hc, fnet_w[j], fnet_b[j]) if need_ctx else None
        else:
            lam_init = 0.8 - 0.6 * math.exp(-0.3 * i)
            ox, oc = diff_attention(hx, hc, attn_w_qkv[j], attn_lambda[j], attn_subln_g[j],
                                    attn_w_o[j], lam_init, need_ctx)
        x = x + mx[:, 2] * ox
        if need_ctx:
            cx = cx + mc[2] * oc

        hx = rms_norm(x, norm_g[i, 1]) * (1 + mx[:, 4]) + mx[:, 3]
        if need_ctx:
            hc = rms_norm(cx, norm_g[i, 1]) * (1 + mc[4]) + mc[3]
            h = jnp.concatenate([hc, hx], axis=1)
        else:
            h = hx
        j = i // 2
        if i % 2 == 0:
            o = swiglu(h, ffn_w_gate[j], ffn_w_up[j], ffn_w_down[j])
        else:
            o = moe_swiglu(h, moe_w_router[j], moe_w_gate[j], moe_w_up[j], moe_w_down[j])
        x = x + mx[:, 5] * o[:, h.shape[1] - n:]
        if need_ctx:
            cx = cx + mc[5] * o[:, :n_ctx]
    return rms_norm(x, final_g)
```

```python
import functools
import math

import jax
import jax.numpy as jnp
from jax import lax
from jax.experimental import pallas as pl
from jax.experimental.pallas import tpu as pltpu

F32 = jnp.float32
BF16 = jnp.bfloat16

NORM_EPS = 1e-6
LN_EPS = 1e-5
CONV_WIDTH = 31
CONV_PAD = (CONV_WIDTH - 1) // 2
CONV_HALO = 16
GRID_W = 64
F_GROUPS = 8
DA_HEADS = 8
ROPE_BASE = 10000.0
TOP_K = 2
LANES = 128
MOD_ROWS = 16
DFT_ROWS = 64

V7X_VMEM_LIMIT = 56 * 1024 * 1024


def _cparams(sem, vmem=None):
    return pltpu.CompilerParams(dimension_semantics=sem, vmem_limit_bytes=vmem)


def _norm_mod(x, g, scale, shift):
    ms = jnp.mean(x * x, axis=-1, keepdims=True)
    y = x * lax.rsqrt(ms + NORM_EPS)
    return (y * g) * (1.0 + scale) + shift


def _mod_index(tm, n_lat_rows, seq_len, ctx_row):
    n_lat_tiles = n_lat_rows // tm
    per_seq = seq_len // tm

    def f(i):
        return jnp.where(i < n_lat_tiles, i // per_seq, ctx_row)

    return f


def _mod_kernel(c_ref, w_ref, b_ref, o_ref):
    c = c_ref[...]
    sc = c * jax.nn.sigmoid(c)
    o_ref[...] = jnp.dot(sc, w_ref[...], precision=lax.Precision.HIGHEST,
                         preferred_element_type=F32) + b_ref[...]


def _modulation(c_all, w_mod, b_mod):
    depth, d, nd = w_mod.shape
    tn = 1024
    return pl.pallas_call(
        _mod_kernel,
        out_shape=jax.ShapeDtypeStruct((depth, MOD_ROWS, nd), F32),
        grid=(depth, nd // tn),
        in_specs=[
            pl.BlockSpec((MOD_ROWS, d), lambda l, j: (0, 0)),
            pl.BlockSpec((None, d, tn), lambda l, j: (l, 0, j)),
            pl.BlockSpec((None, 1, tn), lambda l, j: (l, 0, j)),
        ],
        out_specs=pl.BlockSpec((None, MOD_ROWS, tn), lambda l, j: (l, 0, j)),
        compiler_params=_cparams(("arbitrary", "arbitrary")),
        name="modulation",
    )(c_all, w_mod, b_mod.reshape(depth, 1, nd))


def _conv_in_kernel(x_ref, mod_ref, g_ref, w_ref, b_ref, u_ref):
    d = u_ref.shape[1]
    h = _norm_mod(x_ref[...], g_ref[...], mod_ref[1:2, :], mod_ref[0:1, :])
    y = jnp.dot(h.astype(BF16), w_ref[...], preferred_element_type=F32) + b_ref[...]
    u_ref[...] = (y[:, :d] * jax.nn.sigmoid(y[:, d:])).astype(u_ref.dtype)


def _conv_in(xs, mod_l, g, w_in, b_in, *, rows, n_lat_rows, seq_len, ctx_row, tm):
    d = xs.shape[1]
    midx = _mod_index(tm, n_lat_rows, seq_len, ctx_row)
    return pl.pallas_call(
        _conv_in_kernel,
        out_shape=jax.ShapeDtypeStruct((rows, d), BF16),
        grid=(rows // tm,),
        in_specs=[
            pl.BlockSpec((tm, d), lambda i: (i, 0)),
            pl.BlockSpec((None, 6, d), lambda i: (midx(i), 0, 0)),
            pl.BlockSpec((1, d), lambda i: (0, 0)),
            pl.BlockSpec((d, 2 * d), lambda i: (0, 0)),
            pl.BlockSpec((1, 2 * d), lambda i: (0, 0)),
        ],
        out_specs=pl.BlockSpec((tm, d), lambda i: (i, 0)),
        compiler_params=_cparams(("arbitrary",), V7X_VMEM_LIMIT),
        name="conv_in",
    )(xs, mod_l, g, w_in, b_in)


def _conv_out_kernel(u_ref, up_ref, un_ref, x_ref, mod_ref, wdw_ref, bdw_ref, lng_ref, lnb_ref,
                     wo_ref, bo_ref, o_ref, ubuf, cbuf, *, n_lat_tiles, tiles_per_seq):
    tm, d = x_ref.shape
    i = pl.program_id(0)
    is_ctx = i >= n_lat_tiles
    j = i % tiles_per_seq
    first = jnp.logical_or(is_ctx, j == 0)
    last = jnp.logical_or(is_ctx, j == tiles_per_seq - 1)
    ubuf[0:CONV_HALO, :] = jnp.where(first, 0.0, up_ref[...].astype(F32))
    ubuf[CONV_HALO:CONV_HALO + tm, :] = u_ref[...].astype(F32)
    ubuf[CONV_HALO + tm:, :] = jnp.where(last, 0.0, un_ref[...].astype(F32))

    for c in range(d // LANES):
        cols = slice(c * LANES, (c + 1) * LANES)
        acc = jnp.zeros((tm, LANES), F32)
        for k in range(CONV_WIDTH):
            r0 = CONV_HALO - CONV_PAD + k
            acc = acc + wdw_ref[k:k + 1, cols] * ubuf[r0:r0 + tm, cols]
        cbuf[:, cols] = acc + bdw_ref[:, cols]
    v = cbuf[...]
    mu = jnp.mean(v, axis=-1, keepdims=True)
    vc = v - mu
    var = jnp.mean(vc * vc, axis=-1, keepdims=True)
    y = vc * lax.rsqrt(var + LN_EPS) * lng_ref[...] + lnb_ref[...]
    y = y * jax.nn.sigmoid(y)
    out = jnp.dot(y.astype(BF16), wo_ref[...], preferred_element_type=F32) + bo_ref[...]
    o_ref[...] = x_ref[...] + mod_ref[2:3, :] * out


def _conv_out(u, xs, mod_l, w_dw, b_dw, ln_g, ln_b, w_out, b_out, *, rows, n_lat_rows, seq_len,
              ctx_row, tm):
    d = xs.shape[1]
    n_tiles = rows // tm
    n_lat_tiles = n_lat_rows // tm
    hpt = tm // CONV_HALO
    n_halo = u.shape[0] // CONV_HALO
    midx = _mod_index(tm, n_lat_rows, seq_len, ctx_row)
    kern = functools.partial(_conv_out_kernel, n_lat_tiles=n_lat_tiles, tiles_per_seq=seq_len // tm)
    return pl.pallas_call(
        kern,
        out_shape=jax.ShapeDtypeStruct(xs.shape, F32),
        grid=(n_tiles,),
        in_specs=[
            pl.BlockSpec((tm, d), lambda i: (i, 0)),
            pl.BlockSpec((CONV_HALO, d), lambda i: (jnp.maximum(i * hpt - 1, 0), 0)),
            pl.BlockSpec((CONV_HALO, d), lambda i: (jnp.minimum((i + 1) * hpt, n_halo - 1), 0)),
            pl.BlockSpec((tm, d), lambda i: (i, 0)),
            pl.BlockSpec((None, 6, d), lambda i: (midx(i), 0, 0)),
            pl.BlockSpec((CONV_WIDTH, d), lambda i: (0, 0)),
            pl.BlockSpec((1, d), lambda i: (0, 0)),
            pl.BlockSpec((1, d), lambda i: (0, 0)),
            pl.BlockSpec((1, d), lambda i: (0, 0)),
            pl.BlockSpec((d, d), lambda i: (0, 0)),
            pl.BlockSpec((1, d), lambda i: (0, 0)),
        ],
        out_specs=pl.BlockSpec((tm, d), lambda i: (i, 0)),
        scratch_shapes=[pltpu.VMEM((tm + 2 * CONV_HALO, d), F32), pltpu.VMEM((tm, d), F32)],
        input_output_aliases={3: 0},
        compiler_params=_cparams(("arbitrary",), V7X_VMEM_LIMIT),
        name="conv_out",
    )(u, u, u, xs, mod_l, w_dw, b_dw, ln_g, ln_b, w_out, b_out)


def _ffn_kernel(te_ref, nv_ref, *refs, fuse_norm, n_f):
    if fuse_norm:
        x_ref, mod_ref, g_ref, wg_ref, wu_ref, wd_ref, o_ref, h_scr, acc_scr = refs
    else:
        x_ref, wg_ref, wu_ref, wd_ref, o_ref, h_scr, acc_scr = refs
    del te_ref
    j = pl.program_id(0)
    f = pl.program_id(1)
    valid = j < nv_ref[0]

    @pl.when(valid)
    def _():
        @pl.when(f == 0)
        def _():
            if fuse_norm:
                h = _norm_mod(x_ref[...], g_ref[...], mod_ref[4:5, :], mod_ref[3:4, :])
            else:
                h = x_ref[...]
            h_scr[...] = h.astype(BF16)

        h = h_scr[...]
        gt = jnp.dot(h, wg_ref[...], preferred_element_type=F32)
        up = jnp.dot(h, wu_ref[...], preferred_element_type=F32)
        a = (gt * jax.nn.sigmoid(gt) * up).astype(BF16)
        part = jnp.dot(a, wd_ref[...], preferred_element_type=F32)

        @pl.when(f == 0)
        def _():
            acc_scr[...] = part

        @pl.when(f > 0)
        def _():
            acc_scr[...] += part

        @pl.when(f == n_f - 1)
        def _():
            if fuse_norm:
                o_ref[...] = x_ref[...] + mod_ref[5:6, :] * acc_scr[...]
            else:
                o_ref[...] = acc_scr[...]

    @pl.when(jnp.logical_and(jnp.logical_not(valid), f == n_f - 1))
    def _():
        o_ref[...] = jnp.zeros_like(o_ref)


def _ffn(x, tile_expert, n_valid, wg, wu, wd, *, rows, tm, tf, mod_l=None, g=None, midx=None):
    d = x.shape[1]
    f_dim = wg.shape[2]
    n_f = f_dim // tf
    fuse_norm = mod_l is not None

    def fsel(j, f, nv):
        return jnp.where(j < nv[0], f, 0)

    in_specs = [pl.BlockSpec((tm, d), lambda j, f, te, nv: (j, 0))]
    args = [x]
    if fuse_norm:
        in_specs += [pl.BlockSpec((None, 6, d), lambda j, f, te, nv: (midx(j), 0, 0)),
                     pl.BlockSpec((1, d), lambda j, f, te, nv: (0, 0))]
        args += [mod_l, g]
    in_specs += [
        pl.BlockSpec((None, d, tf), lambda j, f, te, nv: (te[j], 0, fsel(j, f, nv))),
        pl.BlockSpec((None, d, tf), lambda j, f, te, nv: (te[j], 0, fsel(j, f, nv))),
        pl.BlockSpec((None, tf, d), lambda j, f, te, nv: (te[j], fsel(j, f, nv), 0)),
    ]
    args += [wg, wu, wd]
    kern = functools.partial(_ffn_kernel, fuse_norm=fuse_norm, n_f=n_f)
    return pl.pallas_call(
        kern,
        out_shape=jax.ShapeDtypeStruct(x.shape, F32),
        grid_spec=pltpu.PrefetchScalarGridSpec(
            num_scalar_prefetch=2,
            grid=(rows // tm, n_f),
            in_specs=in_specs,
            out_specs=pl.BlockSpec((tm, d), lambda j, f, te, nv: (j, 0)),
            scratch_shapes=[pltpu.VMEM((tm, d), BF16), pltpu.VMEM((tm, d), F32)],
        ),
        input_output_aliases=({2: 0} if fuse_norm else {}),
        compiler_params=_cparams(("arbitrary", "arbitrary"), V7X_VMEM_LIMIT),
        name="ffn_dense" if fuse_norm else "ffn_grouped",
    )(tile_expert, n_valid, *args)


def _router_kernel(x_ref, mod_ref, g_ref, wr_ref, h_ref, meta_ref, gw_ref, cnt_ref, carry, *, n_exp):
    tm = x_ref.shape[0]
    i = pl.program_id(0)

    @pl.when(i == 0)
    def _():
        carry[...] = jnp.zeros_like(carry)

    h = _norm_mod(x_ref[...], g_ref[...], mod_ref[4:5, :], mod_ref[3:4, :])
    h_ref[...] = h
    logits = jnp.dot(h, wr_ref[...], precision=lax.Precision.HIGHEST, preferred_element_type=F32)
    lane = lax.broadcasted_iota(jnp.int32, (tm, LANES), 1).astype(F32)
    neg = jnp.float32(-jnp.inf)
    lg = jnp.where(lane < n_exp, logits, neg)
    m1 = jnp.max(lg, axis=-1, keepdims=True)
    i1 = jnp.min(jnp.where(lg == m1, lane, float(LANES)), axis=-1, keepdims=True)
    lg2 = jnp.where(lane == i1, neg, lg)
    m2 = jnp.max(lg2, axis=-1, keepdims=True)
    i2 = jnp.min(jnp.where(lg2 == m2, lane, float(LANES)), axis=-1, keepdims=True)
    e2 = jnp.exp(m2 - m1)
    w1 = 1.0 / (1.0 + e2)
    w2 = e2 * w1
    sel1 = lane == i1
    sel2 = lane == i2
    onehot = jnp.where(jnp.logical_or(sel1, sel2), 1.0, 0.0)
    rr = lax.broadcasted_iota(jnp.int32, (tm, tm), 0)
    cc = lax.broadcasted_iota(jnp.int32, (tm, tm), 1)
    tri = jnp.where(rr > cc, 1.0, 0.0).astype(BF16)
    cum = jnp.dot(tri, onehot.astype(BF16), preferred_element_type=F32) + carry[...]
    r1 = jnp.sum(jnp.where(sel1, cum, 0.0), axis=-1, keepdims=True)
    r2 = jnp.sum(jnp.where(sel2, cum, 0.0), axis=-1, keepdims=True)
    carry[...] += jnp.sum(onehot, axis=0, keepdims=True)
    meta = jnp.where(lane == 0, i1, jnp.where(lane == 1, i2, jnp.where(lane == 2, r1,
                     jnp.where(lane == 3, r2, 0.0))))
    meta_ref[...] = meta.astype(jnp.int32)
    gw_ref[...] = jnp.where(lane == 0, w1, jnp.where(lane == 1, w2, 0.0))
    cnt_ref[...] = jnp.broadcast_to(carry[...], cnt_ref.shape)


def _router(xs, mod_l, g, w_router_pad, *, rows, n_lat_rows, seq_len, ctx_row, tm, n_exp):
    d = xs.shape[1]
    midx = _mod_index(tm, n_lat_rows, seq_len, ctx_row)
    kern = functools.partial(_router_kernel, n_exp=n_exp)
    return pl.pallas_call(
        kern,
        out_shape=[jax.ShapeDtypeStruct((rows, d), F32),
                   jax.ShapeDtypeStruct((rows, LANES), jnp.int32),
                   jax.ShapeDtypeStruct((rows, LANES), F32),
                   jax.ShapeDtypeStruct((8, LANES), F32)],
        grid=(rows // tm,),
        in_specs=[
            pl.BlockSpec((tm, d), lambda i: (i, 0)),
            pl.BlockSpec((None, 6, d), lambda i: (midx(i), 0, 0)),
            pl.BlockSpec((1, d), lambda i: (0, 0)),
            pl.BlockSpec((d, LANES), lambda i: (0, 0)),
        ],
        out_specs=[pl.BlockSpec((tm, d), lambda i: (i, 0)),
                   pl.BlockSpec((tm, LANES), lambda i: (i, 0)),
                   pl.BlockSpec((tm, LANES), lambda i: (i, 0)),
                   pl.BlockSpec((8, LANES), lambda i: (0, 0))],
        scratch_shapes=[pltpu.VMEM((1, LANES), F32)],
        compiler_params=_cparams(("arbitrary",), V7X_VMEM_LIMIT),
        name="router",
    )(xs, mod_l, g, w_router_pad)


def _permute_kernel(sidx_ref, didx_ref, src_ref, *rest, ch):
    dst_ref, sem = rest[-2:]

    def row_copy(s, t):
        return pltpu.make_async_copy(src_ref.at[pl.ds(s, 1)], dst_ref.at[pl.ds(t, 1)], sem)

    def start(i, carry):
        row_copy(sidx_ref[0, i], didx_ref[0, i]).start()
        return carry

    lax.fori_loop(0, ch, start, 0)

    def wait(i, carry):
        row_copy(0, 0).wait()
        return carry

    lax.fori_loop(0, ch, wait, 0)


def _permute_rows(src, src_idx, dst_idx, dst_rows, *, ch, zero_init):
    n = src_idx.shape[0]
    n_chunks = n // ch
    kern = functools.partial(_permute_kernel, ch=ch)
    in_specs = [
        pl.BlockSpec((None, 1, ch), lambda i: (i, 0, 0), memory_space=pltpu.SMEM),
        pl.BlockSpec((None, 1, ch), lambda i: (i, 0, 0), memory_space=pltpu.SMEM),
        pl.BlockSpec(memory_space=pl.ANY),
    ]
    args = [src_idx.reshape(n_chunks, 1, ch), dst_idx.reshape(n_chunks, 1, ch), src]
    if zero_init:
        in_specs.append(pl.BlockSpec(memory_space=pl.ANY))
        args.append(jnp.zeros((dst_rows, src.shape[1]), src.dtype))
    return pl.pallas_call(
        kern,
        out_shape=jax.ShapeDtypeStruct((dst_rows, src.shape[1]), src.dtype),
        grid=(n_chunks,),
        in_specs=in_specs,
        out_specs=pl.BlockSpec(memory_space=pl.ANY),
        scratch_shapes=[pltpu.SemaphoreType.DMA],
        input_output_aliases=({3: 0} if zero_init else {}),
        compiler_params=_cparams(("arbitrary",)),
        name="permute_rows",
    )(*args)


def _combine_kernel(x_ref, y1_ref, y2_ref, gw_ref, mod_ref, *rest, final):
    if final:
        fg_ref, o_ref = rest
    else:
        (o_ref,) = rest
    gw = gw_ref[...]
    y = gw[:, 0:1] * y1_ref[...] + gw[:, 1:2] * y2_ref[...]
    xn = x_ref[...] + mod_ref[5:6, :] * y
    if final:
        ms = jnp.mean(xn * xn, axis=-1, keepdims=True)
        xn = xn * lax.rsqrt(ms + NORM_EPS) * fg_ref[...]
    o_ref[...] = xn


def _combine(xs, ypair, gw, mod_l, *, rows, n_lat_rows, seq_len, ctx_row, tm, final_g=None):
    d = xs.shape[1]
    n_tiles = rows // tm
    midx = _mod_index(tm, n_lat_rows, seq_len, ctx_row)
    final = final_g is not None
    in_specs = [
        pl.BlockSpec((tm, d), lambda i: (i, 0)),
        pl.BlockSpec((tm, d), lambda i: (i, 0)),
        pl.BlockSpec((tm, d), lambda i: (i + n_tiles, 0)),
        pl.BlockSpec((tm, LANES), lambda i: (i, 0)),
        pl.BlockSpec((None, 6, d), lambda i: (midx(i), 0, 0)),
    ]
    args = [xs, ypair, ypair, gw, mod_l]
    if final:
        in_specs.append(pl.BlockSpec((1, d), lambda i: (0, 0)))
        args.append(final_g)
    out_rows = rows if final else xs.shape[0]
    return pl.pallas_call(
        functools.partial(_combine_kernel, final=final),
        out_shape=jax.ShapeDtypeStruct((out_rows, d), F32),
        grid=(n_tiles,),
        in_specs=in_specs,
        out_specs=pl.BlockSpec((tm, d), lambda i: (i, 0)),
        input_output_aliases=({} if final else {0: 0}),
        compiler_params=_cparams(("arbitrary",), V7X_VMEM_LIMIT),
        name="moe_combine",
    )(*args)


def _fnet_a_kernel(x_ref, mod_ref, g_ref, cs_ref, yc_ref, ys_ref, *, n_groups):
    h = _norm_mod(x_ref[...], g_ref[...], mod_ref[1:2, :], mod_ref[0:1, :]).astype(BF16)
    gd = cs_ref.shape[0]
    for gi in range(n_groups):
        y = jnp.dot(h[:, gi * gd:(gi + 1) * gd], cs_ref[...], preferred_element_type=F32)
        yc_ref[:, gi * gd:(gi + 1) * gd] = y[:, :gd].astype(yc_ref.dtype)
        ys_ref[:, gi * gd:(gi + 1) * gd] = y[:, gd:].astype(ys_ref.dtype)


def _fnet_a(xs, mod_l, g, cs, *, rows, n_lat_rows, seq_len, ctx_row, tm):
    d = xs.shape[1]
    gd = cs.shape[0]
    midx = _mod_index(tm, n_lat_rows, seq_len, ctx_row)
    return pl.pallas_call(
        functools.partial(_fnet_a_kernel, n_groups=d // gd),
        out_shape=[jax.ShapeDtypeStruct((rows, d), BF16)] * 2,
        grid=(rows // tm,),
        in_specs=[
            pl.BlockSpec((tm, d), lambda i: (i, 0)),
            pl.BlockSpec((None, 6, d), lambda i: (midx(i), 0, 0)),
            pl.BlockSpec((1, d), lambda i: (0, 0)),
            pl.BlockSpec((gd, 2 * gd), lambda i: (0, 0)),
        ],
        out_specs=[pl.BlockSpec((tm, d), lambda i: (i, 0))] * 2,
        compiler_params=_cparams(("arbitrary",), V7X_VMEM_LIMIT),
        name="fnet_group_dft",
    )(xs, mod_l, g, cs)


def _dft_table_kernel(ac_ref, as_ref, bc_ref, bs_ref, c_ref, sn_ref):
    ac, as_ = ac_ref[...], as_ref[...]
    bc, bs = bc_ref[...], bs_ref[...]
    c_ref[...] = (bc * ac - bs * as_).astype(c_ref.dtype)
    sn_ref[...] = (-(bs * ac + bc * as_)).astype(sn_ref.dtype)


def _dft_tables(n):
    r = DFT_ROWS
    k = jnp.arange(n, dtype=jnp.int32)[None, :]
    j1 = jnp.arange(r, dtype=jnp.int32)[:, None]
    j0 = (jnp.arange(n // r, dtype=jnp.int32) * r)[:, None]
    ang1 = ((j1 * k) % n).astype(F32) * (2.0 * math.pi / n)
    ang0 = ((j0 * k) % n).astype(F32) * (2.0 * math.pi / n)
    scale = 1.0 / math.sqrt(n)
    ac, as_ = jnp.cos(ang1), jnp.sin(ang1)
    bc = (jnp.cos(ang0) * scale).reshape(n // r, 1, n)
    bs = (jnp.sin(ang0) * scale).reshape(n // r, 1, n)
    return pl.pallas_call(
        _dft_table_kernel,
        out_shape=[jax.ShapeDtypeStruct((n, n), BF16)] * 2,
        grid=(n // r,),
        in_specs=[
            pl.BlockSpec((r, n), lambda i: (0, 0)),
            pl.BlockSpec((r, n), lambda i: (0, 0)),
            pl.BlockSpec((None, 1, n), lambda i: (i, 0, 0)),
            pl.BlockSpec((None, 1, n), lambda i: (i, 0, 0)),
        ],
        out_specs=[pl.BlockSpec((r, n), lambda i: (i, 0))] * 2,
        compiler_params=_cparams(("arbitrary",)),
        name="dft_tables",
    )(ac, as_, bc, bs)


def _fnet_b_kernel(c_ref, sn_ref, yc_ref, ys_ref, x_ref, mod_ref, wf_ref, bf_ref, o_ref, acc, *, n_k):
    k = pl.program_id(2)
    part = (jnp.dot(c_ref[...], yc_ref[...], preferred_element_type=F32)
            + jnp.dot(sn_ref[...], ys_ref[...], preferred_element_type=F32))

    @pl.when(k == 0)
    def _():
        acc[...] = part

    @pl.when(k > 0)
    def _():
        acc[...] += part

    @pl.when(k == n_k - 1)
    def _():
        z = acc[...].astype(BF16)
        out = jnp.dot(z, wf_ref[...], preferred_element_type=F32) + bf_ref[...]
        o_ref[...] = x_ref[...] + mod_ref[2:3, :] * out


def _fnet_b(ctab, stab, yc, ys, xs, mod_l, wf, bf, *, n_batch, seq_len, row_off, mod_ctx_row, tm, tk):
    d = xs.shape[1]
    n_i = seq_len // tm
    n_k = seq_len // tk
    off_m = row_off // tm
    off_k = row_off // tk

    def mrow(b):
        return b if mod_ctx_row is None else mod_ctx_row

    return pl.pallas_call(
        functools.partial(_fnet_b_kernel, n_k=n_k),
        out_shape=jax.ShapeDtypeStruct(xs.shape, F32),
        grid=(n_batch, n_i, n_k),
        in_specs=[
            pl.BlockSpec((tm, tk), lambda b, i, k: (i, k)),
            pl.BlockSpec((tm, tk), lambda b, i, k: (i, k)),
            pl.BlockSpec((tk, d), lambda b, i, k: (off_k + b * n_k + k, 0)),
            pl.BlockSpec((tk, d), lambda b, i, k: (off_k + b * n_k + k, 0)),
            pl.BlockSpec((tm, d), lambda b, i, k: (off_m + b * n_i + i, 0)),
            pl.BlockSpec((None, 6, d), lambda b, i, k: (mrow(b), 0, 0)),
            pl.BlockSpec((d, d), lambda b, i, k: (0, 0)),
            pl.BlockSpec((1, d), lambda b, i, k: (0, 0)),
        ],
        out_specs=pl.BlockSpec((tm, d), lambda b, i, k: (off_m + b * n_i + i, 0)),
        scratch_shapes=[pltpu.VMEM((tm, d), F32)],
        input_output_aliases={4: 0},
        compiler_params=_cparams(("arbitrary", "arbitrary", "arbitrary"), V7X_VMEM_LIMIT),
        name="fnet_seq_dft",
    )(ctab, stab, yc, ys, xs, mod_l, wf, bf)


def _qkv_kernel(x_ref, mod_ref, g_ref, w_ref, cos_ref, sin_ref, q_ref, k_ref, v_ref, *,
                n_lat_tiles, q_scale):
    tm, d = x_ref.shape
    i = pl.program_id(0)
    is_ctx = i >= n_lat_tiles
    h = _norm_mod(x_ref[...], g_ref[...], mod_ref[1:2, :], mod_ref[0:1, :])
    y = jnp.dot(h.astype(BF16), w_ref[...], preferred_element_type=F32)
    tw = cos_ref.shape[1]
    cos = jnp.where(is_ctx, 1.0, cos_ref[...])
    sin = jnp.where(is_ctx, 0.0, sin_ref[...])
    lane = lax.broadcasted_iota(jnp.int32, (tm, tw), 1)
    half = tw // 8
    lo = (lane % (2 * half)) < half

    def rope(t):
        rot = jnp.where(lo, -pltpu.roll(t, tw - half, 1), pltpu.roll(t, half, 1))
        return t * cos + rot * sin

    for hd in range(d // tw):
        cols = slice(hd * tw, (hd + 1) * tw)
        q_ref[:, cols] = (rope(y[:, hd * tw:(hd + 1) * tw]) * q_scale).astype(q_ref.dtype)
        k_ref[:, cols] = rope(y[:, d + hd * tw:d + (hd + 1) * tw]).astype(k_ref.dtype)
    v_ref[...] = y[:, 2 * d:].astype(v_ref.dtype)


def _qkv(xs, mod_l, g, w_qkv, cos_t, sin_t, *, rows, n_lat_rows, seq_len, ctx_row, tm, q_scale):
    d = xs.shape[1]
    midx = _mod_index(tm, n_lat_rows, seq_len, ctx_row)
    per_seq = seq_len // tm
    n_lat_tiles = n_lat_rows // tm
    tw = cos_t.shape[1]

    def pos(i):
        return jnp.where(i < n_lat_tiles, i % per_seq, 0)

    kern = functools.partial(_qkv_kernel, n_lat_tiles=n_lat_tiles, q_scale=q_scale)
    return pl.pallas_call(
        kern,
        out_shape=[jax.ShapeDtypeStruct((rows, d), BF16)] * 3,
        grid=(rows // tm,),
        in_specs=[
            pl.BlockSpec((tm, d), lambda i: (i, 0)),
            pl.BlockSpec((None, 6, d), lambda i: (midx(i), 0, 0)),
            pl.BlockSpec((1, d), lambda i: (0, 0)),
            pl.BlockSpec((d, 3 * d), lambda i: (0, 0)),
            pl.BlockSpec((tm, tw), lambda i: (pos(i), 0)),
            pl.BlockSpec((tm, tw), lambda i: (pos(i), 0)),
        ],
        out_specs=[pl.BlockSpec((tm, d), lambda i: (i, 0))] * 3,
        compiler_params=_cparams(("arbitrary",), V7X_VMEM_LIMIT),
        name="attn_qkv",
    )(xs, mod_l, g, w_qkv, cos_t, sin_t)


def _dot_nt(a, b):
    return lax.dot_general(a, b, (((1,), (1,)), ((), ())), preferred_element_type=F32)


def _flash_kernel(lam_ref, sg_ref, q_ref, kc_ref, vc_ref, *rest, tk, n_kx, lam_init):
    if n_kx:
        kx_ref, vx_ref, o_ref = rest
    else:
        (o_ref,) = rest
    tq, dh = q_ref.shape
    q = q_ref[...]
    lane = lax.broadcasted_iota(jnp.int32, (tq, dh), 1)
    zero = jnp.zeros_like(q)
    qa = jnp.where(lane < dh // 2, q, zero)
    qb = jnp.where(lane >= dh // 2, q, zero)

    def scores(k):
        return _dot_nt(qa, k), _dot_nt(qb, k)

    def init(k, v):
        out = []
        for s in scores(k):
            m = jnp.max(s, axis=-1, keepdims=True)
            p = jnp.exp(s - m)
            l = jnp.sum(p, axis=-1, keepdims=True)
            out += [m, l, jnp.dot(p.astype(BF16), v, preferred_element_type=F32)]
        return tuple(out)

    def update(k, v, carry):
        out = []
        for s, (m, l, acc) in zip(scores(k), (carry[0:3], carry[3:6])):
            m_new = jnp.maximum(m, jnp.max(s, axis=-1, keepdims=True))
            alpha = jnp.exp(m - m_new)
            p = jnp.exp(s - m_new)
            l = alpha * l + jnp.sum(p, axis=-1, keepdims=True)
            acc = alpha * acc + jnp.dot(p.astype(BF16), v, preferred_element_type=F32)
            out += [m_new, l, acc]
        return tuple(out)

    carry = init(kc_ref[...], vc_ref[...])
    if n_kx:
        def body(t, c):
            r0 = pl.multiple_of(t * tk, tk)
            return update(kx_ref[pl.ds(r0, tk), :], vx_ref[pl.ds(r0, tk), :], c)

        carry = lax.fori_loop(0, n_kx, body, carry)
    m1, l1, a1, m2, l2, a2 = carry
    lv = lam_ref[...]
    lam = (jnp.exp(jnp.sum(lv[0:1, :] * lv[1:2, :], axis=-1, keepdims=True))
           - jnp.exp(jnp.sum(lv[2:3, :] * lv[3:4, :], axis=-1, keepdims=True)) + lam_init)
    o = a1 / l1 - lam * (a2 / l2)
    ms = jnp.mean(o * o, axis=-1, keepdims=True)
    o = o * lax.rsqrt(ms + NORM_EPS) * sg_ref[...] * (1.0 - lam_init)
    o_ref[...] = o.astype(o_ref.dtype)


def _flash(q, k, v, lam_vecs, subln_g, out_init, *, n_batch, n_heads, q_len, q_off, kc_len, kc_off,
           kx_len, tq, tk, lam_init):
    dh = q.shape[1] // n_heads
    n_q = q_len // tq
    qo = q_off // tq
    kco = kc_off // kc_len
    n_kx = kx_len // tk if kx_len else 0
    in_specs = [
        pl.BlockSpec(lam_vecs.shape, lambda b, h, i: (0, 0)),
        pl.BlockSpec((1, dh), lambda b, h, i: (0, 0)),
        pl.BlockSpec((tq, dh), lambda b, h, i: (qo + b * n_q + i, h)),
        pl.BlockSpec((kc_len, dh), lambda b, h, i: (kco + b, h)),
        pl.BlockSpec((kc_len, dh), lambda b, h, i: (kco + b, h)),
    ]
    args = [lam_vecs, subln_g, q, k, v]
    if n_kx:
        in_specs += [pl.BlockSpec((kx_len, dh), lambda b, h, i: (b, h)),
                     pl.BlockSpec((kx_len, dh), lambda b, h, i: (b, h))]
        args += [k, v]
    in_specs.append(pl.BlockSpec(memory_space=pl.ANY))
    args.append(out_init)
    kern = functools.partial(_flash_kernel_alias, tk=tk, n_kx=n_kx, lam_init=lam_init)
    return pl.pallas_call(
        kern,
        out_shape=jax.ShapeDtypeStruct(out_init.shape, out_init.dtype),
        grid=(n_batch, n_heads, n_q),
        in_specs=in_specs,
        out_specs=pl.BlockSpec((tq, dh), lambda b, h, i: (qo + b * n_q + i, h)),
        input_output_aliases={len(args) - 1: 0},
        compiler_params=_cparams(("arbitrary", "arbitrary", "arbitrary"), V7X_VMEM_LIMIT),
        name="diff_attn_latent" if n_kx else "diff_attn_ctx",
    )(*args)


def _flash_kernel_alias(*refs, tk, n_kx, lam_init):
    _flash_kernel(*refs[:-2], refs[-1], tk=tk, n_kx=n_kx, lam_init=lam_init)


def _proj_residual_kernel(a_ref, w_ref, x_ref, mod_ref, o_ref):
    out = jnp.dot(a_ref[...], w_ref[...], preferred_element_type=F32)
    o_ref[...] = x_ref[...] + mod_ref[2:3, :] * out


def _proj_residual(a, w, xs, mod_l, *, rows, n_lat_rows, seq_len, ctx_row, tm):
    d = xs.shape[1]
    midx = _mod_index(tm, n_lat_rows, seq_len, ctx_row)
    return pl.pallas_call(
        _proj_residual_kernel,
        out_shape=jax.ShapeDtypeStruct(xs.shape, F32),
        grid=(rows // tm,),
        in_specs=[
            pl.BlockSpec((tm, d), lambda i: (i, 0)),
            pl.BlockSpec((d, d), lambda i: (0, 0)),
            pl.BlockSpec((tm, d), lambda i: (i, 0)),
            pl.BlockSpec((None, 6, d), lambda i: (midx(i), 0, 0)),
        ],
        out_specs=pl.BlockSpec((tm, d), lambda i: (i, 0)),
        input_output_aliases={2: 0},
        compiler_params=_cparams(("arbitrary",), V7X_VMEM_LIMIT),
        name="attn_out_proj",
    )(a, w, xs, mod_l)


def _rope_tables(n_tokens, width):
    rope_axis_dim = 32
    freqs = rope_axis_dim // 2
    rows = n_tokens // GRID_W
    row = jnp.repeat(jnp.arange(rows, dtype=F32), GRID_W)
    col = jnp.tile(jnp.arange(GRID_W, dtype=F32), rows)
    inv_freq = 1.0 / (ROPE_BASE ** (jnp.arange(freqs, dtype=F32) * 2.0 / rope_axis_dim))
    ang = jnp.stack([row[:, None] * inv_freq, col[:, None] * inv_freq], axis=1)
    ang = jnp.stack([ang, ang], axis=2).reshape(n_tokens, 4 * freqs)
    ang = jnp.tile(ang, (1, width // (4 * freqs)))
    return jnp.cos(ang), jnp.sin(ang)


def _dense_ffn_layer(xs, mod_l, g, wg, wu, wd, geo, rows):
    tm = geo["tm_ffn"]
    n_tiles = rows // tm
    te = jnp.zeros((n_tiles,), jnp.int32)
    nv = jnp.full((1,), n_tiles, jnp.int32)
    midx = _mod_index(tm, geo["n_lat_rows"], geo["seq_len"], geo["ctx_row"])
    return _ffn(xs, te, nv, wg[None].astype(BF16), wu[None].astype(BF16), wd[None].astype(BF16),
                rows=rows, tm=tm, tf=geo["tf"], mod_l=mod_l, g=g, midx=midx)


def _moe_layer(xs, mod_l, g, w_router, wg, wu, wd, geo, rows, final_g=None):
    d = xs.shape[1]
    n_exp = w_router.shape[1]
    tm = geo["tm_ffn"]
    common = dict(rows=rows, n_lat_rows=geo["n_lat_rows"], seq_len=geo["seq_len"], ctx_row=geo["ctx_row"])
    wr = jnp.zeros((d, LANES), F32).at[:, :n_exp].set(w_router)
    h, meta, gw, cnt = _router(xs, mod_l, g, wr, tm=geo["tm_router"], n_exp=n_exp, **common)
    counts = cnt[0, :n_exp].astype(jnp.int32)
    padded = ((counts + tm - 1) // tm) * tm
    ends = jnp.cumsum(padded)
    offs = ends - padded
    n_sorted_tiles = (TOP_K * rows) // tm + n_exp
    tile_start = jnp.arange(n_sorted_tiles, dtype=jnp.int32) * tm
    tile_expert = jnp.minimum(jnp.sum(tile_start[:, None] >= ends[None, :], axis=1), n_exp - 1).astype(jnp.int32)
    n_valid = (ends[-1] // tm).astype(jnp.int32).reshape(1)
    e1, e2, r1, r2 = meta[:, 0], meta[:, 1], meta[:, 2], meta[:, 3]
    dest = jnp.concatenate([offs[e1] + r1, offs[e2] + r2]).astype(jnp.int32)
    token = jnp.tile(jnp.arange(rows, dtype=jnp.int32), TOP_K)
    slot = jnp.arange(TOP_K * rows, dtype=jnp.int32)
    hs = _permute_rows(h, token, dest, n_sorted_tiles * tm, ch=geo["perm_chunk"], zero_init=True)
    ys = _ffn(hs, tile_expert, n_valid, wg.astype(BF16), wu.astype(BF16), wd.astype(BF16),
              rows=n_sorted_tiles * tm, tm=tm, tf=geo["tf"])
    ypair = _permute_rows(ys, dest, slot, TOP_K * rows, ch=geo["perm_chunk"], zero_init=False)
    return _combine(xs, ypair, gw, mod_l, tm=geo["tm_row"], final_g=final_g, **common)


def kernel(x, c, ctx, c_ctx, w_mod, b_mod, norm_g, conv_w_in, conv_b_in, conv_w_dw, conv_b_dw,
           conv_ln_g, conv_ln_b, conv_w_out, conv_b_out, fnet_w, fnet_b, attn_w_qkv, attn_lambda,
           attn_subln_g, attn_w_o, ffn_w_gate, ffn_w_up, ffn_w_down, moe_w_router, moe_w_gate,
           moe_w_up, moe_w_down, final_g):
    b_, n, d = x.shape
    n_ctx = ctx.shape[1]
    depth = w_mod.shape[0]
    n_lat = b_ * n
    n_all = n_lat + b_ * n_ctx
    assert b_ < MOD_ROWS and d % LANES == 0 and n % GRID_W == 0 and n % n_ctx == 0
    assert d // DA_HEADS == LANES
    geo = dict(n_lat_rows=n_lat, seq_len=n, ctx_row=b_,
               tm_row=min(512, n_ctx * b_, n), tm_ffn=min(1024, n_ctx * b_, n),
               tm_router=min(512, n_ctx * b_, n), tf=min(512, ffn_w_gate.shape[2]),
               tm_conv=n_ctx, perm_chunk=math.gcd(2048, TOP_K * n_lat, TOP_K * n_all))

    xs = jnp.concatenate([x.reshape(n_lat, d), ctx.reshape(b_ * n_ctx, d)], axis=0)
    c_all = jnp.zeros((MOD_ROWS, d), F32).at[:b_].set(c).at[b_].set(c_ctx)
    mod = _modulation(c_all, w_mod, b_mod).reshape(depth, MOD_ROWS, 6, d)

    out = None
    for i in range(depth):
        need_ctx = i < depth - 1
        rows = n_all if need_ctx else n_lat
        kind = i % 3
        mod_l = mod[i]
        common = dict(rows=rows, n_lat_rows=n_lat, seq_len=n, ctx_row=b_)
        g0 = norm_g[i, 0].reshape(1, d)
        g1 = norm_g[i, 1].reshape(1, d)
        j = i // 3
        if kind == 0:
            u = _conv_in(xs, mod_l, g0, conv_w_in[j].astype(BF16), conv_b_in[j].reshape(1, 2 * d),
                         tm=geo["tm_row"], **common)
            xs = _conv_out(u, xs, mod_l, conv_w_dw[j], conv_b_dw[j].reshape(1, d),
                           conv_ln_g[j].reshape(1, d), conv_ln_b[j].reshape(1, d),
                           conv_w_out[j].astype(BF16), conv_b_out[j].reshape(1, d),
                           tm=geo["tm_conv"], **common)
        elif kind == 1:
            gd = d // F_GROUPS
            kk = jnp.arange(gd, dtype=jnp.int32)
            ang = ((kk[:, None] * kk[None, :]) % gd).astype(F32) * (2.0 * math.pi / gd)
            cs = (jnp.concatenate([jnp.cos(ang), jnp.sin(ang)], axis=1) / math.sqrt(gd)).astype(BF16)
            yc, ys = _fnet_a(xs, mod_l, g0, cs, tm=geo["tm_row"], **common)
            wf = fnet_w[j].astype(BF16)
            bf = fnet_b[j].reshape(1, d)
            ct, st = _dft_tables(n)
            tmk = min(1024, n)
            xs = _fnet_b(ct, st, yc, ys, xs, mod_l, wf, bf, n_batch=b_, seq_len=n, row_off=0,
                         mod_ctx_row=None, tm=tmk, tk=tmk)
            if need_ctx:
                ct, st = _dft_tables(n_ctx)
                xs = _fnet_b(ct, st, yc, ys, xs, mod_l, wf, bf, n_batch=b_, seq_len=n_ctx,
                             row_off=n_lat, mod_ctx_row=b_, tm=n_ctx, tk=n_ctx)
        else:
            lam_init = 0.8 - 0.6 * math.exp(-0.3 * i)
            dh = d // DA_HEADS
            cos_t, sin_t = _rope_tables(n, LANES)
            q, k, v = _qkv(xs, mod_l, g0, attn_w_qkv[j].astype(BF16), cos_t, sin_t,
                           tm=geo["tm_row"], q_scale=(dh // 2) ** -0.5, **common)
            sg = attn_subln_g[j].reshape(1, dh)
            o = jnp.zeros((rows, d), BF16)
            fl = dict(n_batch=b_, n_heads=DA_HEADS, kc_len=n_ctx, kc_off=n_lat, lam_init=lam_init)
            o = _flash(q, k, v, attn_lambda[j], sg, o, q_len=n, q_off=0, kx_len=n,
                       tq=min(512, n), tk=min(512, n), **fl)
            if need_ctx:
                o = _flash(q, k, v, attn_lambda[j], sg, o, q_len=n_ctx, q_off=n_lat, kx_len=0,
                           tq=n_ctx, tk=n_ctx, **fl)
            xs = _proj_residual(o, attn_w_o[j].astype(BF16), xs, mod_l, tm=geo["tm_row"], **common)

        j = i // 2
        if i % 2 == 0:
            xs = _dense_ffn_layer(xs, mod_l, g1, ffn_w_gate[j], ffn_w_up[j], ffn_w_down[j], geo, rows)
        else:
            fg = final_g.reshape(1, d) if i == depth - 1 else None
            res = _moe_layer(xs, mod_l, g1, moe_w_router[j], moe_w_gate[j], moe_w_up[j],
                             moe_w_down[j], geo, rows, final_g=fg)
            if fg is not None:
                out = res
            else:
                xs = res
    if out is None:
        ms = jnp.mean(jnp.square(xs[:n_lat]), axis=-1, keepdims=True)
        out = xs[:n_lat] * lax.rsqrt(ms + NORM_EPS) * final_g
    return out.reshape(b_, n, d)
```

```python
import functools
import math

import jax
import jax.numpy as jnp
from jax import lax
from jax.experimental import pallas as pl
from jax.experimental.pallas import tpu as pltpu

F32 = jnp.float32
BF16 = jnp.bfloat16

NORM_EPS = 1e-6
LN_EPS = 1e-5
CONV_WIDTH = 31
CONV_PAD = (CONV_WIDTH - 1) // 2
CONV_HALO = 16
GRID_W = 64
F_GROUPS = 8
DA_HEADS = 8
ROPE_BASE = 10000.0
TOP_K = 2
LANES = 128
MOD_ROWS = 16
DFT_ROWS = 64

V7X_VMEM_LIMIT = 56 * 1024 * 1024


def _cparams(sem, vmem=None):
    return pltpu.CompilerParams(dimension_semantics=sem, vmem_limit_bytes=vmem)


def _norm_mod(x, g, scale, shift):
    ms = jnp.mean(x * x, axis=-1, keepdims=True)
    y = x * lax.rsqrt(ms + NORM_EPS)
    return (y * g) * (1.0 + scale) + shift


def _mod_index(tm, n_lat_rows, seq_len, ctx_row):
    n_lat_tiles = n_lat_rows // tm
    per_seq = seq_len // tm

    def f(i):
        return jnp.where(i < n_lat_tiles, i // per_seq, ctx_row)

    return f


def _mod_kernel(c_ref, w_ref, b_ref, o_ref):
    c = c_ref[...]
    sc = c * jax.nn.sigmoid(c)
    o_ref[...] = jnp.dot(sc, w_ref[...], precision=lax.Precision.HIGHEST,
                         preferred_element_type=F32) + b_ref[...]


def _modulation(c_all, w_mod, b_mod):
    depth, d, nd = w_mod.shape
    tn = 1024
    return pl.pallas_call(
        _mod_kernel,
        out_shape=jax.ShapeDtypeStruct((depth, MOD_ROWS, nd), F32),
        grid=(depth, nd // tn),
        in_specs=[
            pl.BlockSpec((MOD_ROWS, d), lambda l, j: (0, 0)),
            pl.BlockSpec((None, d, tn), lambda l, j: (l, 0, j)),
            pl.BlockSpec((None, 1, tn), lambda l, j: (l, 0, j)),
        ],
        out_specs=pl.BlockSpec((None, MOD_ROWS, tn), lambda l, j: (l, 0, j)),
        compiler_params=_cparams(("arbitrary", "arbitrary")),
        name="modulation",
    )(c_all, w_mod, b_mod.reshape(depth, 1, nd))


def _conv_in_kernel(x_ref, mod_ref, g_ref, w_ref, b_ref, u_ref):
    d = u_ref.shape[1]
    h = _norm_mod(x_ref[...], g_ref[...], mod_ref[1:2, :], mod_ref[0:1, :])
    y = jnp.dot(h.astype(BF16), w_ref[...], preferred_element_type=F32) + b_ref[...]
    u_ref[...] = (y[:, :d] * jax.nn.sigmoid(y[:, d:])).astype(u_ref.dtype)


def _conv_in(xs, mod_l, g, w_in, b_in, *, rows, n_lat_rows, seq_len, ctx_row, tm):
    d = xs.shape[1]
    midx = _mod_index(tm, n_lat_rows, seq_len, ctx_row)
    return pl.pallas_call(
        _conv_in_kernel,
        out_shape=jax.ShapeDtypeStruct((rows, d), BF16),
        grid=(rows // tm,),
        in_specs=[
            pl.BlockSpec((tm, d), lambda i: (i, 0)),
            pl.BlockSpec((None, 6, d), lambda i: (midx(i), 0, 0)),
            pl.BlockSpec((1, d), lambda i: (0, 0)),
            pl.BlockSpec((d, 2 * d), lambda i: (0, 0)),
            pl.BlockSpec((1, 2 * d), lambda i: (0, 0)),
        ],
        out_specs=pl.BlockSpec((tm, d), lambda i: (i, 0)),
        compiler_params=_cparams(("arbitrary",), V7X_VMEM_LIMIT),
        name="conv_in",
    )(xs, mod_l, g, w_in, b_in)


def _conv_out_kernel(u_ref, up_ref, un_ref, x_ref, mod_ref, wdw_ref, bdw_ref, lng_ref, lnb_ref,
                     wo_ref, bo_ref, o_ref, ubuf, cbuf, *, n_lat_tiles, tiles_per_seq):
    tm, d = x_ref.shape
    i = pl.program_id(0)
    is_ctx = i >= n_lat_tiles
    j = i % tiles_per_seq
    first = jnp.logical_or(is_ctx, j == 0)
    last = jnp.logical_or(is_ctx, j == tiles_per_seq - 1)
    ubuf[0:CONV_HALO, :] = jnp.where(first, 0.0, up_ref[...].astype(F32))
    ubuf[CONV_HALO:CONV_HALO + tm, :] = u_ref[...].astype(F32)
    ubuf[CONV_HALO + tm:, :] = jnp.where(last, 0.0, un_ref[...].astype(F32))

    for c in range(d // LANES):
        cols = slice(c * LANES, (c + 1) * LANES)
        acc = jnp.zeros((tm, LANES), F32)
        for k in range(CONV_WIDTH):
            r0 = CONV_HALO - CONV_PAD + k
            acc = acc + wdw_ref[k:k + 1, cols] * ubuf[r0:r0 + tm, cols]
        cbuf[:, cols] = acc + bdw_ref[:, cols]
    v = cbuf[...]
    mu = jnp.mean(v, axis=-1, keepdims=True)
    vc = v - mu
    var = jnp.mean(vc * vc, axis=-1, keepdims=True)
    y = vc * lax.rsqrt(var + LN_EPS) * lng_ref[...] + lnb_ref[...]
    y = y * jax.nn.sigmoid(y)
    out = jnp.dot(y.astype(BF16), wo_ref[...], preferred_element_type=F32) + bo_ref[...]
    o_ref[...] = x_ref[...] + mod_ref[2:3, :] * out


def _conv_out(u, xs, mod_l, w_dw, b_dw, ln_g, ln_b, w_out, b_out, *, rows, n_lat_rows, seq_len,
              ctx_row, tm):
    d = xs.shape[1]
    n_tiles = rows // tm
    n_lat_tiles = n_lat_rows // tm
    hpt = tm // CONV_HALO
    n_halo = u.shape[0] // CONV_HALO
    midx = _mod_index(tm, n_lat_rows, seq_len, ctx_row)
    kern = functools.partial(_conv_out_kernel, n_lat_tiles=n_lat_tiles, tiles_per_seq=seq_len // tm)
    return pl.pallas_call(
        kern,
        out_shape=jax.ShapeDtypeStruct(xs.shape, F32),
        grid=(n_tiles,),
        in_specs=[
            pl.BlockSpec((tm, d), lambda i: (i, 0)),
            pl.BlockSpec((CONV_HALO, d), lambda i: (jnp.maximum(i * hpt - 1, 0), 0)),
            pl.BlockSpec((CONV_HALO, d), lambda i: (jnp.minimum((i + 1) * hpt, n_halo - 1), 0)),
            pl.BlockSpec((tm, d), lambda i: (i, 0)),
            pl.BlockSpec((None, 6, d), lambda i: (midx(i), 0, 0)),
            pl.BlockSpec((CONV_WIDTH, d), lambda i: (0, 0)),
            pl.BlockSpec((1, d), lambda i: (0, 0)),
            pl.BlockSpec((1, d), lambda i: (0, 0)),
            pl.BlockSpec((1, d), lambda i: (0, 0)),
            pl.BlockSpec((d, d), lambda i: (0, 0)),
            pl.BlockSpec((1, d), lambda i: (0, 0)),
        ],
        out_specs=pl.BlockSpec((tm, d), lambda i: (i, 0)),
        scratch_shapes=[pltpu.VMEM((tm + 2 * CONV_HALO, d), F32), pltpu.VMEM((tm, d), F32)],
        input_output_aliases={3: 0},
        compiler_params=_cparams(("arbitrary",), V7X_VMEM_LIMIT),
        name="conv_out",
    )(u, u, u, xs, mod_l, w_dw, b_dw, ln_g, ln_b, w_out, b_out)


def _ffn_kernel(te_ref, nv_ref, *refs, fuse_norm, n_f):
    if fuse_norm:
        x_ref, mod_ref, g_ref, wg_ref, wu_ref, wd_ref, o_ref, h_scr, acc_scr = refs
    else:
        x_ref, wg_ref, wu_ref, wd_ref, o_ref, h_scr, acc_scr = refs
    del te_ref
    j = pl.program_id(0)
    f = pl.program_id(1)
    valid = j < nv_ref[0]

    @pl.when(valid)
    def _():
        @pl.when(f == 0)
        def _():
            if fuse_norm:
                h = _norm_mod(x_ref[...], g_ref[...], mod_ref[4:5, :], mod_ref[3:4, :])
            else:
                h = x_ref[...]
            h_scr[...] = h.astype(BF16)

        h = h_scr[...]
        gt = jnp.dot(h, wg_ref[...], preferred_element_type=F32)
        up = jnp.dot(h, wu_ref[...], preferred_element_type=F32)
        a = (gt * jax.nn.sigmoid(gt) * up).astype(BF16)
        part = jnp.dot(a, wd_ref[...], preferred_element_type=F32)

        @pl.when(f == 0)
        def _():
            acc_scr[...] = part

        @pl.when(f > 0)
        def _():
            acc_scr[...] += part

        @pl.when(f == n_f - 1)
        def _():
            if fuse_norm:
                o_ref[...] = x_ref[...] + mod_ref[5:6, :] * acc_scr[...]
            else:
                o_ref[...] = acc_scr[...]

    @pl.when(jnp.logical_and(jnp.logical_not(valid), f == n_f - 1))
    def _():
        o_ref[...] = jnp.zeros_like(o_ref)


def _ffn(x, tile_expert, n_valid, wg, wu, wd, *, rows, tm, tf, mod_l=None, g=None, midx=None):
    d = x.shape[1]
    f_dim = wg.shape[2]
    n_f = f_dim // tf
    fuse_norm = mod_l is not None

    def fsel(j, f, nv):
        return jnp.where(j < nv[0], f, 0)

    in_specs = [pl.BlockSpec((tm, d), lambda j, f, te, nv: (j, 0))]
    args = [x]
    if fuse_norm:
        in_specs += [pl.BlockSpec((None, 6, d), lambda j, f, te, nv: (midx(j), 0, 0)),
                     pl.BlockSpec((1, d), lambda j, f, te, nv: (0, 0))]
        args += [mod_l, g]
    in_specs += [
        pl.BlockSpec((None, d, tf), lambda j, f, te, nv: (te[j], 0, fsel(j, f, nv))),
        pl.BlockSpec((None, d, tf), lambda j, f, te, nv: (te[j], 0, fsel(j, f, nv))),
        pl.BlockSpec((None, tf, d), lambda j, f, te, nv: (te[j], fsel(j, f, nv), 0)),
    ]
    args += [wg, wu, wd]
    kern = functools.partial(_ffn_kernel, fuse_norm=fuse_norm, n_f=n_f)
    return pl.pallas_call(
        kern,
        out_shape=jax.ShapeDtypeStruct(x.shape, F32),
        grid_spec=pltpu.PrefetchScalarGridSpec(
            num_scalar_prefetch=2,
            grid=(rows // tm, n_f),
            in_specs=in_specs,
            out_specs=pl.BlockSpec((tm, d), lambda j, f, te, nv: (j, 0)),
            scratch_shapes=[pltpu.VMEM((tm, d), BF16), pltpu.VMEM((tm, d), F32)],
        ),
        input_output_aliases=({2: 0} if fuse_norm else {}),
        compiler_params=_cparams(("arbitrary", "arbitrary"), V7X_VMEM_LIMIT),
        name="ffn_dense" if fuse_norm else "ffn_grouped",
    )(tile_expert, n_valid, *args)


def _router_kernel(x_ref, mod_ref, g_ref, wr_ref, h_ref, meta_ref, gw_ref, cnt_ref, carry, *, n_exp):
    tm = x_ref.shape[0]
    i = pl.program_id(0)

    @pl.when(i == 0)
    def _():
        carry[...] = jnp.zeros_like(carry)

    h = _norm_mod(x_ref[...], g_ref[...], mod_ref[4:5, :], mod_ref[3:4, :])
    h_ref[...] = h
    logits = jnp.dot(h, wr_ref[...], precision=lax.Precision.HIGHEST, preferred_element_type=F32)
    lane = lax.broadcasted_iota(jnp.int32, (tm, LANES), 1).astype(F32)
    neg = jnp.float32(-jnp.inf)
    lg = jnp.where(lane < n_exp, logits, neg)
    m1 = jnp.max(lg, axis=-1, keepdims=True)
    i1 = jnp.min(jnp.where(lg == m1, lane, float(LANES)), axis=-1, keepdims=True)
    lg2 = jnp.where(lane == i1, neg, lg)
    m2 = jnp.max(lg2, axis=-1, keepdims=True)
    i2 = jnp.min(jnp.where(lg2 == m2, lane, float(LANES)), axis=-1, keepdims=True)
    e2 = jnp.exp(m2 - m1)
    w1 = 1.0 / (1.0 + e2)
    w2 = e2 * w1
    sel1 = lane == i1
    sel2 = lane == i2
    onehot = jnp.where(jnp.logical_or(sel1, sel2), 1.0, 0.0)
    rr = lax.broadcasted_iota(jnp.int32, (tm, tm), 0)
    cc = lax.broadcasted_iota(jnp.int32, (tm, tm), 1)
    tri = jnp.where(rr > cc, 1.0, 0.0).astype(BF16)
    cum = jnp.dot(tri, onehot.astype(BF16), preferred_element_type=F32) + carry[...]
    r1 = jnp.sum(jnp.where(sel1, cum, 0.0), axis=-1, keepdims=True)
    r2 = jnp.sum(jnp.where(sel2, cum, 0.0), axis=-1, keepdims=True)
    carry[...] += jnp.sum(onehot, axis=0, keepdims=True)
    meta = jnp.where(lane == 0, i1, jnp.where(lane == 1, i2, jnp.where(lane == 2, r1,
                     jnp.where(lane == 3, r2, 0.0))))
    meta_ref[...] = meta.astype(jnp.int32)
    gw_ref[...] = jnp.where(lane == 0, w1, jnp.where(lane == 1, w2, 0.0))
    cnt_ref[...] = jnp.broadcast_to(carry[...], cnt_ref.shape)


def _router(xs, mod_l, g, w_router_pad, *, rows, n_lat_rows, seq_len, ctx_row, tm, n_exp):
    d = xs.shape[1]
    midx = _mod_index(tm, n_lat_rows, seq_len, ctx_row)
    kern = functools.partial(_router_kernel, n_exp=n_exp)
    return pl.pallas_call(
        kern,
        out_shape=[jax.ShapeDtypeStruct((rows, d), F32),
                   jax.ShapeDtypeStruct((rows, LANES), jnp.int32),
                   jax.ShapeDtypeStruct((rows, LANES), F32),
                   jax.ShapeDtypeStruct((8, LANES), F32)],
        grid=(rows // tm,),
        in_specs=[
            pl.BlockSpec((tm, d), lambda i: (i, 0)),
            pl.BlockSpec((None, 6, d), lambda i: (midx(i), 0, 0)),
            pl.BlockSpec((1, d), lambda i: (0, 0)),
            pl.BlockSpec((d, LANES), lambda i: (0, 0)),
        ],
        out_specs=[pl.BlockSpec((tm, d), lambda i: (i, 0)),
                   pl.BlockSpec((tm, LANES), lambda i: (i, 0)),
                   pl.BlockSpec((tm, LANES), lambda i: (i, 0)),
                   pl.BlockSpec((8, LANES), lambda i: (0, 0))],
        scratch_shapes=[pltpu.VMEM((1, LANES), F32)],
        compiler_params=_cparams(("arbitrary",), V7X_VMEM_LIMIT),
        name="router",
    )(xs, mod_l, g, w_router_pad)


def _gather_kernel(idx_ref, src_ref, o_ref, sem):
    ch = o_ref.shape[0]

    def row_copy(s, i):
        return pltpu.make_async_copy(src_ref.at[pl.ds(s, 1)], o_ref.at[pl.ds(i, 1)], sem)

    def start(i, carry):
        row_copy(idx_ref[0, i], i).start()
        return carry

    lax.fori_loop(0, ch, start, 0, unroll=8)

    def wait(i, carry):
        row_copy(0, i).wait()
        return carry

    lax.fori_loop(0, ch, wait, 0, unroll=8)


def _gather_rows(src, idx, *, ch):
    n = idx.shape[0]
    n_chunks = n // ch
    d = src.shape[1]
    return pl.pallas_call(
        _gather_kernel,
        out_shape=jax.ShapeDtypeStruct((n, d), src.dtype),
        grid=(n_chunks,),
        in_specs=[
            pl.BlockSpec((None, 1, ch), lambda i: (i, 0, 0), memory_space=pltpu.SMEM),
            pl.BlockSpec(memory_space=pl.ANY),
        ],
        out_specs=pl.BlockSpec((ch, d), lambda i: (i, 0)),
        scratch_shapes=[pltpu.SemaphoreType.DMA],
        compiler_params=_cparams(("arbitrary",), V7X_VMEM_LIMIT),
        name="gather_rows",
    )(idx.reshape(n_chunks, 1, ch), src)


def _combine_kernel(x_ref, y1_ref, y2_ref, gw_ref, mod_ref, *rest, final):
    if final:
        fg_ref, o_ref = rest
    else:
        (o_ref,) = rest
    gw = gw_ref[...]
    y = gw[:, 0:1] * y1_ref[...] + gw[:, 1:2] * y2_ref[...]
    xn = x_ref[...] + mod_ref[5:6, :] * y
    if final:
        ms = jnp.mean(xn * xn, axis=-1, keepdims=True)
        xn = xn * lax.rsqrt(ms + NORM_EPS) * fg_ref[...]
    o_ref[...] = xn


def _combine(xs, ypair, gw, mod_l, *, rows, n_lat_rows, seq_len, ctx_row, tm, final_g=None):
    d = xs.shape[1]
    n_tiles = rows // tm
    midx = _mod_index(tm, n_lat_rows, seq_len, ctx_row)
    final = final_g is not None
    in_specs = [
        pl.BlockSpec((tm, d), lambda i: (i, 0)),
        pl.BlockSpec((tm, d), lambda i: (i, 0)),
        pl.BlockSpec((tm, d), lambda i: (i + n_tiles, 0)),
        pl.BlockSpec((tm, LANES), lambda i: (i, 0)),
        pl.BlockSpec((None, 6, d), lambda i: (midx(i), 0, 0)),
    ]
    args = [xs, ypair, ypair, gw, mod_l]
    if final:
        in_specs.append(pl.BlockSpec((1, d), lambda i: (0, 0)))
        args.append(final_g)
    out_rows = rows if final else xs.shape[0]
    return pl.pallas_call(
        functools.partial(_combine_kernel, final=final),
        out_shape=jax.ShapeDtypeStruct((out_rows, d), F32),
        grid=(n_tiles,),
        in_specs=in_specs,
        out_specs=pl.BlockSpec((tm, d), lambda i: (i, 0)),
        input_output_aliases=({} if final else {0: 0}),
        compiler_params=_cparams(("arbitrary",), V7X_VMEM_LIMIT),
        name="moe_combine",
    )(*args)


def _fnet_a_kernel(x_ref, mod_ref, g_ref, cs_ref, yc_ref, ys_ref, *, n_groups):
    h = _norm_mod(x_ref[...], g_ref[...], mod_ref[1:2, :], mod_ref[0:1, :]).astype(BF16)
    gd = cs_ref.shape[0]
    for gi in range(n_groups):
        y = jnp.dot(h[:, gi * gd:(gi + 1) * gd], cs_ref[...], preferred_element_type=F32)
        yc_ref[:, gi * gd:(gi + 1) * gd] = y[:, :gd].astype(yc_ref.dtype)
        ys_ref[:, gi * gd:(gi + 1) * gd] = y[:, gd:].astype(ys_ref.dtype)


def _fnet_a(xs, mod_l, g, cs, *, rows, n_lat_rows, seq_len, ctx_row, tm):
    d = xs.shape[1]
    gd = cs.shape[0]
    midx = _mod_index(tm, n_lat_rows, seq_len, ctx_row)
    return pl.pallas_call(
        functools.partial(_fnet_a_kernel, n_groups=d // gd),
        out_shape=[jax.ShapeDtypeStruct((rows, d), BF16)] * 2,
        grid=(rows // tm,),
        in_specs=[
            pl.BlockSpec((tm, d), lambda i: (i, 0)),
            pl.BlockSpec((None, 6, d), lambda i: (midx(i), 0, 0)),
            pl.BlockSpec((1, d), lambda i: (0, 0)),
            pl.BlockSpec((gd, 2 * gd), lambda i: (0, 0)),
        ],
        out_specs=[pl.BlockSpec((tm, d), lambda i: (i, 0))] * 2,
        compiler_params=_cparams(("arbitrary",), V7X_VMEM_LIMIT),
        name="fnet_group_dft",
    )(xs, mod_l, g, cs)


def _dft_table_kernel(ac_ref, as_ref, bc_ref, bs_ref, c_ref, sn_ref):
    ac, as_ = ac_ref[...], as_ref[...]
    bc, bs = bc_ref[...], bs_ref[...]
    c_ref[...] = (bc * ac - bs * as_).astype(c_ref.dtype)
    sn_ref[...] = (-(bs * ac + bc * as_)).astype(sn_ref.dtype)


def _dft_tables(n):
    r = DFT_ROWS
    k = jnp.arange(n, dtype=jnp.int32)[None, :]
    j1 = jnp.arange(r, dtype=jnp.int32)[:, None]
    j0 = (jnp.arange(n // r, dtype=jnp.int32) * r)[:, None]
    ang1 = ((j1 * k) % n).astype(F32) * (2.0 * math.pi / n)
    ang0 = ((j0 * k) % n).astype(F32) * (2.0 * math.pi / n)
    scale = 1.0 / math.sqrt(n)
    ac, as_ = jnp.cos(ang1), jnp.sin(ang1)
    bc = (jnp.cos(ang0) * scale).reshape(n // r, 1, n)
    bs = (jnp.sin(ang0) * scale).reshape(n // r, 1, n)
    return pl.pallas_call(
        _dft_table_kernel,
        out_shape=[jax.ShapeDtypeStruct((n, n), BF16)] * 2,
        grid=(n // r,),
        in_specs=[
            pl.BlockSpec((r, n), lambda i: (0, 0)),
            pl.BlockSpec((r, n), lambda i: (0, 0)),
            pl.BlockSpec((None, 1, n), lambda i: (i, 0, 0)),
            pl.BlockSpec((None, 1, n), lambda i: (i, 0, 0)),
        ],
        out_specs=[pl.BlockSpec((r, n), lambda i: (i, 0))] * 2,
        compiler_params=_cparams(("arbitrary",)),
        name="dft_tables",
    )(ac, as_, bc, bs)


def _fnet_b_kernel(c_ref, sn_ref, yc_ref, ys_ref, x_ref, mod_ref, wf_ref, bf_ref, o_ref, acc, *, n_k):
    k = pl.program_id(2)
    part = (jnp.dot(c_ref[...], yc_ref[...], preferred_element_type=F32)
            + jnp.dot(sn_ref[...], ys_ref[...], preferred_element_type=F32))

    @pl.when(k == 0)
    def _():
        acc[...] = part

    @pl.when(k > 0)
    def _():
        acc[...] += part

    @pl.when(k == n_k - 1)
    def _():
        z = acc[...].astype(BF16)
        out = jnp.dot(z, wf_ref[...], preferred_element_type=F32) + bf_ref[...]
        o_ref[...] = x_ref[...] + mod_ref[2:3, :] * out


def _fnet_b(ctab, stab, yc, ys, xs, mod_l, wf, bf, *, n_batch, seq_len, row_off, mod_ctx_row, tm, tk):
    d = xs.shape[1]
    n_i = seq_len // tm
    n_k = seq_len // tk
    off_m = row_off // tm
    off_k = row_off // tk

    def mrow(b):
        return b if mod_ctx_row is None else mod_ctx_row

    return pl.pallas_call(
        functools.partial(_fnet_b_kernel, n_k=n_k),
        out_shape=jax.ShapeDtypeStruct(xs.shape, F32),
        grid=(n_batch, n_i, n_k),
        in_specs=[
            pl.BlockSpec((tm, tk), lambda b, i, k: (i, k)),
            pl.BlockSpec((tm, tk), lambda b, i, k: (i, k)),
            pl.BlockSpec((tk, d), lambda b, i, k: (off_k + b * n_k + k, 0)),
            pl.BlockSpec((tk, d), lambda b, i, k: (off_k + b * n_k + k, 0)),
            pl.BlockSpec((tm, d), lambda b, i, k: (off_m + b * n_i + i, 0)),
            pl.BlockSpec((None, 6, d), lambda b, i, k: (mrow(b), 0, 0)),
            pl.BlockSpec((d, d), lambda b, i, k: (0, 0)),
            pl.BlockSpec((1, d), lambda b, i, k: (0, 0)),
        ],
        out_specs=pl.BlockSpec((tm, d), lambda b, i, k: (off_m + b * n_i + i, 0)),
        scratch_shapes=[pltpu.VMEM((tm, d), F32)],
        input_output_aliases={4: 0},
        compiler_params=_cparams(("arbitrary", "arbitrary", "arbitrary"), V7X_VMEM_LIMIT),
        name="fnet_seq_dft",
    )(ctab, stab, yc, ys, xs, mod_l, wf, bf)


def _qkv_kernel(x_ref, mod_ref, g_ref, w_ref, cos_ref, sin_ref, q_ref, k_ref, v_ref, *,
                n_lat_tiles, q_scale):
    tm, d = x_ref.shape
    i = pl.program_id(0)
    is_ctx = i >= n_lat_tiles
    h = _norm_mod(x_ref[...], g_ref[...], mod_ref[1:2, :], mod_ref[0:1, :])
    y = jnp.dot(h.astype(BF16), w_ref[...], preferred_element_type=F32)
    tw = cos_ref.shape[1]
    cos = jnp.where(is_ctx, 1.0, cos_ref[...])
    sin = jnp.where(is_ctx, 0.0, sin_ref[...])
    lane = lax.broadcasted_iota(jnp.int32, (tm, tw), 1)
    half = tw // 8
    lo = (lane % (2 * half)) < half

    def rope(t):
        rot = jnp.where(lo, -pltpu.roll(t, tw - half, 1), pltpu.roll(t, half, 1))
        return t * cos + rot * sin

    for hd in range(d // tw):
        cols = slice(hd * tw, (hd + 1) * tw)
        q_ref[:, cols] = (rope(y[:, hd * tw:(hd + 1) * tw]) * q_scale).astype(q_ref.dtype)
        k_ref[:, cols] = rope(y[:, d + hd * tw:d + (hd + 1) * tw]).astype(k_ref.dtype)
    v_ref[...] = y[:, 2 * d:].astype(v_ref.dtype)


def _qkv(xs, mod_l, g, w_qkv, cos_t, sin_t, *, rows, n_lat_rows, seq_len, ctx_row, tm, q_scale):
    d = xs.shape[1]
    midx = _mod_index(tm, n_lat_rows, seq_len, ctx_row)
    per_seq = seq_len // tm
    n_lat_tiles = n_lat_rows // tm
    tw = cos_t.shape[1]

    def pos(i):
        return jnp.where(i < n_lat_tiles, i % per_seq, 0)

    kern = functools.partial(_qkv_kernel, n_lat_tiles=n_lat_tiles, q_scale=q_scale)
    return pl.pallas_call(
        kern,
        out_shape=[jax.ShapeDtypeStruct((rows, d), BF16)] * 3,
        grid=(rows // tm,),
        in_specs=[
            pl.BlockSpec((tm, d), lambda i: (i, 0)),
            pl.BlockSpec((None, 6, d), lambda i: (midx(i), 0, 0)),
            pl.BlockSpec((1, d), lambda i: (0, 0)),
            pl.BlockSpec((d, 3 * d), lambda i: (0, 0)),
            pl.BlockSpec((tm, tw), lambda i: (pos(i), 0)),
            pl.BlockSpec((tm, tw), lambda i: (pos(i), 0)),
        ],
        out_specs=[pl.BlockSpec((tm, d), lambda i: (i, 0))] * 3,
        compiler_params=_cparams(("arbitrary",), V7X_VMEM_LIMIT),
        name="attn_qkv",
    )(xs, mod_l, g, w_qkv, cos_t, sin_t)


def _dot_nt(a, b):
    return lax.dot_general(a, b, (((1,), (1,)), ((), ())), preferred_element_type=F32)


def _flash_kernel(lam_ref, sg_ref, q_ref, kc_ref, vc_ref, *rest, tk, n_kx, lam_init):
    if n_kx:
        kx_ref, vx_ref, o_ref = rest
    else:
        (o_ref,) = rest
    tq, dh = q_ref.shape
    q = q_ref[...]
    lane = lax.broadcasted_iota(jnp.int32, (tq, dh), 1)
    zero = jnp.zeros_like(q)
    qa = jnp.where(lane < dh // 2, q, zero)
    qb = jnp.where(lane >= dh // 2, q, zero)

    def scores(k):
        return _dot_nt(qa, k), _dot_nt(qb, k)

    def init(k, v):
        out = []
        for s in scores(k):
            m = jnp.max(s, axis=-1, keepdims=True)
            p = jnp.exp(s - m)
            l = jnp.sum(p, axis=-1, keepdims=True)
            out += [m, l, jnp.dot(p.astype(BF16), v, preferred_element_type=F32)]
        return tuple(out)

    def update(k, v, carry):
        out = []
        for s, (m, l, acc) in zip(scores(k), (carry[0:3], carry[3:6])):
            m_new = jnp.maximum(m, jnp.max(s, axis=-1, keepdims=True))
            alpha = jnp.exp(m - m_new)
            p = jnp.exp(s - m_new)
            l = alpha * l + jnp.sum(p, axis=-1, keepdims=True)
            acc = alpha * acc + jnp.dot(p.astype(BF16), v, preferred_element_type=F32)
            out += [m_new, l, acc]
        return tuple(out)

    carry = init(kc_ref[...], vc_ref[...])
    if n_kx:
        def body(t, c):
            r0 = pl.multiple_of(t * tk, tk)
            return update(kx_ref[pl.ds(r0, tk), :], vx_ref[pl.ds(r0, tk), :], c)

        carry = lax.fori_loop(0, n_kx, body, carry)
    m1, l1, a1, m2, l2, a2 = carry
    lv = lam_ref[...]
    lam = (jnp.exp(jnp.sum(lv[0:1, :] * lv[1:2, :], axis=-1, keepdims=True))
           - jnp.exp(jnp.sum(lv[2:3, :] * lv[3:4, :], axis=-1, keepdims=True)) + lam_init)
    o = a1 / l1 - lam * (a2 / l2)
    ms = jnp.mean(o * o, axis=-1, keepdims=True)
    o = o * lax.rsqrt(ms + NORM_EPS) * sg_ref[...] * (1.0 - lam_init)
    o_ref[...] = o.astype(o_ref.dtype)


def _flash(q, k, v, lam_vecs, subln_g, out_init, *, n_batch, n_heads, q_len, q_off, kc_len, kc_off,
           kx_len, tq, tk, lam_init):
    dh = q.shape[1] // n_heads
    n_q = q_len // tq
    qo = q_off // tq
    kco = kc_off // kc_len
    n_kx = kx_len // tk if kx_len else 0
    in_specs = [
        pl.BlockSpec(lam_vecs.shape, lambda b, h, i: (0, 0)),
        pl.BlockSpec((1, dh), lambda b, h, i: (0, 0)),
        pl.BlockSpec((tq, dh), lambda b, h, i: (qo + b * n_q + i, h)),
        pl.BlockSpec((kc_len, dh), lambda b, h, i: (kco + b, h)),
        pl.BlockSpec((kc_len, dh), lambda b, h, i: (kco + b, h)),
    ]
    args = [lam_vecs, subln_g, q, k, v]
    if n_kx:
        in_specs += [pl.BlockSpec((kx_len, dh), lambda b, h, i: (b, h)),
                     pl.BlockSpec((kx_len, dh), lambda b, h, i: (b, h))]
        args += [k, v]
    in_specs.append(pl.BlockSpec(memory_space=pl.ANY))
    args.append(out_init)
    kern = functools.partial(_flash_kernel_alias, tk=tk, n_kx=n_kx, lam_init=lam_init)
    return pl.pallas_call(
        kern,
        out_shape=jax.ShapeDtypeStruct(out_init.shape, out_init.dtype),
        grid=(n_batch, n_heads, n_q),
        in_specs=in_specs,
        out_specs=pl.BlockSpec((tq, dh), lambda b, h, i: (qo + b * n_q + i, h)),
        input_output_aliases={len(args) - 1: 0},
        compiler_params=_cparams(("arbitrary", "arbitrary", "arbitrary"), V7X_VMEM_LIMIT),
        name="diff_attn_latent" if n_kx else "diff_attn_ctx",
    )(*args)


def _flash_kernel_alias(*refs, tk, n_kx, lam_init):
    _flash_kernel(*refs[:-2], refs[-1], tk=tk, n_kx=n_kx, lam_init=lam_init)


def _proj_residual_kernel(a_ref, w_ref, x_ref, mod_ref, o_ref):
    out = jnp.dot(a_ref[...], w_ref[...], preferred_element_type=F32)
    o_ref[...] = x_ref[...] + mod_ref[2:3, :] * out


def _proj_residual(a, w, xs, mod_l, *, rows, n_lat_rows, seq_len, ctx_row, tm):
    d = xs.shape[1]
    midx = _mod_index(tm, n_lat_rows, seq_len, ctx_row)
    return pl.pallas_call(
        _proj_residual_kernel,
        out_shape=jax.ShapeDtypeStruct(xs.shape, F32),
        grid=(rows // tm,),
        in_specs=[
            pl.BlockSpec((tm, d), lambda i: (i, 0)),
            pl.BlockSpec((d, d), lambda i: (0, 0)),
            pl.BlockSpec((tm, d), lambda i: (i, 0)),
            pl.BlockSpec((None, 6, d), lambda i: (midx(i), 0, 0)),
        ],
        out_specs=pl.BlockSpec((tm, d), lambda i: (i, 0)),
        input_output_aliases={2: 0},
        compiler_params=_cparams(("arbitrary",), V7X_VMEM_LIMIT),
        name="attn_out_proj",
    )(a, w, xs, mod_l)


def _rope_tables(n_tokens, width):
    rope_axis_dim = 32
    freqs = rope_axis_dim // 2
    rows = n_tokens // GRID_W
    row = jnp.repeat(jnp.arange(rows, dtype=F32), GRID_W)
    col = jnp.tile(jnp.arange(GRID_W, dtype=F32), rows)
    inv_freq = 1.0 / (ROPE_BASE ** (jnp.arange(freqs, dtype=F32) * 2.0 / rope_axis_dim))
    ang = jnp.stack([row[:, None] * inv_freq, col[:, None] * inv_freq], axis=1)
    ang = jnp.stack([ang, ang], axis=2).reshape(n_tokens, 4 * freqs)
    ang = jnp.tile(ang, (1, width // (4 * freqs)))
    return jnp.cos(ang), jnp.sin(ang)


def _dense_ffn_layer(xs, mod_l, g, wg, wu, wd, geo, rows):
    tm = geo["tm_ffn"]
    n_tiles = rows // tm
    te = jnp.zeros((n_tiles,), jnp.int32)
    nv = jnp.full((1,), n_tiles, jnp.int32)
    midx = _mod_index(tm, geo["n_lat_rows"], geo["seq_len"], geo["ctx_row"])
    return _ffn(xs, te, nv, wg[None].astype(BF16), wu[None].astype(BF16), wd[None].astype(BF16),
                rows=rows, tm=tm, tf=geo["tf"], mod_l=mod_l, g=g, midx=midx)


def _moe_layer(xs, mod_l, g, w_router, wg, wu, wd, geo, rows, final_g=None):
    d = xs.shape[1]
    n_exp = w_router.shape[1]
    tm = geo["tm_ffn"]
    common = dict(rows=rows, n_lat_rows=geo["n_lat_rows"], seq_len=geo["seq_len"], ctx_row=geo["ctx_row"])
    wr = jnp.zeros((d, LANES), F32).at[:, :n_exp].set(w_router)
    h, meta, gw, cnt = _router(xs, mod_l, g, wr, tm=geo["tm_router"], n_exp=n_exp, **common)
    counts = cnt[0, :n_exp].astype(jnp.int32)
    padded = ((counts + tm - 1) // tm) * tm
    ends = jnp.cumsum(padded)
    offs = ends - padded
    n_sorted_tiles = (TOP_K * rows) // tm + n_exp
    tile_start = jnp.arange(n_sorted_tiles, dtype=jnp.int32) * tm
    tile_expert = jnp.minimum(jnp.sum(tile_start[:, None] >= ends[None, :], axis=1), n_exp - 1).astype(jnp.int32)
    n_valid = (ends[-1] // tm).astype(jnp.int32).reshape(1)
    e1, e2, r1, r2 = meta[:, 0], meta[:, 1], meta[:, 2], meta[:, 3]
    dest = jnp.concatenate([offs[e1] + r1, offs[e2] + r2]).astype(jnp.int32)
    token = jnp.tile(jnp.arange(rows, dtype=jnp.int32), TOP_K)
    src_of_sorted = jnp.zeros((n_sorted_tiles * tm,), jnp.int32).at[dest].set(token)
    hs = _gather_rows(h, src_of_sorted, ch=tm)
    ys = _ffn(hs, tile_expert, n_valid, wg.astype(BF16), wu.astype(BF16), wd.astype(BF16),
              rows=n_sorted_tiles * tm, tm=tm, tf=geo["tf"])
    ypair = _gather_rows(ys, dest, ch=geo["perm_chunk"])
    return _combine(xs, ypair, gw, mod_l, tm=geo["tm_row"], final_g=final_g, **common)


def kernel(x, c, ctx, c_ctx, w_mod, b_mod, norm_g, conv_w_in, conv_b_in, conv_w_dw, conv_b_dw,
           conv_ln_g, conv_ln_b, conv_w_out, conv_b_out, fnet_w, fnet_b, attn_w_qkv, attn_lambda,
           attn_subln_g, attn_w_o, ffn_w_gate, ffn_w_up, ffn_w_down, moe_w_router, moe_w_gate,
           moe_w_up, moe_w_down, final_g):
    b_, n, d = x.shape
    n_ctx = ctx.shape[1]
    depth = w_mod.shape[0]
    n_lat = b_ * n
    n_all = n_lat + b_ * n_ctx
    assert b_ < MOD_ROWS and d % LANES == 0 and n % GRID_W == 0 and n % n_ctx == 0
    assert d // DA_HEADS == LANES
    geo = dict(n_lat_rows=n_lat, seq_len=n, ctx_row=b_,
               tm_row=min(512, n_ctx * b_, n), tm_ffn=min(1024, n_ctx * b_, n),
               tm_router=min(512, n_ctx * b_, n), tf=min(512, ffn_w_gate.shape[2]),
               tm_conv=n_ctx, perm_chunk=math.gcd(2048, TOP_K * n_lat, TOP_K * n_all))

    xs = jnp.concatenate([x.reshape(n_lat, d), ctx.reshape(b_ * n_ctx, d)], axis=0)
    c_all = jnp.zeros((MOD_ROWS, d), F32).at[:b_].set(c).at[b_].set(c_ctx)
    mod = _modulation(c_all, w_mod, b_mod).reshape(depth, MOD_ROWS, 6, d)

    out = None
    for i in range(depth):
        need_ctx = i < depth - 1
        rows = n_all if need_ctx else n_lat
        kind = i % 3
        mod_l = mod[i]
        common = dict(rows=rows, n_lat_rows=n_lat, seq_len=n, ctx_row=b_)
        g0 = norm_g[i, 0].reshape(1, d)
        g1 = norm_g[i, 1].reshape(1, d)
        j = i // 3
        if kind == 0:
            u = _conv_in(xs, mod_l, g0, conv_w_in[j].astype(BF16), conv_b_in[j].reshape(1, 2 * d),
                         tm=geo["tm_row"], **common)
            xs = _conv_out(u, xs, mod_l, conv_w_dw[j], conv_b_dw[j].reshape(1, d),
                           conv_ln_g[j].reshape(1, d), conv_ln_b[j].reshape(1, d),
                           conv_w_out[j].astype(BF16), conv_b_out[j].reshape(1, d),
                           tm=geo["tm_conv"], **common)
        elif kind == 1:
            gd = d // F_GROUPS
            kk = jnp.arange(gd, dtype=jnp.int32)
            ang = ((kk[:, None] * kk[None, :]) % gd).astype(F32) * (2.0 * math.pi / gd)
            cs = (jnp.concatenate([jnp.cos(ang), jnp.sin(ang)], axis=1) / math.sqrt(gd)).astype(BF16)
            yc, ys = _fnet_a(xs, mod_l, g0, cs, tm=geo["tm_row"], **common)
            wf = fnet_w[j].astype(BF16)
            bf = fnet_b[j].reshape(1, d)
            ct, st = _dft_tables(n)
            tmk = min(1024, n)
            xs = _fnet_b(ct, st, yc, ys, xs, mod_l, wf, bf, n_batch=b_, seq_len=n, row_off=0,
                         mod_ctx_row=None, tm=tmk, tk=tmk)
            if need_ctx:
                ct, st = _dft_tables(n_ctx)
                xs = _fnet_b(ct, st, yc, ys, xs, mod_l, wf, bf, n_batch=b_, seq_len=n_ctx,
                             row_off=n_lat, mod_ctx_row=b_, tm=n_ctx, tk=n_ctx)
        else:
            lam_init = 0.8 - 0.6 * math.exp(-0.3 * i)
            dh = d // DA_HEADS
            cos_t, sin_t = _rope_tables(n, LANES)
            q, k, v = _qkv(xs, mod_l, g0, attn_w_qkv[j].astype(BF16), cos_t, sin_t,
                           tm=geo["tm_row"], q_scale=(dh // 2) ** -0.5, **common)
            sg = attn_subln_g[j].reshape(1, dh)
            o = jnp.zeros((rows, d), BF16)
            fl = dict(n_batch=b_, n_heads=DA_HEADS, kc_len=n_ctx, kc_off=n_lat, lam_init=lam_init)
            o = _flash(q, k, v, attn_lambda[j], sg, o, q_len=n, q_off=0, kx_len=n,
                       tq=min(512, n), tk=min(512, n), **fl)
            if need_ctx:
                o = _flash(q, k, v, attn_lambda[j], sg, o, q_len=n_ctx, q_off=n_lat, kx_len=0,
                           tq=n_ctx, tk=n_ctx, **fl)
            xs = _proj_residual(o, attn_w_o[j].astype(BF16), xs, mod_l, tm=geo["tm_row"], **common)

        j = i // 2
        if i % 2 == 0:
            xs = _dense_ffn_layer(xs, mod_l, g1, ffn_w_gate[j], ffn_w_up[j], ffn_w_down[j], geo, rows)
        else:
            fg = final_g.reshape(1, d) if i == depth - 1 else None
            res = _moe_layer(xs, mod_l, g1, moe_w_router[j], moe_w_gate[j], moe_w_up[j],
                             moe_w_down[j], geo, rows, final_g=fg)
            if fg is not None:
                out = res
            else:
                xs = res
    if out is None:
        ms = jnp.mean(jnp.square(xs[:n_lat]), axis=-1, keepdims=True)
        out = xs[:n_lat] * lax.rsqrt(ms + NORM_EPS) * final_g
    return out.reshape(b_, n, d)
```

```python
import functools
import math

import jax
import jax.numpy as jnp
from jax import lax
from jax.experimental import pallas as pl
from jax.experimental.pallas import tpu as pltpu
from jax.experimental.pallas import tpu_sc as plsc

F32 = jnp.float32
BF16 = jnp.bfloat16

NORM_EPS = 1e-6
LN_EPS = 1e-5
CONV_WIDTH = 31
CONV_PAD = (CONV_WIDTH - 1) // 2
CONV_HALO = 16
GRID_W = 64
F_GROUPS = 8
DA_HEADS = 8
ROPE_BASE = 10000.0
TOP_K = 2
LANES = 128
SUBLANES = 8
MOD_ROWS = 16
DFT_ROWS = 64
SC_CORES = 2
SC_SUBCORES = 16
SC_INDEX_WINDOW = 128
SC_GATHER_ROWS = 32

V7X_VMEM_LIMIT = 56 * 1024 * 1024


def _cparams(sem, vmem=None):
    return pltpu.CompilerParams(dimension_semantics=sem, vmem_limit_bytes=vmem)


def _norm_mod(x, g, scale, shift):
    ms = jnp.mean(x * x, axis=-1, keepdims=True)
    y = x * lax.rsqrt(ms + NORM_EPS)
    return (y * g) * (1.0 + scale) + shift


def _mod_index(tm, n_lat_rows, seq_len, ctx_row):
    n_lat_tiles = n_lat_rows // tm
    per_seq = seq_len // tm

    def f(i):
        return jnp.where(i < n_lat_tiles, i // per_seq, ctx_row)

    return f


def _mod_kernel(c_ref, w_ref, b_ref, o_ref):
    c = c_ref[...]
    sc = c * jax.nn.sigmoid(c)
    o_ref[...] = jnp.dot(sc, w_ref[...], precision=lax.Precision.HIGHEST,
                         preferred_element_type=F32) + b_ref[...]


def _modulation(c_all, w_mod, b_mod):
    depth, d, nd = w_mod.shape
    tn = 1024
    return pl.pallas_call(
        _mod_kernel,
        out_shape=jax.ShapeDtypeStruct((depth, MOD_ROWS, nd), F32),
        grid=(depth, nd // tn),
        in_specs=[
            pl.BlockSpec((MOD_ROWS, d), lambda l, j: (0, 0)),
            pl.BlockSpec((None, d, tn), lambda l, j: (l, 0, j)),
            pl.BlockSpec((None, 1, tn), lambda l, j: (l, 0, j)),
        ],
        out_specs=pl.BlockSpec((None, MOD_ROWS, tn), lambda l, j: (l, 0, j)),
        compiler_params=_cparams(("arbitrary", "arbitrary")),
        name="modulation",
    )(c_all, w_mod, b_mod.reshape(depth, 1, nd))


def _conv_in_kernel(x_ref, mod_ref, g_ref, w_ref, b_ref, u_ref):
    d = u_ref.shape[1]
    h = _norm_mod(x_ref[...], g_ref[...], mod_ref[1:2, :], mod_ref[0:1, :])
    y = jnp.dot(h.astype(BF16), w_ref[...], preferred_element_type=F32) + b_ref[...]
    u_ref[...] = (y[:, :d] * jax.nn.sigmoid(y[:, d:])).astype(u_ref.dtype)


def _conv_in(xs, mod_l, g, w_in, b_in, *, rows, n_lat_rows, seq_len, ctx_row, tm):
    d = xs.shape[1]
    midx = _mod_index(tm, n_lat_rows, seq_len, ctx_row)
    return pl.pallas_call(
        _conv_in_kernel,
        out_shape=jax.ShapeDtypeStruct((rows, d), BF16),
        grid=(rows // tm,),
        in_specs=[
            pl.BlockSpec((tm, d), lambda i: (i, 0)),
            pl.BlockSpec((None, 6, d), lambda i: (midx(i), 0, 0)),
            pl.BlockSpec((1, d), lambda i: (0, 0)),
            pl.BlockSpec((d, 2 * d), lambda i: (0, 0)),
            pl.BlockSpec((1, 2 * d), lambda i: (0, 0)),
        ],
        out_specs=pl.BlockSpec((tm, d), lambda i: (i, 0)),
        compiler_params=_cparams(("arbitrary",), V7X_VMEM_LIMIT),
        name="conv_in",
    )(xs, mod_l, g, w_in, b_in)


def _conv_out_kernel(u_ref, up_ref, un_ref, x_ref, mod_ref, wdw_ref, bdw_ref, lng_ref, lnb_ref,
                     wo_ref, bo_ref, o_ref, ubuf, shifted, cbuf, *, n_lat_tiles, tiles_per_seq):
    tm, d = x_ref.shape
    i = pl.program_id(0)
    is_ctx = i >= n_lat_tiles
    j = i % tiles_per_seq
    first = jnp.logical_or(is_ctx, j == 0)
    last = jnp.logical_or(is_ctx, j == tiles_per_seq - 1)
    ubuf[0:CONV_HALO, :] = jnp.where(first, 0.0, up_ref[...].astype(F32))
    ubuf[CONV_HALO:CONV_HALO + tm, :] = u_ref[...].astype(F32)
    ubuf[CONV_HALO + tm:, :] = jnp.where(last, 0.0, un_ref[...].astype(F32))

    span = ubuf.shape[0] - SUBLANES
    for s in range(1, SUBLANES):
        shifted[s - 1] = ubuf[s:s + span, :]

    for c in range(d // LANES):
        cols = slice(c * LANES, (c + 1) * LANES)
        acc = jnp.zeros((tm, LANES), F32)
        for k in range(CONV_WIDTH):
            r0 = CONV_HALO - CONV_PAD + k
            s, a0 = r0 % SUBLANES, r0 - r0 % SUBLANES
            win = ubuf[a0:a0 + tm, cols] if s == 0 else shifted[s - 1, a0:a0 + tm, cols]
            acc = acc + wdw_ref[k:k + 1, cols] * win
        cbuf[:, cols] = acc + bdw_ref[:, cols]
    v = cbuf[...]
    mu = jnp.mean(v, axis=-1, keepdims=True)
    vc = v - mu
    var = jnp.mean(vc * vc, axis=-1, keepdims=True)
    y = vc * lax.rsqrt(var + LN_EPS) * lng_ref[...] + lnb_ref[...]
    y = y * jax.nn.sigmoid(y)
    out = jnp.dot(y.astype(BF16), wo_ref[...], preferred_element_type=F32) + bo_ref[...]
    o_ref[...] = x_ref[...] + mod_ref[2:3, :] * out


def _conv_out(u, xs, mod_l, w_dw, b_dw, ln_g, ln_b, w_out, b_out, *, rows, n_lat_rows, seq_len,
              ctx_row, tm):
    d = xs.shape[1]
    n_tiles = rows // tm
    n_lat_tiles = n_lat_rows // tm
    hpt = tm // CONV_HALO
    n_halo = u.shape[0] // CONV_HALO
    midx = _mod_index(tm, n_lat_rows, seq_len, ctx_row)
    kern = functools.partial(_conv_out_kernel, n_lat_tiles=n_lat_tiles, tiles_per_seq=seq_len // tm)
    return pl.pallas_call(
        kern,
        out_shape=jax.ShapeDtypeStruct(xs.shape, F32),
        grid=(n_tiles,),
        in_specs=[
            pl.BlockSpec((tm, d), lambda i: (i, 0)),
            pl.BlockSpec((CONV_HALO, d), lambda i: (jnp.maximum(i * hpt - 1, 0), 0)),
            pl.BlockSpec((CONV_HALO, d), lambda i: (jnp.minimum((i + 1) * hpt, n_halo - 1), 0)),
            pl.BlockSpec((tm, d), lambda i: (i, 0)),
            pl.BlockSpec((None, 6, d), lambda i: (midx(i), 0, 0)),
            pl.BlockSpec((CONV_WIDTH, d), lambda i: (0, 0)),
            pl.BlockSpec((1, d), lambda i: (0, 0)),
            pl.BlockSpec((1, d), lambda i: (0, 0)),
            pl.BlockSpec((1, d), lambda i: (0, 0)),
            pl.BlockSpec((d, d), lambda i: (0, 0)),
            pl.BlockSpec((1, d), lambda i: (0, 0)),
        ],
        out_specs=pl.BlockSpec((tm, d), lambda i: (i, 0)),
        scratch_shapes=[pltpu.VMEM((tm + 2 * CONV_HALO, d), F32),
                        pltpu.VMEM((SUBLANES - 1, tm + 2 * CONV_HALO - SUBLANES, d), F32),
                        pltpu.VMEM((tm, d), F32)],
        input_output_aliases={3: 0},
        compiler_params=_cparams(("arbitrary",), V7X_VMEM_LIMIT),
        name="conv_out",
    )(u, u, u, xs, mod_l, w_dw, b_dw, ln_g, ln_b, w_out, b_out)


def _cast_kernel(w_ref, o_ref):
    o_ref[...] = w_ref[...].astype(o_ref.dtype)


def _cast_layer_bf16(w, layer):
    _, n_e, a, b = w.shape
    ta = min(a, 512)
    return pl.pallas_call(
        _cast_kernel,
        out_shape=jax.ShapeDtypeStruct((n_e, a, b), BF16),
        grid=(n_e, a // ta),
        in_specs=[pl.BlockSpec((None, None, ta, b), lambda e, i: (layer, e, i, 0))],
        out_specs=pl.BlockSpec((None, ta, b), lambda e, i: (e, i, 0)),
        compiler_params=_cparams(("arbitrary", "arbitrary"), V7X_VMEM_LIMIT),
        name="cast_weights",
    )(w)


def _ffn_kernel(te_ref, rv_ref, *refs, fuse_norm, tf):
    if fuse_norm:
        x_ref, mod_ref, g_ref, wg_ref, wu_ref, wd_ref, o_ref = refs
    else:
        x_ref, wg_ref, wu_ref, wd_ref, o_ref = refs
    del te_ref
    n_rows = rv_ref[pl.program_id(0)]
    valid = n_rows > 0

    @pl.when(valid)
    def _():
        if fuse_norm:
            h = _norm_mod(x_ref[...], g_ref[...], mod_ref[4:5, :], mod_ref[3:4, :])
        else:
            row = lax.broadcasted_iota(jnp.int32, (x_ref.shape[0], 1), 0)
            h = jnp.where(row < n_rows, x_ref[...], 0.0)
        h = h.astype(BF16)
        acc = None
        for c in range(wg_ref.shape[1] // tf):
            cols = slice(c * tf, (c + 1) * tf)
            gt = jnp.dot(h, wg_ref[:, cols], preferred_element_type=F32)
            up = jnp.dot(h, wu_ref[:, cols], preferred_element_type=F32)
            a = (gt * jax.nn.sigmoid(gt) * up).astype(BF16)
            part = jnp.dot(a, wd_ref[cols, :], preferred_element_type=F32)
            acc = part if acc is None else acc + part
        if fuse_norm:
            o_ref[...] = x_ref[...] + mod_ref[5:6, :] * acc
        else:
            o_ref[...] = acc

    @pl.when(jnp.logical_not(valid))
    def _():
        o_ref[...] = jnp.zeros_like(o_ref)


def _ffn(x, tile_expert, tile_rows, wg, wu, wd, *, rows, tm, tf, mod_l=None, g=None, midx=None):
    d = x.shape[1]
    f_dim = wg.shape[2]
    fuse_norm = mod_l is not None
    resident = pl.Buffered(1)
    in_specs = [pl.BlockSpec((tm, d), lambda j, te, nv: (j, 0))]
    args = [x]
    if fuse_norm:
        in_specs += [pl.BlockSpec((None, 6, d), lambda j, te, nv: (midx(j), 0, 0)),
                     pl.BlockSpec((1, d), lambda j, te, nv: (0, 0))]
        args += [mod_l, g]
    in_specs += [
        pl.BlockSpec((None, d, f_dim), lambda j, te, nv: (te[j], 0, 0), pipeline_mode=resident),
        pl.BlockSpec((None, d, f_dim), lambda j, te, nv: (te[j], 0, 0), pipeline_mode=resident),
        pl.BlockSpec((None, f_dim, d), lambda j, te, nv: (te[j], 0, 0), pipeline_mode=resident),
    ]
    args += [wg, wu, wd]
    kern = functools.partial(_ffn_kernel, fuse_norm=fuse_norm, tf=tf)
    return pl.pallas_call(
        kern,
        out_shape=jax.ShapeDtypeStruct(x.shape, F32),
        grid_spec=pltpu.PrefetchScalarGridSpec(
            num_scalar_prefetch=2,
            grid=(rows // tm,),
            in_specs=in_specs,
            out_specs=pl.BlockSpec((tm, d), lambda j, te, nv: (j, 0)),
        ),
        input_output_aliases=({2: 0} if fuse_norm else {}),
        compiler_params=_cparams(("arbitrary",), V7X_VMEM_LIMIT),
        name="ffn_dense" if fuse_norm else "ffn_grouped",
    )(tile_expert, tile_rows, *args)


def _router_kernel(x_ref, mod_ref, g_ref, wr_ref, h_ref, meta_ref, gw_ref, cnt_ref, carry, *, n_exp):
    tm = x_ref.shape[0]
    i = pl.program_id(0)

    @pl.when(i == 0)
    def _():
        carry[...] = jnp.zeros_like(carry)

    h = _norm_mod(x_ref[...], g_ref[...], mod_ref[4:5, :], mod_ref[3:4, :])
    h_ref[...] = h
    logits = jnp.dot(h, wr_ref[...], precision=lax.Precision.HIGHEST, preferred_element_type=F32)
    lane = lax.broadcasted_iota(jnp.int32, (tm, LANES), 1).astype(F32)
    neg = jnp.float32(-jnp.inf)
    lg = jnp.where(lane < n_exp, logits, neg)
    m1 = jnp.max(lg, axis=-1, keepdims=True)
    i1 = jnp.min(jnp.where(lg == m1, lane, float(LANES)), axis=-1, keepdims=True)
    lg2 = jnp.where(lane == i1, neg, lg)
    m2 = jnp.max(lg2, axis=-1, keepdims=True)
    i2 = jnp.min(jnp.where(lg2 == m2, lane, float(LANES)), axis=-1, keepdims=True)
    e2 = jnp.exp(m2 - m1)
    w1 = 1.0 / (1.0 + e2)
    w2 = e2 * w1
    sel1 = lane == i1
    sel2 = lane == i2
    onehot = jnp.where(jnp.logical_or(sel1, sel2), 1.0, 0.0)
    rr = lax.broadcasted_iota(jnp.int32, (tm, tm), 0)
    cc = lax.broadcasted_iota(jnp.int32, (tm, tm), 1)
    tri = jnp.where(rr > cc, 1.0, 0.0).astype(BF16)
    cum = jnp.dot(tri, onehot.astype(BF16), preferred_element_type=F32) + carry[...]
    r1 = jnp.sum(jnp.where(sel1, cum, 0.0), axis=-1, keepdims=True)
    r2 = jnp.sum(jnp.where(sel2, cum, 0.0), axis=-1, keepdims=True)
    carry[...] += jnp.sum(onehot, axis=0, keepdims=True)
    meta = jnp.where(lane == 0, i1, jnp.where(lane == 1, i2, jnp.where(lane == 2, r1,
                     jnp.where(lane == 3, r2, 0.0))))
    meta_ref[...] = meta.astype(jnp.int32)
    gw_ref[...] = jnp.where(lane == 0, w1, jnp.where(lane == 1, w2, 0.0))
    cnt_ref[...] = jnp.broadcast_to(carry[...], cnt_ref.shape)


def _router(xs, mod_l, g, w_router_pad, *, rows, n_lat_rows, seq_len, ctx_row, tm, n_exp):
    d = xs.shape[1]
    midx = _mod_index(tm, n_lat_rows, seq_len, ctx_row)
    kern = functools.partial(_router_kernel, n_exp=n_exp)
    return pl.pallas_call(
        kern,
        out_shape=[jax.ShapeDtypeStruct((rows, d), F32),
                   jax.ShapeDtypeStruct((rows, LANES), jnp.int32),
                   jax.ShapeDtypeStruct((rows, LANES), F32),
                   jax.ShapeDtypeStruct((8, LANES), F32)],
        grid=(rows // tm,),
        in_specs=[
            pl.BlockSpec((tm, d), lambda i: (i, 0)),
            pl.BlockSpec((None, 6, d), lambda i: (midx(i), 0, 0)),
            pl.BlockSpec((1, d), lambda i: (0, 0)),
            pl.BlockSpec((d, LANES), lambda i: (0, 0)),
        ],
        out_specs=[pl.BlockSpec((tm, d), lambda i: (i, 0)),
                   pl.BlockSpec((tm, LANES), lambda i: (i, 0)),
                   pl.BlockSpec((tm, LANES), lambda i: (i, 0)),
                   pl.BlockSpec((8, LANES), lambda i: (0, 0))],
        scratch_shapes=[pltpu.VMEM((1, LANES), F32)],
        compiler_params=_cparams(("arbitrary",), V7X_VMEM_LIMIT),
        name="router",
    )(xs, mod_l, g, w_router_pad)


def _dispatch_rows(src, dest, n_out):
    t, d = src.shape
    n_slots = dest.shape[0]
    assert t % SC_INDEX_WINDOW == 0
    n_win = t // SC_INDEX_WINDOW
    n_workers = SC_CORES * SC_SUBCORES
    mesh = plsc.VectorSubcoreMesh(core_axis_name="core", subcore_axis_name="subcore")

    @pl.kernel(out_type=jax.ShapeDtypeStruct((n_out, d), src.dtype), mesh=mesh,
               scratch_types=[pltpu.VMEM((n_slots, SC_INDEX_WINDOW), jnp.int32),
                              pltpu.VMEM((SC_GATHER_ROWS, d), src.dtype)])
    def dispatch(src_hbm, idx_hbm, out_hbm, idx_v, buf):
        wid = lax.axis_index("core") * SC_SUBCORES + lax.axis_index("subcore")

        @pl.loop(0, (n_win - wid + n_workers - 1) // n_workers)
        def _(b):
            base = (b * n_workers + wid) * SC_INDEX_WINDOW
            pltpu.sync_copy(idx_hbm.at[:, pl.ds(base, SC_INDEX_WINDOW)], idx_v)
            for k in range(SC_INDEX_WINDOW // SC_GATHER_ROWS):
                rows = pl.ds(k * SC_GATHER_ROWS, SC_GATHER_ROWS)
                pltpu.sync_copy(src_hbm.at[pl.ds(base + k * SC_GATHER_ROWS, SC_GATHER_ROWS)], buf)
                for s in range(n_slots):
                    pltpu.sync_copy(buf, out_hbm.at[idx_v.at[s, rows]])

    return dispatch(src, dest)


def _gather_rows(src, idx):
    n = idx.shape[0]
    d = src.shape[1]
    n_workers = SC_CORES * SC_SUBCORES
    per = n // n_workers
    assert n % (n_workers * SC_INDEX_WINDOW) == 0, (n, n_workers, SC_INDEX_WINDOW)
    mesh = plsc.VectorSubcoreMesh(core_axis_name="core", subcore_axis_name="subcore")

    @pl.kernel(out_type=jax.ShapeDtypeStruct((n, d), src.dtype), mesh=mesh,
               scratch_types=[pltpu.VMEM((1, SC_INDEX_WINDOW), jnp.int32),
                              pltpu.VMEM((SC_GATHER_ROWS, d), src.dtype)])
    def gather(src_hbm, idx_hbm, out_hbm, idx_v, buf):
        wid = lax.axis_index("core") * SC_SUBCORES + lax.axis_index("subcore")

        @pl.loop(0, per // SC_INDEX_WINDOW)
        def _(b):
            base = wid * per + b * SC_INDEX_WINDOW
            pltpu.sync_copy(idx_hbm.at[:, pl.ds(base, SC_INDEX_WINDOW)], idx_v)
            for k in range(SC_INDEX_WINDOW // SC_GATHER_ROWS):
                rows = pl.ds(k * SC_GATHER_ROWS, SC_GATHER_ROWS)
                pltpu.sync_copy(src_hbm.at[idx_v.at[0, rows]], buf)
                pltpu.sync_copy(buf, out_hbm.at[pl.ds(base + k * SC_GATHER_ROWS, SC_GATHER_ROWS)])

    return gather(src, idx.reshape(1, n))


def _combine_kernel(x_ref, y1_ref, y2_ref, gw_ref, mod_ref, *rest, final):
    if final:
        fg_ref, o_ref = rest
    else:
        (o_ref,) = rest
    gw = gw_ref[...]
    y = gw[:, 0:1] * y1_ref[...] + gw[:, 1:2] * y2_ref[...]
    xn = x_ref[...] + mod_ref[5:6, :] * y
    if final:
        ms = jnp.mean(xn * xn, axis=-1, keepdims=True)
        xn = xn * lax.rsqrt(ms + NORM_EPS) * fg_ref[...]
    o_ref[...] = xn


def _combine(xs, ypair, gw, mod_l, *, rows, n_lat_rows, seq_len, ctx_row, tm, final_g=None):
    d = xs.shape[1]
    n_tiles = rows // tm
    midx = _mod_index(tm, n_lat_rows, seq_len, ctx_row)
    final = final_g is not None
    in_specs = [
        pl.BlockSpec((tm, d), lambda i: (i, 0)),
        pl.BlockSpec((tm, d), lambda i: (i, 0)),
        pl.BlockSpec((tm, d), lambda i: (i + n_tiles, 0)),
        pl.BlockSpec((tm, LANES), lambda i: (i, 0)),
        pl.BlockSpec((None, 6, d), lambda i: (midx(i), 0, 0)),
    ]
    args = [xs, ypair, ypair, gw, mod_l]
    if final:
        in_specs.append(pl.BlockSpec((1, d), lambda i: (0, 0)))
        args.append(final_g)
    out_rows = rows if final else xs.shape[0]
    return pl.pallas_call(
        functools.partial(_combine_kernel, final=final),
        out_shape=jax.ShapeDtypeStruct((out_rows, d), F32),
        grid=(n_tiles,),
        in_specs=in_specs,
        out_specs=pl.BlockSpec((tm, d), lambda i: (i, 0)),
        input_output_aliases=({} if final else {0: 0}),
        compiler_params=_cparams(("arbitrary",), V7X_VMEM_LIMIT),
        name="moe_combine",
    )(*args)


def _fnet_a_kernel(x_ref, mod_ref, g_ref, cs_ref, yc_ref, ys_ref, *, n_groups):
    h = _norm_mod(x_ref[...], g_ref[...], mod_ref[1:2, :], mod_ref[0:1, :]).astype(BF16)
    gd = cs_ref.shape[0]
    for gi in range(n_groups):
        y = jnp.dot(h[:, gi * gd:(gi + 1) * gd], cs_ref[...], preferred_element_type=F32)
        yc_ref[:, gi * gd:(gi + 1) * gd] = y[:, :gd].astype(yc_ref.dtype)
        ys_ref[:, gi * gd:(gi + 1) * gd] = y[:, gd:].astype(ys_ref.dtype)


def _fnet_a(xs, mod_l, g, cs, *, rows, n_lat_rows, seq_len, ctx_row, tm):
    d = xs.shape[1]
    gd = cs.shape[0]
    midx = _mod_index(tm, n_lat_rows, seq_len, ctx_row)
    return pl.pallas_call(
        functools.partial(_fnet_a_kernel, n_groups=d // gd),
        out_shape=[jax.ShapeDtypeStruct((rows, d), BF16)] * 2,
        grid=(rows // tm,),
        in_specs=[
            pl.BlockSpec((tm, d), lambda i: (i, 0)),
            pl.BlockSpec((None, 6, d), lambda i: (midx(i), 0, 0)),
            pl.BlockSpec((1, d), lambda i: (0, 0)),
            pl.BlockSpec((gd, 2 * gd), lambda i: (0, 0)),
        ],
        out_specs=[pl.BlockSpec((tm, d), lambda i: (i, 0))] * 2,
        compiler_params=_cparams(("arbitrary",), V7X_VMEM_LIMIT),
        name="fnet_group_dft",
    )(xs, mod_l, g, cs)


def _dft_table_kernel(ac_ref, as_ref, bc_ref, bs_ref, c_ref, sn_ref):
    ac, as_ = ac_ref[...], as_ref[...]
    bc, bs = bc_ref[...], bs_ref[...]
    c_ref[...] = (bc * ac - bs * as_).astype(c_ref.dtype)
    sn_ref[...] = (-(bs * ac + bc * as_)).astype(sn_ref.dtype)


def _dft_tables(n):
    r = DFT_ROWS
    k = jnp.arange(n, dtype=jnp.int32)[None, :]
    j1 = jnp.arange(r, dtype=jnp.int32)[:, None]
    j0 = (jnp.arange(n // r, dtype=jnp.int32) * r)[:, None]
    ang1 = ((j1 * k) % n).astype(F32) * (2.0 * math.pi / n)
    ang0 = ((j0 * k) % n).astype(F32) * (2.0 * math.pi / n)
    scale = 1.0 / math.sqrt(n)
    ac, as_ = jnp.cos(ang1), jnp.sin(ang1)
    bc = (jnp.cos(ang0) * scale).reshape(n // r, 1, n)
    bs = (jnp.sin(ang0) * scale).reshape(n // r, 1, n)
    return pl.pallas_call(
        _dft_table_kernel,
        out_shape=[jax.ShapeDtypeStruct((n, n), BF16)] * 2,
        grid=(n // r,),
        in_specs=[
            pl.BlockSpec((r, n), lambda i: (0, 0)),
            pl.BlockSpec((r, n), lambda i: (0, 0)),
            pl.BlockSpec((None, 1, n), lambda i: (i, 0, 0)),
            pl.BlockSpec((None, 1, n), lambda i: (i, 0, 0)),
        ],
        out_specs=[pl.BlockSpec((r, n), lambda i: (i, 0))] * 2,
        compiler_params=_cparams(("arbitrary",)),
        name="dft_tables",
    )(ac, as_, bc, bs)


def _fnet_b_kernel(c_ref, sn_ref, yc_ref, ys_ref, x_ref, mod_ref, wf_ref, bf_ref, o_ref, acc, *, n_k):
    k = pl.program_id(2)
    part = (jnp.dot(c_ref[...], yc_ref[...], preferred_element_type=F32)
            + jnp.dot(sn_ref[...], ys_ref[...], preferred_element_type=F32))

    @pl.when(k == 0)
    def _():
        acc[...] = part

    @pl.when(k > 0)
    def _():
        acc[...] += part

    @pl.when(k == n_k - 1)
    def _():
        z = acc[...].astype(BF16)
        out = jnp.dot(z, wf_ref[...], preferred_element_type=F32) + bf_ref[...]
        o_ref[...] = x_ref[...] + mod_ref[2:3, :] * out


def _fnet_b(ctab, stab, yc, ys, xs, mod_l, wf, bf, *, n_batch, seq_len, row_off, mod_ctx_row, tm, tk):
    d = xs.shape[1]
    n_i = seq_len // tm
    n_k = seq_len // tk
    off_m = row_off // tm
    off_k = row_off // tk

    def mrow(b):
        return b if mod_ctx_row is None else mod_ctx_row

    return pl.pallas_call(
        functools.partial(_fnet_b_kernel, n_k=n_k),
        out_shape=jax.ShapeDtypeStruct(xs.shape, F32),
        grid=(n_batch, n_i, n_k),
        in_specs=[
            pl.BlockSpec((tm, tk), lambda b, i, k: (i, k)),
            pl.BlockSpec((tm, tk), lambda b, i, k: (i, k)),
            pl.BlockSpec((tk, d), lambda b, i, k: (off_k + b * n_k + k, 0)),
            pl.BlockSpec((tk, d), lambda b, i, k: (off_k + b * n_k + k, 0)),
            pl.BlockSpec((tm, d), lambda b, i, k: (off_m + b * n_i + i, 0)),
            pl.BlockSpec((None, 6, d), lambda b, i, k: (mrow(b), 0, 0)),
            pl.BlockSpec((d, d), lambda b, i, k: (0, 0)),
            pl.BlockSpec((1, d), lambda b, i, k: (0, 0)),
        ],
        out_specs=pl.BlockSpec((tm, d), lambda b, i, k: (off_m + b * n_i + i, 0)),
        scratch_shapes=[pltpu.VMEM((tm, d), F32)],
        input_output_aliases={4: 0},
        compiler_params=_cparams(("arbitrary", "arbitrary", "arbitrary"), V7X_VMEM_LIMIT),
        name="fnet_seq_dft",
    )(ctab, stab, yc, ys, xs, mod_l, wf, bf)


def _qkv_kernel(x_ref, mod_ref, g_ref, w_ref, cos_ref, sin_ref, q_ref, k_ref, v_ref, *,
                n_lat_tiles, q_scale):
    tm, d = x_ref.shape
    i = pl.program_id(0)
    is_ctx = i >= n_lat_tiles
    h = _norm_mod(x_ref[...], g_ref[...], mod_ref[1:2, :], mod_ref[0:1, :])
    y = jnp.dot(h.astype(BF16), w_ref[...], preferred_element_type=F32)
    tw = cos_ref.shape[1]
    cos = jnp.where(is_ctx, 1.0, cos_ref[...])
    sin = jnp.where(is_ctx, 0.0, sin_ref[...])
    lane = lax.broadcasted_iota(jnp.int32, (tm, tw), 1)
    half = tw // 8
    lo = (lane % (2 * half)) < half

    def rope(t):
        rot = jnp.where(lo, -pltpu.roll(t, tw - half, 1), pltpu.roll(t, half, 1))
        return t * cos + rot * sin

    for hd in range(d // tw):
        cols = slice(hd * tw, (hd + 1) * tw)
        q_ref[:, cols] = (rope(y[:, hd * tw:(hd + 1) * tw]) * q_scale).astype(q_ref.dtype)
        k_ref[:, cols] = rope(y[:, d + hd * tw:d + (hd + 1) * tw]).astype(k_ref.dtype)
    v_ref[...] = y[:, 2 * d:].astype(v_ref.dtype)


def _qkv(xs, mod_l, g, w_qkv, cos_t, sin_t, *, rows, n_lat_rows, seq_len, ctx_row, tm, q_scale):
    d = xs.shape[1]
    midx = _mod_index(tm, n_lat_rows, seq_len, ctx_row)
    per_seq = seq_len // tm
    n_lat_tiles = n_lat_rows // tm
    tw = cos_t.shape[1]

    def pos(i):
        return jnp.where(i < n_lat_tiles, i % per_seq, 0)

    kern = functools.partial(_qkv_kernel, n_lat_tiles=n_lat_tiles, q_scale=q_scale)
    return pl.pallas_call(
        kern,
        out_shape=[jax.ShapeDtypeStruct((rows, d), BF16)] * 3,
        grid=(rows // tm,),
        in_specs=[
            pl.BlockSpec((tm, d), lambda i: (i, 0)),
            pl.BlockSpec((None, 6, d), lambda i: (midx(i), 0, 0)),
            pl.BlockSpec((1, d), lambda i: (0, 0)),
            pl.BlockSpec((d, 3 * d), lambda i: (0, 0)),
            pl.BlockSpec((tm, tw), lambda i: (pos(i), 0)),
            pl.BlockSpec((tm, tw), lambda i: (pos(i), 0)),
        ],
        out_specs=[pl.BlockSpec((tm, d), lambda i: (i, 0))] * 3,
        compiler_params=_cparams(("arbitrary",), V7X_VMEM_LIMIT),
        name="attn_qkv",
    )(xs, mod_l, g, w_qkv, cos_t, sin_t)


def _dot_nt(a, b):
    return lax.dot_general(a, b, (((1,), (1,)), ((), ())), preferred_element_type=F32)


def _flash_kernel(lam_ref, sg_ref, q_ref, kc_ref, vc_ref, *rest, tk, n_kx, lam_init):
    if n_kx:
        kx_ref, vx_ref, o_ref, s_even, s_odd = rest
    else:
        (o_ref,) = rest
    tq, dh = q_ref.shape
    q = q_ref[...]
    lane = lax.broadcasted_iota(jnp.int32, (tq, dh), 1)
    zero = jnp.zeros_like(q)
    qa = jnp.where(lane < dh // 2, q, zero)
    qb = jnp.where(lane >= dh // 2, q, zero)

    def v_ext(v):
        return jnp.concatenate([v, jnp.ones_like(v)], axis=1)

    def init(k, v):
        ve = v_ext(v)
        out = []
        for qm in (qa, qb):
            s = _dot_nt(qm, k)
            m = jnp.max(s, axis=-1, keepdims=True)
            p = jnp.exp2(s - m).astype(BF16)
            out += [m, jnp.dot(p, ve, preferred_element_type=F32)]
        return tuple(out)

    def scores(t, s_ref):
        k = kx_ref[pl.ds(pl.multiple_of(t * tk, tk), tk), :]
        s_ref[0] = _dot_nt(qa, k)
        s_ref[1] = _dot_nt(qb, k)

    def update(t, s_ref, carry):
        ve = v_ext(vx_ref[pl.ds(pl.multiple_of(t * tk, tk), tk), :])
        out = []
        for mp, (m, acc) in enumerate((carry[0:2], carry[2:4])):
            s = s_ref[mp]
            m_new = jnp.maximum(m, jnp.max(s, axis=-1, keepdims=True))
            p = jnp.exp2(s - m_new).astype(BF16)
            acc = jnp.exp2(m - m_new) * acc + jnp.dot(p, ve, preferred_element_type=F32)
            out += [m_new, acc]
        return tuple(out)

    carry = init(kc_ref[...], vc_ref[...])
    if n_kx:
        assert n_kx % 2 == 0
        scores(0, s_even)

        def pair(u, c, prefetch_next):
            t = 2 * u
            scores(t + 1, s_odd)
            c = update(t, s_even, c)
            if prefetch_next:
                scores(t + 2, s_even)
            return update(t + 1, s_odd, c)

        carry = lax.fori_loop(0, n_kx // 2 - 1, lambda u, c: pair(u, c, True), carry)
        carry = pair(n_kx // 2 - 1, carry, False)
    _, a1, _, a2 = carry
    lv = lam_ref[...]
    lam = (jnp.exp(jnp.sum(lv[0:1, :] * lv[1:2, :], axis=-1, keepdims=True))
           - jnp.exp(jnp.sum(lv[2:3, :] * lv[3:4, :], axis=-1, keepdims=True)) + lam_init)
    o = a1[:, :dh] / a1[:, dh:] - lam * (a2[:, :dh] / a2[:, dh:])
    ms = jnp.mean(o * o, axis=-1, keepdims=True)
    o = o * lax.rsqrt(ms + NORM_EPS) * sg_ref[...] * (1.0 - lam_init)
    o_ref[...] = o.astype(o_ref.dtype)


def _flash(q, k, v, lam_vecs, subln_g, out_init, *, n_batch, n_heads, q_len, q_off, kc_len, kc_off,
           kx_len, tq, tk, lam_init):
    dh = q.shape[1] // n_heads
    n_q = q_len // tq
    qo = q_off // tq
    kco = kc_off // kc_len
    n_kx = kx_len // tk if kx_len else 0
    in_specs = [
        pl.BlockSpec(memory_space=pl.ANY),
        pl.BlockSpec(lam_vecs.shape, lambda b, h, i: (0, 0)),
        pl.BlockSpec((1, dh), lambda b, h, i: (0, 0)),
        pl.BlockSpec((tq, dh), lambda b, h, i: (qo + b * n_q + i, h)),
        pl.BlockSpec((kc_len, dh), lambda b, h, i: (kco + b, h)),
        pl.BlockSpec((kc_len, dh), lambda b, h, i: (kco + b, h)),
    ]
    args = [out_init, lam_vecs, subln_g, q, k, v]
    if n_kx:
        in_specs += [pl.BlockSpec((kx_len, dh), lambda b, h, i: (b, h)),
                     pl.BlockSpec((kx_len, dh), lambda b, h, i: (b, h))]
        args += [k, v]
    kern = functools.partial(_flash_kernel_alias, tk=tk, n_kx=n_kx, lam_init=lam_init)
    return pl.pallas_call(
        kern,
        out_shape=jax.ShapeDtypeStruct(out_init.shape, out_init.dtype),
        grid=(n_batch, n_heads, n_q),
        in_specs=in_specs,
        out_specs=pl.BlockSpec((tq, dh), lambda b, h, i: (qo + b * n_q + i, h)),
        scratch_shapes=([pltpu.VMEM((2, tq, tk), F32)] * 2 if n_kx else []),
        input_output_aliases={0: 0},
        compiler_params=_cparams(("arbitrary", "arbitrary", "arbitrary"), V7X_VMEM_LIMIT),
        name="diff_attn_latent" if n_kx else "diff_attn_ctx",
    )(*args)


def _flash_kernel_alias(out_init_ref, *refs, tk, n_kx, lam_init):
    del out_init_ref
    _flash_kernel(*refs, tk=tk, n_kx=n_kx, lam_init=lam_init)


def _proj_residual_kernel(a_ref, w_ref, x_ref, mod_ref, o_ref):
    out = jnp.dot(a_ref[...], w_ref[...], preferred_element_type=F32)
    o_ref[...] = x_ref[...] + mod_ref[2:3, :] * out


def _proj_residual(a, w, xs, mod_l, *, rows, n_lat_rows, seq_len, ctx_row, tm):
    d = xs.shape[1]
    midx = _mod_index(tm, n_lat_rows, seq_len, ctx_row)
    return pl.pallas_call(
        _proj_residual_kernel,
        out_shape=jax.ShapeDtypeStruct(xs.shape, F32),
        grid=(rows // tm,),
        in_specs=[
            pl.BlockSpec((tm, d), lambda i: (i, 0)),
            pl.BlockSpec((d, d), lambda i: (0, 0)),
            pl.BlockSpec((tm, d), lambda i: (i, 0)),
            pl.BlockSpec((None, 6, d), lambda i: (midx(i), 0, 0)),
        ],
        out_specs=pl.BlockSpec((tm, d), lambda i: (i, 0)),
        input_output_aliases={2: 0},
        compiler_params=_cparams(("arbitrary",), V7X_VMEM_LIMIT),
        name="attn_out_proj",
    )(a, w, xs, mod_l)


def _rope_tables(n_tokens, width):
    rope_axis_dim = 32
    freqs = rope_axis_dim // 2
    rows = n_tokens // GRID_W
    row = jnp.repeat(jnp.arange(rows, dtype=F32), GRID_W)
    col = jnp.tile(jnp.arange(GRID_W, dtype=F32), rows)
    inv_freq = 1.0 / (ROPE_BASE ** (jnp.arange(freqs, dtype=F32) * 2.0 / rope_axis_dim))
    ang = jnp.stack([row[:, None] * inv_freq, col[:, None] * inv_freq], axis=1)
    ang = jnp.stack([ang, ang], axis=2).reshape(n_tokens, 4 * freqs)
    ang = jnp.tile(ang, (1, width // (4 * freqs)))
    return jnp.cos(ang), jnp.sin(ang)


def _dense_ffn_layer(xs, mod_l, g, wg, wu, wd, geo, rows):
    tm = geo["tm_ffn"]
    n_tiles = rows // tm
    te = jnp.zeros((n_tiles,), jnp.int32)
    tr = jnp.full((n_tiles,), tm, jnp.int32)
    midx = _mod_index(tm, geo["n_lat_rows"], geo["seq_len"], geo["ctx_row"])
    return _ffn(xs, te, tr, wg, wu, wd, rows=rows, tm=tm, tf=geo["tf"], mod_l=mod_l, g=g, midx=midx)


def _moe_layer(xs, mod_l, g, w_router, wg, wu, wd, geo, rows, final_g=None):
    d = xs.shape[1]
    n_exp = w_router.shape[1]
    tm = geo["tm_ffn"]
    common = dict(rows=rows, n_lat_rows=geo["n_lat_rows"], seq_len=geo["seq_len"], ctx_row=geo["ctx_row"])
    wr = jnp.zeros((d, LANES), F32).at[:, :n_exp].set(w_router)
    h, meta, gw, cnt = _router(xs, mod_l, g, wr, tm=geo["tm_router"], n_exp=n_exp, **common)
    counts = cnt[0, :n_exp].astype(jnp.int32)
    padded = ((counts + tm - 1) // tm) * tm
    ends = jnp.cumsum(padded)
    offs = ends - padded
    n_sorted_tiles = (TOP_K * rows) // tm + n_exp
    tile_start = jnp.arange(n_sorted_tiles, dtype=jnp.int32) * tm
    tile_expert = jnp.minimum(jnp.sum(tile_start[:, None] >= ends[None, :], axis=1), n_exp - 1).astype(jnp.int32)
    tile_rows = jnp.clip((offs + counts)[tile_expert] - tile_start, 0, tm).astype(jnp.int32)
    tile_rows = jnp.where(tile_start < ends[-1], tile_rows, 0)
    e1, e2, r1, r2 = meta[:, 0], meta[:, 1], meta[:, 2], meta[:, 3]
    dest = jnp.stack([offs[e1] + r1, offs[e2] + r2]).astype(jnp.int32)
    hs = _dispatch_rows(h, dest, n_sorted_tiles * tm)
    ys = _ffn(hs, tile_expert, tile_rows, wg, wu, wd, rows=n_sorted_tiles * tm, tm=tm, tf=geo["tf"])
    ypair = _gather_rows(ys, dest.reshape(TOP_K * rows))
    return _combine(xs, ypair, gw, mod_l, tm=geo["tm_row"], final_g=final_g, **common)


def kernel(x, c, ctx, c_ctx, w_mod, b_mod, norm_g, conv_w_in, conv_b_in, conv_w_dw, conv_b_dw,
           conv_ln_g, conv_ln_b, conv_w_out, conv_b_out, fnet_w, fnet_b, attn_w_qkv, attn_lambda,
           attn_subln_g, attn_w_o, ffn_w_gate, ffn_w_up, ffn_w_down, moe_w_router, moe_w_gate,
           moe_w_up, moe_w_down, final_g):
    b_, n, d = x.shape
    n_ctx = ctx.shape[1]
    depth = w_mod.shape[0]
    n_lat = b_ * n
    n_all = n_lat + b_ * n_ctx
    assert b_ < MOD_ROWS and d % LANES == 0 and n % GRID_W == 0 and n % n_ctx == 0
    assert d // DA_HEADS == LANES
    geo = dict(n_lat_rows=n_lat, seq_len=n, ctx_row=b_,
               tm_row=min(512, n_ctx * b_, n), tm_ffn=min(1024, n_ctx * b_, n),
               tm_router=min(512, n_ctx * b_, n), tf=min(512, ffn_w_gate.shape[2]),
               tm_conv=n_ctx)

    xs = jnp.concatenate([x.reshape(n_lat, d), ctx.reshape(b_ * n_ctx, d)], axis=0)
    c_all = jnp.zeros((MOD_ROWS, d), F32).at[:b_].set(c).at[b_].set(c_ctx)
    mod = _modulation(c_all, w_mod, b_mod).reshape(depth, MOD_ROWS, 6, d)

    out = None
    for i in range(depth):
        need_ctx = i < depth - 1
        rows = n_all if need_ctx else n_lat
        kind = i % 3
        mod_l = mod[i]
        common = dict(rows=rows, n_lat_rows=n_lat, seq_len=n, ctx_row=b_)
        g0 = norm_g[i, 0].reshape(1, d)
        g1 = norm_g[i, 1].reshape(1, d)
        j = i // 3
        if kind == 0:
            u = _conv_in(xs, mod_l, g0, conv_w_in[j].astype(BF16), conv_b_in[j].reshape(1, 2 * d),
                         tm=geo["tm_row"], **common)
            xs = _conv_out(u, xs, mod_l, conv_w_dw[j], conv_b_dw[j].reshape(1, d),
                           conv_ln_g[j].reshape(1, d), conv_ln_b[j].reshape(1, d),
                           conv_w_out[j].astype(BF16), conv_b_out[j].reshape(1, d),
                           tm=geo["tm_conv"], **common)
        elif kind == 1:
            gd = d // F_GROUPS
            kk = jnp.arange(gd, dtype=jnp.int32)
            ang = ((kk[:, None] * kk[None, :]) % gd).astype(F32) * (2.0 * math.pi / gd)
            cs = (jnp.concatenate([jnp.cos(ang), jnp.sin(ang)], axis=1) / math.sqrt(gd)).astype(BF16)
            yc, ys = _fnet_a(xs, mod_l, g0, cs, tm=geo["tm_row"], **common)
            wf = fnet_w[j].astype(BF16)
            bf = fnet_b[j].reshape(1, d)
            ct, st = _dft_tables(n)
            tmk = min(1024, n)
            xs = _fnet_b(ct, st, yc, ys, xs, mod_l, wf, bf, n_batch=b_, seq_len=n, row_off=0,
                         mod_ctx_row=None, tm=tmk, tk=tmk)
            if need_ctx:
                ct, st = _dft_tables(n_ctx)
                xs = _fnet_b(ct, st, yc, ys, xs, mod_l, wf, bf, n_batch=b_, seq_len=n_ctx,
                             row_off=n_lat, mod_ctx_row=b_, tm=n_ctx, tk=n_ctx)
        else:
            lam_init = 0.8 - 0.6 * math.exp(-0.3 * i)
            dh = d // DA_HEADS
            cos_t, sin_t = _rope_tables(n, LANES)
            q, k, v = _qkv(xs, mod_l, g0, attn_w_qkv[j].astype(BF16), cos_t, sin_t,
                           tm=geo["tm_row"], q_scale=(dh // 2) ** -0.5 * math.log2(math.e), **common)
            sg = attn_subln_g[j].reshape(1, dh)
            o = jnp.zeros((rows, d), BF16)
            fl = dict(n_batch=b_, n_heads=DA_HEADS, kc_len=n_ctx, kc_off=n_lat, lam_init=lam_init)
            o = _flash(q, k, v, attn_lambda[j], sg, o, q_len=n, q_off=0, kx_len=n,
                       tq=min(512, n), tk=min(512, n), **fl)
            if need_ctx:
                o = _flash(q, k, v, attn_lambda[j], sg, o, q_len=n_ctx, q_off=n_lat, kx_len=0,
                           tq=n_ctx, tk=n_ctx, **fl)
            xs = _proj_residual(o, attn_w_o[j].astype(BF16), xs, mod_l, tm=geo["tm_row"], **common)

        j = i // 2
        if i % 2 == 0:
            wg, wu, wd = (_cast_layer_bf16(w[:, None], j) for w in (ffn_w_gate, ffn_w_up, ffn_w_down))
            xs = _dense_ffn_layer(xs, mod_l, g1, wg, wu, wd, geo, rows)
        else:
            fg = final_g.reshape(1, d) if i == depth - 1 else None
            xs, w32 = lax.optimization_barrier((xs, (moe_w_gate, moe_w_up, moe_w_down)))
            wg, wu, wd = (_cast_layer_bf16(w, j) for w in w32)
            xs, wg, wu, wd = lax.optimization_barrier((xs, wg, wu, wd))
            res = _moe_layer(xs, mod_l, g1, moe_w_router[j], wg, wu, wd, geo, rows, final_g=fg)
            if fg is not None:
                out = res
            else:
                xs = res
    if out is None:
        ms = jnp.mean(jnp.square(xs[:n_lat]), axis=-1, keepdims=True)
        out = xs[:n_lat] * lax.rsqrt(ms + NORM_EPS) * final_g
    return out.reshape(b_, n, d)
```

```python
import functools
import math

import jax
import jax.numpy as jnp
from jax import lax
from jax.experimental import pallas as pl
from jax.experimental.pallas import tpu as pltpu
from jax.experimental.pallas import tpu_sc as plsc

F32 = jnp.float32
BF16 = jnp.bfloat16

NORM_EPS = 1e-6
LN_EPS = 1e-5
CONV_WIDTH = 31
CONV_PAD = (CONV_WIDTH - 1) // 2
CONV_HALO = 16
GRID_W = 64
F_GROUPS = 8
DA_HEADS = 8
ROPE_BASE = 10000.0
TOP_K = 2
LANES = 128
SUBLANES = 8
MOD_ROWS = 16
DFT_ROWS = 64
SC_CORES = 2
SC_SUBCORES = 16
SC_INDEX_WINDOW = 128
SC_GATHER_ROWS = 32

V7X_VMEM_LIMIT = 56 * 1024 * 1024


def _cparams(sem, vmem=None):
    return pltpu.CompilerParams(dimension_semantics=sem, vmem_limit_bytes=vmem)


def _norm_mod(x, g, scale, shift):
    ms = jnp.mean(x * x, axis=-1, keepdims=True)
    y = x * lax.rsqrt(ms + NORM_EPS)
    return (y * g) * (1.0 + scale) + shift


def _mod_index(tm, n_lat_rows, seq_len, ctx_row):
    n_lat_tiles = n_lat_rows // tm
    per_seq = seq_len // tm

    def f(i):
        return jnp.where(i < n_lat_tiles, i // per_seq, ctx_row)

    return f


def _mod_kernel(c_ref, w_ref, b_ref, o_ref):
    c = c_ref[...]
    sc = c * jax.nn.sigmoid(c)
    o_ref[...] = jnp.dot(sc, w_ref[...], precision=lax.Precision.HIGHEST,
                         preferred_element_type=F32) + b_ref[...]


def _modulation(c_all, w_mod, b_mod):
    depth, d, nd = w_mod.shape
    tn = 1024
    return pl.pallas_call(
        _mod_kernel,
        out_shape=jax.ShapeDtypeStruct((depth, MOD_ROWS, nd), F32),
        grid=(depth, nd // tn),
        in_specs=[
            pl.BlockSpec((MOD_ROWS, d), lambda l, j: (0, 0)),
            pl.BlockSpec((None, d, tn), lambda l, j: (l, 0, j)),
            pl.BlockSpec((None, 1, tn), lambda l, j: (l, 0, j)),
        ],
        out_specs=pl.BlockSpec((None, MOD_ROWS, tn), lambda l, j: (l, 0, j)),
        compiler_params=_cparams(("arbitrary", "arbitrary")),
        name="modulation",
    )(c_all, w_mod, b_mod.reshape(depth, 1, nd))


def _conv_in_kernel(x_ref, mod_ref, g_ref, w_ref, b_ref, u_ref):
    d = u_ref.shape[1]
    h = _norm_mod(x_ref[...], g_ref[...], mod_ref[1:2, :], mod_ref[0:1, :])
    y = jnp.dot(h.astype(BF16), w_ref[...], preferred_element_type=F32) + b_ref[...]
    u_ref[...] = (y[:, :d] * jax.nn.sigmoid(y[:, d:])).astype(u_ref.dtype)


def _conv_in(xs, mod_l, g, w_in, b_in, *, rows, n_lat_rows, seq_len, ctx_row, tm):
    d = xs.shape[1]
    midx = _mod_index(tm, n_lat_rows, seq_len, ctx_row)
    return pl.pallas_call(
        _conv_in_kernel,
        out_shape=jax.ShapeDtypeStruct((rows, d), BF16),
        grid=(rows // tm,),
        in_specs=[
            pl.BlockSpec((tm, d), lambda i: (i, 0)),
            pl.BlockSpec((None, 6, d), lambda i: (midx(i), 0, 0)),
            pl.BlockSpec((1, d), lambda i: (0, 0)),
            pl.BlockSpec((d, 2 * d), lambda i: (0, 0)),
            pl.BlockSpec((1, 2 * d), lambda i: (0, 0)),
        ],
        out_specs=pl.BlockSpec((tm, d), lambda i: (i, 0)),
        compiler_params=_cparams(("arbitrary",), V7X_VMEM_LIMIT),
        name="conv_in",
    )(xs, mod_l, g, w_in, b_in)


def _conv_out_kernel(u_ref, up_ref, un_ref, x_ref, mod_ref, wdw_ref, bdw_ref, lng_ref, lnb_ref,
                     wo_ref, bo_ref, o_ref, ubuf, shifted, cbuf, *, n_lat_tiles, tiles_per_seq):
    tm, d = x_ref.shape
    i = pl.program_id(0)
    is_ctx = i >= n_lat_tiles
    j = i % tiles_per_seq
    first = jnp.logical_or(is_ctx, j == 0)
    last = jnp.logical_or(is_ctx, j == tiles_per_seq - 1)
    ubuf[0:CONV_HALO, :] = jnp.where(first, 0.0, up_ref[...].astype(F32))
    ubuf[CONV_HALO:CONV_HALO + tm, :] = u_ref[...].astype(F32)
    ubuf[CONV_HALO + tm:, :] = jnp.where(last, 0.0, un_ref[...].astype(F32))

    span = ubuf.shape[0] - SUBLANES
    for s in range(1, SUBLANES):
        shifted[s - 1] = ubuf[s:s + span, :]

    for c in range(d // LANES):
        cols = slice(c * LANES, (c + 1) * LANES)
        acc = jnp.zeros((tm, LANES), F32)
        for k in range(CONV_WIDTH):
            r0 = CONV_HALO - CONV_PAD + k
            s, a0 = r0 % SUBLANES, r0 - r0 % SUBLANES
            win = ubuf[a0:a0 + tm, cols] if s == 0 else shifted[s - 1, a0:a0 + tm, cols]
            acc = acc + wdw_ref[k:k + 1, cols] * win
        cbuf[:, cols] = acc + bdw_ref[:, cols]
    v = cbuf[...]
    mu = jnp.mean(v, axis=-1, keepdims=True)
    vc = v - mu
    var = jnp.mean(vc * vc, axis=-1, keepdims=True)
    y = vc * lax.rsqrt(var + LN_EPS) * lng_ref[...] + lnb_ref[...]
    y = y * jax.nn.sigmoid(y)
    out = jnp.dot(y.astype(BF16), wo_ref[...], preferred_element_type=F32) + bo_ref[...]
    o_ref[...] = x_ref[...] + mod_ref[2:3, :] * out


def _conv_out(u, xs, mod_l, w_dw, b_dw, ln_g, ln_b, w_out, b_out, *, rows, n_lat_rows, seq_len,
              ctx_row, tm):
    d = xs.shape[1]
    n_tiles = rows // tm
    n_lat_tiles = n_lat_rows // tm
    hpt = tm // CONV_HALO
    n_halo = u.shape[0] // CONV_HALO
    midx = _mod_index(tm, n_lat_rows, seq_len, ctx_row)
    kern = functools.partial(_conv_out_kernel, n_lat_tiles=n_lat_tiles, tiles_per_seq=seq_len // tm)
    return pl.pallas_call(
        kern,
        out_shape=jax.ShapeDtypeStruct(xs.shape, F32),
        grid=(n_tiles,),
        in_specs=[
            pl.BlockSpec((tm, d), lambda i: (i, 0)),
            pl.BlockSpec((CONV_HALO, d), lambda i: (jnp.maximum(i * hpt - 1, 0), 0)),
            pl.BlockSpec((CONV_HALO, d), lambda i: (jnp.minimum((i + 1) * hpt, n_halo - 1), 0)),
            pl.BlockSpec((tm, d), lambda i: (i, 0)),
            pl.BlockSpec((None, 6, d), lambda i: (midx(i), 0, 0)),
            pl.BlockSpec((CONV_WIDTH, d), lambda i: (0, 0)),
            pl.BlockSpec((1, d), lambda i: (0, 0)),
            pl.BlockSpec((1, d), lambda i: (0, 0)),
            pl.BlockSpec((1, d), lambda i: (0, 0)),
            pl.BlockSpec((d, d), lambda i: (0, 0)),
            pl.BlockSpec((1, d), lambda i: (0, 0)),
        ],
        out_specs=pl.BlockSpec((tm, d), lambda i: (i, 0)),
        scratch_shapes=[pltpu.VMEM((tm + 2 * CONV_HALO, d), F32),
                        pltpu.VMEM((SUBLANES - 1, tm + 2 * CONV_HALO - SUBLANES, d), F32),
                        pltpu.VMEM((tm, d), F32)],
        input_output_aliases={3: 0},
        compiler_params=_cparams(("arbitrary",), V7X_VMEM_LIMIT),
        name="conv_out",
    )(u, u, u, xs, mod_l, w_dw, b_dw, ln_g, ln_b, w_out, b_out)


def _cast_kernel(w_ref, o_ref):
    o_ref[...] = w_ref[...].astype(o_ref.dtype)


def _cast_layer_bf16(w, layer):
    _, n_e, a, b = w.shape
    ta = min(a, 512)
    return pl.pallas_call(
        _cast_kernel,
        out_shape=jax.ShapeDtypeStruct((n_e, a, b), BF16),
        grid=(n_e, a // ta),
        in_specs=[pl.BlockSpec((None, None, ta, b), lambda e, i: (layer, e, i, 0))],
        out_specs=pl.BlockSpec((None, ta, b), lambda e, i: (e, i, 0)),
        compiler_params=_cparams(("arbitrary", "arbitrary"), V7X_VMEM_LIMIT),
        name="cast_weights",
    )(w)


def _ffn_kernel(te_ref, rv_ref, *refs, fuse_norm, tf):
    if fuse_norm:
        x_ref, mod_ref, g_ref, wg_ref, wu_ref, wd_ref, o_ref = refs
    else:
        x_ref, wg_ref, wu_ref, wd_ref, o_ref = refs
    del te_ref
    n_rows = rv_ref[pl.program_id(0)]
    valid = n_rows > 0

    @pl.when(valid)
    def _():
        if fuse_norm:
            h = _norm_mod(x_ref[...], g_ref[...], mod_ref[4:5, :], mod_ref[3:4, :])
        else:
            row = lax.broadcasted_iota(jnp.int32, (x_ref.shape[0], 1), 0)
            h = jnp.where(row < n_rows, x_ref[...], 0.0)
        h = h.astype(BF16)
        acc = None
        for c in range(wg_ref.shape[1] // tf):
            cols = slice(c * tf, (c + 1) * tf)
            gt = jnp.dot(h, wg_ref[:, cols], preferred_element_type=F32)
            up = jnp.dot(h, wu_ref[:, cols], preferred_element_type=F32)
            a = (gt * jax.nn.sigmoid(gt) * up).astype(BF16)
            part = jnp.dot(a, wd_ref[cols, :], preferred_element_type=F32)
            acc = part if acc is None else acc + part
        if fuse_norm:
            o_ref[...] = x_ref[...] + mod_ref[5:6, :] * acc
        else:
            o_ref[...] = acc

    @pl.when(jnp.logical_not(valid))
    def _():
        o_ref[...] = jnp.zeros_like(o_ref)


def _ffn(x, tile_expert, tile_rows, wg, wu, wd, *, rows, tm, tf, mod_l=None, g=None, midx=None):
    d = x.shape[1]
    f_dim = wg.shape[2]
    fuse_norm = mod_l is not None
    resident = pl.Buffered(1)
    in_specs = [pl.BlockSpec((tm, d), lambda j, te, nv: (j, 0))]
    args = [x]
    if fuse_norm:
        in_specs += [pl.BlockSpec((None, 6, d), lambda j, te, nv: (midx(j), 0, 0)),
                     pl.BlockSpec((1, d), lambda j, te, nv: (0, 0))]
        args += [mod_l, g]
    in_specs += [
        pl.BlockSpec((None, d, f_dim), lambda j, te, nv: (te[j], 0, 0), pipeline_mode=resident),
        pl.BlockSpec((None, d, f_dim), lambda j, te, nv: (te[j], 0, 0), pipeline_mode=resident),
        pl.BlockSpec((None, f_dim, d), lambda j, te, nv: (te[j], 0, 0), pipeline_mode=resident),
    ]
    args += [wg, wu, wd]
    kern = functools.partial(_ffn_kernel, fuse_norm=fuse_norm, tf=tf)
    return pl.pallas_call(
        kern,
        out_shape=jax.ShapeDtypeStruct(x.shape, F32),
        grid_spec=pltpu.PrefetchScalarGridSpec(
            num_scalar_prefetch=2,
            grid=(rows // tm,),
            in_specs=in_specs,
            out_specs=pl.BlockSpec((tm, d), lambda j, te, nv: (j, 0)),
        ),
        input_output_aliases=({2: 0} if fuse_norm else {}),
        compiler_params=_cparams(("arbitrary",), V7X_VMEM_LIMIT),
        name="ffn_dense" if fuse_norm else "ffn_grouped",
    )(tile_expert, tile_rows, *args)


def _router_kernel(x_ref, mod_ref, g_ref, wr_ref, h_ref, meta_ref, gw_ref, cnt_ref, carry, *, n_exp):
    tm = x_ref.shape[0]
    i = pl.program_id(0)

    @pl.when(i == 0)
    def _():
        carry[...] = jnp.zeros_like(carry)

    h = _norm_mod(x_ref[...], g_ref[...], mod_ref[4:5, :], mod_ref[3:4, :])
    h_ref[...] = h
    logits = jnp.dot(h, wr_ref[...], precision=lax.Precision.HIGHEST, preferred_element_type=F32)
    lane = lax.broadcasted_iota(jnp.int32, (tm, LANES), 1).astype(F32)
    neg = jnp.float32(-jnp.inf)
    lg = jnp.where(lane < n_exp, logits, neg)
    m1 = jnp.max(lg, axis=-1, keepdims=True)
    i1 = jnp.min(jnp.where(lg == m1, lane, float(LANES)), axis=-1, keepdims=True)
    lg2 = jnp.where(lane == i1, neg, lg)
    m2 = jnp.max(lg2, axis=-1, keepdims=True)
    i2 = jnp.min(jnp.where(lg2 == m2, lane, float(LANES)), axis=-1, keepdims=True)
    e2 = jnp.exp(m2 - m1)
    w1 = 1.0 / (1.0 + e2)
    w2 = e2 * w1
    sel1 = lane == i1
    sel2 = lane == i2
    onehot = jnp.where(jnp.logical_or(sel1, sel2), 1.0, 0.0)
    rr = lax.broadcasted_iota(jnp.int32, (tm, tm), 0)
    cc = lax.broadcasted_iota(jnp.int32, (tm, tm), 1)
    tri = jnp.where(rr > cc, 1.0, 0.0).astype(BF16)
    cum = jnp.dot(tri, onehot.astype(BF16), preferred_element_type=F32) + carry[...]
    r1 = jnp.sum(jnp.where(sel1, cum, 0.0), axis=-1, keepdims=True)
    r2 = jnp.sum(jnp.where(sel2, cum, 0.0), axis=-1, keepdims=True)
    carry[...] += jnp.sum(onehot, axis=0, keepdims=True)
    meta = jnp.where(lane == 0, i1, jnp.where(lane == 1, i2, jnp.where(lane == 2, r1,
                     jnp.where(lane == 3, r2, 0.0))))
    meta_ref[...] = meta.astype(jnp.int32)
    gw_ref[...] = jnp.where(lane == 0, w1, jnp.where(lane == 1, w2, 0.0))
    cnt_ref[...] = jnp.broadcast_to(carry[...], cnt_ref.shape)


def _router(xs, mod_l, g, w_router_pad, *, rows, n_lat_rows, seq_len, ctx_row, tm, n_exp):
    d = xs.shape[1]
    midx = _mod_index(tm, n_lat_rows, seq_len, ctx_row)
    kern = functools.partial(_router_kernel, n_exp=n_exp)
    return pl.pallas_call(
        kern,
        out_shape=[jax.ShapeDtypeStruct((rows, d), F32),
                   jax.ShapeDtypeStruct((rows, LANES), jnp.int32),
                   jax.ShapeDtypeStruct((rows, LANES), F32),
                   jax.ShapeDtypeStruct((8, LANES), F32)],
        grid=(rows // tm,),
        in_specs=[
            pl.BlockSpec((tm, d), lambda i: (i, 0)),
            pl.BlockSpec((None, 6, d), lambda i: (midx(i), 0, 0)),
            pl.BlockSpec((1, d), lambda i: (0, 0)),
            pl.BlockSpec((d, LANES), lambda i: (0, 0)),
        ],
        out_specs=[pl.BlockSpec((tm, d), lambda i: (i, 0)),
                   pl.BlockSpec((tm, LANES), lambda i: (i, 0)),
                   pl.BlockSpec((tm, LANES), lambda i: (i, 0)),
                   pl.BlockSpec((8, LANES), lambda i: (0, 0))],
        scratch_shapes=[pltpu.VMEM((1, LANES), F32)],
        compiler_params=_cparams(("arbitrary",), V7X_VMEM_LIMIT),
        name="router",
    )(xs, mod_l, g, w_router_pad)


def _dispatch_rows(src, dest, n_out):
    t, d = src.shape
    n_slots = dest.shape[0]
    assert t % SC_INDEX_WINDOW == 0
    n_win = t // SC_INDEX_WINDOW
    n_workers = SC_CORES * SC_SUBCORES
    mesh = plsc.VectorSubcoreMesh(core_axis_name="core", subcore_axis_name="subcore")

    @pl.kernel(out_type=jax.ShapeDtypeStruct((n_out, d), src.dtype), mesh=mesh,
               scratch_types=[pltpu.VMEM((n_slots, SC_INDEX_WINDOW), jnp.int32),
                              pltpu.VMEM((SC_GATHER_ROWS, d), src.dtype)])
    def dispatch(src_hbm, idx_hbm, out_hbm, idx_v, buf):
        wid = lax.axis_index("core") * SC_SUBCORES + lax.axis_index("subcore")

        @pl.loop(0, (n_win - wid + n_workers - 1) // n_workers)
        def _(b):
            base = (b * n_workers + wid) * SC_INDEX_WINDOW
            pltpu.sync_copy(idx_hbm.at[:, pl.ds(base, SC_INDEX_WINDOW)], idx_v)
            for k in range(SC_INDEX_WINDOW // SC_GATHER_ROWS):
                rows = pl.ds(k * SC_GATHER_ROWS, SC_GATHER_ROWS)
                pltpu.sync_copy(src_hbm.at[pl.ds(base + k * SC_GATHER_ROWS, SC_GATHER_ROWS)], buf)
                for s in range(n_slots):
                    pltpu.sync_copy(buf, out_hbm.at[idx_v.at[s, rows]])

    return dispatch(src, dest)


def _gather_rows(src, idx):
    n = idx.shape[0]
    d = src.shape[1]
    n_workers = SC_CORES * SC_SUBCORES
    per = n // n_workers
    assert n % (n_workers * SC_INDEX_WINDOW) == 0, (n, n_workers, SC_INDEX_WINDOW)
    mesh = plsc.VectorSubcoreMesh(core_axis_name="core", subcore_axis_name="subcore")

    @pl.kernel(out_type=jax.ShapeDtypeStruct((n, d), src.dtype), mesh=mesh,
               scratch_types=[pltpu.VMEM((1, SC_INDEX_WINDOW), jnp.int32),
                              pltpu.VMEM((SC_GATHER_ROWS, d), src.dtype)])
    def gather(src_hbm, idx_hbm, out_hbm, idx_v, buf):
        wid = lax.axis_index("core") * SC_SUBCORES + lax.axis_index("subcore")

        @pl.loop(0, per // SC_INDEX_WINDOW)
        def _(b):
            base = wid * per + b * SC_INDEX_WINDOW
            pltpu.sync_copy(idx_hbm.at[:, pl.ds(base, SC_INDEX_WINDOW)], idx_v)
            for k in range(SC_INDEX_WINDOW // SC_GATHER_ROWS):
                rows = pl.ds(k * SC_GATHER_ROWS, SC_GATHER_ROWS)
                pltpu.sync_copy(src_hbm.at[idx_v.at[0, rows]], buf)
                pltpu.sync_copy(buf, out_hbm.at[pl.ds(base + k * SC_GATHER_ROWS, SC_GATHER_ROWS)])

    return gather(src, idx.reshape(1, n))


def _combine_kernel(x_ref, y1_ref, y2_ref, gw_ref, mod_ref, *rest, final):
    if final:
        fg_ref, o_ref = rest
    else:
        (o_ref,) = rest
    gw = gw_ref[...]
    y = gw[:, 0:1] * y1_ref[...] + gw[:, 1:2] * y2_ref[...]
    xn = x_ref[...] + mod_ref[5:6, :] * y
    if final:
        ms = jnp.mean(xn * xn, axis=-1, keepdims=True)
        xn = xn * lax.rsqrt(ms + NORM_EPS) * fg_ref[...]
    o_ref[...] = xn


def _combine(xs, ypair, gw, mod_l, *, rows, n_lat_rows, seq_len, ctx_row, tm, final_g=None):
    d = xs.shape[1]
    n_tiles = rows // tm
    midx = _mod_index(tm, n_lat_rows, seq_len, ctx_row)
    final = final_g is not None
    in_specs = [
        pl.BlockSpec((tm, d), lambda i: (i, 0)),
        pl.BlockSpec((tm, d), lambda i: (i, 0)),
        pl.BlockSpec((tm, d), lambda i: (i + n_tiles, 0)),
        pl.BlockSpec((tm, LANES), lambda i: (i, 0)),
        pl.BlockSpec((None, 6, d), lambda i: (midx(i), 0, 0)),
    ]
    args = [xs, ypair, ypair, gw, mod_l]
    if final:
        in_specs.append(pl.BlockSpec((1, d), lambda i: (0, 0)))
        args.append(final_g)
    out_rows = rows if final else xs.shape[0]
    return pl.pallas_call(
        functools.partial(_combine_kernel, final=final),
        out_shape=jax.ShapeDtypeStruct((out_rows, d), F32),
        grid=(n_tiles,),
        in_specs=in_specs,
        out_specs=pl.BlockSpec((tm, d), lambda i: (i, 0)),
        input_output_aliases=({} if final else {0: 0}),
        compiler_params=_cparams(("arbitrary",), V7X_VMEM_LIMIT),
        name="moe_combine",
    )(*args)


def _fnet_a_kernel(x_ref, mod_ref, g_ref, cs_ref, yc_ref, ys_ref, *, n_groups):
    h = _norm_mod(x_ref[...], g_ref[...], mod_ref[1:2, :], mod_ref[0:1, :]).astype(BF16)
    gd = cs_ref.shape[0]
    for gi in range(n_groups):
        y = jnp.dot(h[:, gi * gd:(gi + 1) * gd], cs_ref[...], preferred_element_type=F32)
        yc_ref[:, gi * gd:(gi + 1) * gd] = y[:, :gd].astype(yc_ref.dtype)
        ys_ref[:, gi * gd:(gi + 1) * gd] = y[:, gd:].astype(ys_ref.dtype)


def _fnet_a(xs, mod_l, g, cs, *, rows, n_lat_rows, seq_len, ctx_row, tm):
    d = xs.shape[1]
    gd = cs.shape[0]
    midx = _mod_index(tm, n_lat_rows, seq_len, ctx_row)
    return pl.pallas_call(
        functools.partial(_fnet_a_kernel, n_groups=d // gd),
        out_shape=[jax.ShapeDtypeStruct((rows, d), BF16)] * 2,
        grid=(rows // tm,),
        in_specs=[
            pl.BlockSpec((tm, d), lambda i: (i, 0)),
            pl.BlockSpec((None, 6, d), lambda i: (midx(i), 0, 0)),
            pl.BlockSpec((1, d), lambda i: (0, 0)),
            pl.BlockSpec((gd, 2 * gd), lambda i: (0, 0)),
        ],
        out_specs=[pl.BlockSpec((tm, d), lambda i: (i, 0))] * 2,
        compiler_params=_cparams(("arbitrary",), V7X_VMEM_LIMIT),
        name="fnet_group_dft",
    )(xs, mod_l, g, cs)


def _dft_table_kernel(ac_ref, as_ref, bc_ref, bs_ref, c_ref, sn_ref):
    ac, as_ = ac_ref[...], as_ref[...]
    bc, bs = bc_ref[...], bs_ref[...]
    c_ref[...] = (bc * ac - bs * as_).astype(c_ref.dtype)
    sn_ref[...] = (-(bs * ac + bc * as_)).astype(sn_ref.dtype)


def _dft_tables(n):
    r = DFT_ROWS
    k = jnp.arange(n, dtype=jnp.int32)[None, :]
    j1 = jnp.arange(r, dtype=jnp.int32)[:, None]
    j0 = (jnp.arange(n // r, dtype=jnp.int32) * r)[:, None]
    ang1 = ((j1 * k) % n).astype(F32) * (2.0 * math.pi / n)
    ang0 = ((j0 * k) % n).astype(F32) * (2.0 * math.pi / n)
    scale = 1.0 / math.sqrt(n)
    ac, as_ = jnp.cos(ang1), jnp.sin(ang1)
    bc = (jnp.cos(ang0) * scale).reshape(n // r, 1, n)
    bs = (jnp.sin(ang0) * scale).reshape(n // r, 1, n)
    return pl.pallas_call(
        _dft_table_kernel,
        out_shape=[jax.ShapeDtypeStruct((n, n), BF16)] * 2,
        grid=(n // r,),
        in_specs=[
            pl.BlockSpec((r, n), lambda i: (0, 0)),
            pl.BlockSpec((r, n), lambda i: (0, 0)),
            pl.BlockSpec((None, 1, n), lambda i: (i, 0, 0)),
            pl.BlockSpec((None, 1, n), lambda i: (i, 0, 0)),
        ],
        out_specs=[pl.BlockSpec((r, n), lambda i: (i, 0))] * 2,
        compiler_params=_cparams(("arbitrary",)),
        name="dft_tables",
    )(ac, as_, bc, bs)


def _fnet_b_kernel(c_ref, sn_ref, yc_ref, ys_ref, x_ref, mod_ref, wf_ref, bf_ref, o_ref):
    z = (jnp.dot(c_ref[...], yc_ref[...], preferred_element_type=F32)
         + jnp.dot(sn_ref[...], ys_ref[...], preferred_element_type=F32))
    out = jnp.dot(z.astype(BF16), wf_ref[...], preferred_element_type=F32) + bf_ref[...]
    o_ref[...] = x_ref[...] + mod_ref[2:3, :] * out


def _fnet_b(ctab, stab, yc, ys, xs, mod_l, wf, bf, *, n_batch, seq_len, row_off, mod_ctx_row, tm):
    d = xs.shape[1]
    n_i = seq_len // tm
    off_m = row_off // tm
    off_b = row_off // seq_len
    resident = pl.Buffered(1)

    def mrow(b):
        return b if mod_ctx_row is None else mod_ctx_row

    return pl.pallas_call(
        _fnet_b_kernel,
        out_shape=jax.ShapeDtypeStruct(xs.shape, F32),
        grid=(n_batch, n_i),
        in_specs=[
            pl.BlockSpec((tm, seq_len), lambda b, i: (i, 0)),
            pl.BlockSpec((tm, seq_len), lambda b, i: (i, 0)),
            pl.BlockSpec((seq_len, d), lambda b, i: (off_b + b, 0), pipeline_mode=resident),
            pl.BlockSpec((seq_len, d), lambda b, i: (off_b + b, 0), pipeline_mode=resident),
            pl.BlockSpec((tm, d), lambda b, i: (off_m + b * n_i + i, 0)),
            pl.BlockSpec((None, 6, d), lambda b, i: (mrow(b), 0, 0)),
            pl.BlockSpec((d, d), lambda b, i: (0, 0)),
            pl.BlockSpec((1, d), lambda b, i: (0, 0)),
        ],
        out_specs=pl.BlockSpec((tm, d), lambda b, i: (off_m + b * n_i + i, 0)),
        input_output_aliases={4: 0},
        compiler_params=_cparams(("arbitrary", "arbitrary"), V7X_VMEM_LIMIT),
        name="fnet_seq_dft",
    )(ctab, stab, yc, ys, xs, mod_l, wf, bf)


def _qkv_kernel(x_ref, mod_ref, g_ref, w_ref, cos_ref, sin_ref, q_ref, k_ref, v_ref, *,
                n_lat_tiles, q_scale):
    tm, d = x_ref.shape
    i = pl.program_id(0)
    is_ctx = i >= n_lat_tiles
    h = _norm_mod(x_ref[...], g_ref[...], mod_ref[1:2, :], mod_ref[0:1, :])
    y = jnp.dot(h.astype(BF16), w_ref[...], preferred_element_type=F32)
    tw = cos_ref.shape[1]
    cos = jnp.where(is_ctx, 1.0, cos_ref[...])
    sin = jnp.where(is_ctx, 0.0, sin_ref[...])
    lane = lax.broadcasted_iota(jnp.int32, (tm, tw), 1)
    half = tw // 8
    lo = (lane % (2 * half)) < half

    def rope(t):
        rot = jnp.where(lo, -pltpu.roll(t, tw - half, 1), pltpu.roll(t, half, 1))
        return t * cos + rot * sin

    for hd in range(d // tw):
        cols = slice(hd * tw, (hd + 1) * tw)
        q_ref[:, cols] = (rope(y[:, hd * tw:(hd + 1) * tw]) * q_scale).astype(q_ref.dtype)
        k_ref[:, cols] = rope(y[:, d + hd * tw:d + (hd + 1) * tw]).astype(k_ref.dtype)
    v_ref[...] = y[:, 2 * d:].astype(v_ref.dtype)


def _qkv(xs, mod_l, g, w_qkv, cos_t, sin_t, *, rows, n_lat_rows, seq_len, ctx_row, tm, q_scale):
    d = xs.shape[1]
    midx = _mod_index(tm, n_lat_rows, seq_len, ctx_row)
    per_seq = seq_len // tm
    n_lat_tiles = n_lat_rows // tm
    tw = cos_t.shape[1]

    def pos(i):
        return jnp.where(i < n_lat_tiles, i % per_seq, 0)

    kern = functools.partial(_qkv_kernel, n_lat_tiles=n_lat_tiles, q_scale=q_scale)
    return pl.pallas_call(
        kern,
        out_shape=[jax.ShapeDtypeStruct((rows, d), BF16)] * 3,
        grid=(rows // tm,),
        in_specs=[
            pl.BlockSpec((tm, d), lambda i: (i, 0)),
            pl.BlockSpec((None, 6, d), lambda i: (midx(i), 0, 0)),
            pl.BlockSpec((1, d), lambda i: (0, 0)),
            pl.BlockSpec((d, 3 * d), lambda i: (0, 0)),
            pl.BlockSpec((tm, tw), lambda i: (pos(i), 0)),
            pl.BlockSpec((tm, tw), lambda i: (pos(i), 0)),
        ],
        out_specs=[pl.BlockSpec((tm, d), lambda i: (i, 0))] * 3,
        compiler_params=_cparams(("arbitrary",), V7X_VMEM_LIMIT),
        name="attn_qkv",
    )(xs, mod_l, g, w_qkv, cos_t, sin_t)


def _dot_nt(a, b):
    return lax.dot_general(a, b, (((1,), (1,)), ((), ())), preferred_element_type=F32)


def _flash_kernel(lam_ref, sg_ref, q_ref, kc_ref, vc_ref, *rest, tk, n_kx, lam_init):
    if n_kx:
        kx_ref, vx_ref, o_ref, s_even, s_odd = rest
    else:
        (o_ref,) = rest
    tq, dh = q_ref.shape
    q = q_ref[...]
    lane = lax.broadcasted_iota(jnp.int32, (tq, dh), 1)
    zero = jnp.zeros_like(q)
    qa = jnp.where(lane < dh // 2, q, zero)
    qb = jnp.where(lane >= dh // 2, q, zero)

    def v_ext(v):
        return jnp.concatenate([v, jnp.ones_like(v)], axis=1)

    def init(k, v):
        ve = v_ext(v)
        out = []
        for qm in (qa, qb):
            s = _dot_nt(qm, k)
            m = jnp.max(s, axis=-1, keepdims=True)
            p = jnp.exp2(s - m).astype(BF16)
            out += [m, jnp.dot(p, ve, preferred_element_type=F32)]
        return tuple(out)

    def scores(t, s_ref):
        k = kx_ref[t * tk:(t + 1) * tk, :]
        s_ref[0] = _dot_nt(qa, k)
        s_ref[1] = _dot_nt(qb, k)

    def update(t, s_ref, carry):
        ve = v_ext(vx_ref[t * tk:(t + 1) * tk, :])
        out = []
        for mp, (m, acc) in enumerate((carry[0:2], carry[2:4])):
            s = s_ref[mp]
            m_new = jnp.maximum(m, jnp.max(s, axis=-1, keepdims=True))
            p = jnp.exp2(s - m_new).astype(BF16)
            acc = jnp.exp2(m - m_new) * acc + jnp.dot(p, ve, preferred_element_type=F32)
            out += [m_new, acc]
        return tuple(out)

    carry = init(kc_ref[...], vc_ref[...])
    if n_kx:
        bufs = (s_even, s_odd)
        scores(0, bufs[0])
        for t in range(n_kx):
            if t + 1 < n_kx:
                scores(t + 1, bufs[(t + 1) % 2])
            carry = update(t, bufs[t % 2], carry)
    _, a1, _, a2 = carry
    lv = lam_ref[...]
    lam = (jnp.exp(jnp.sum(lv[0:1, :] * lv[1:2, :], axis=-1, keepdims=True))
           - jnp.exp(jnp.sum(lv[2:3, :] * lv[3:4, :], axis=-1, keepdims=True)) + lam_init)
    o = a1[:, :dh] / a1[:, dh:] - lam * (a2[:, :dh] / a2[:, dh:])
    ms = jnp.mean(o * o, axis=-1, keepdims=True)
    o = o * lax.rsqrt(ms + NORM_EPS) * sg_ref[...] * (1.0 - lam_init)
    o_ref[...] = o.astype(o_ref.dtype)


def _flash(q, k, v, lam_vecs, subln_g, out_init, *, n_batch, n_heads, q_len, q_off, kc_len, kc_off,
           kx_len, tq, tk, lam_init):
    dh = q.shape[1] // n_heads
    n_q = q_len // tq
    qo = q_off // tq
    kco = kc_off // kc_len
    n_kx = kx_len // tk if kx_len else 0
    in_specs = [
        pl.BlockSpec(memory_space=pl.ANY),
        pl.BlockSpec(lam_vecs.shape, lambda b, h, i: (0, 0)),
        pl.BlockSpec((1, dh), lambda b, h, i: (0, 0)),
        pl.BlockSpec((tq, dh), lambda b, h, i: (qo + b * n_q + i, h)),
        pl.BlockSpec((kc_len, dh), lambda b, h, i: (kco + b, h)),
        pl.BlockSpec((kc_len, dh), lambda b, h, i: (kco + b, h)),
    ]
    args = [out_init, lam_vecs, subln_g, q, k, v]
    if n_kx:
        in_specs += [pl.BlockSpec((kx_len, dh), lambda b, h, i: (b, h)),
                     pl.BlockSpec((kx_len, dh), lambda b, h, i: (b, h))]
        args += [k, v]
    kern = functools.partial(_flash_kernel_alias, tk=tk, n_kx=n_kx, lam_init=lam_init)
    return pl.pallas_call(
        kern,
        out_shape=jax.ShapeDtypeStruct(out_init.shape, out_init.dtype),
        grid=(n_batch, n_heads, n_q),
        in_specs=in_specs,
        out_specs=pl.BlockSpec((tq, dh), lambda b, h, i: (qo + b * n_q + i, h)),
        scratch_shapes=([pltpu.VMEM((2, tq, tk), F32)] * 2 if n_kx else []),
        input_output_aliases={0: 0},
        compiler_params=_cparams(("arbitrary", "arbitrary", "arbitrary"), V7X_VMEM_LIMIT),
        name="diff_attn_latent" if n_kx else "diff_attn_ctx",
    )(*args)


def _flash_kernel_alias(out_init_ref, *refs, tk, n_kx, lam_init):
    del out_init_ref
    _flash_kernel(*refs, tk=tk, n_kx=n_kx, lam_init=lam_init)


def _proj_residual_kernel(a_ref, w_ref, x_ref, mod_ref, o_ref):
    out = jnp.dot(a_ref[...], w_ref[...], preferred_element_type=F32)
    o_ref[...] = x_ref[...] + mod_ref[2:3, :] * out


def _proj_residual(a, w, xs, mod_l, *, rows, n_lat_rows, seq_len, ctx_row, tm):
    d = xs.shape[1]
    midx = _mod_index(tm, n_lat_rows, seq_len, ctx_row)
    return pl.pallas_call(
        _proj_residual_kernel,
        out_shape=jax.ShapeDtypeStruct(xs.shape, F32),
        grid=(rows // tm,),
        in_specs=[
            pl.BlockSpec((tm, d), lambda i: (i, 0)),
            pl.BlockSpec((d, d), lambda i: (0, 0)),
            pl.BlockSpec((tm, d), lambda i: (i, 0)),
            pl.BlockSpec((None, 6, d), lambda i: (midx(i), 0, 0)),
        ],
        out_specs=pl.BlockSpec((tm, d), lambda i: (i, 0)),
        input_output_aliases={2: 0},
        compiler_params=_cparams(("arbitrary",), V7X_VMEM_LIMIT),
        name="attn_out_proj",
    )(a, w, xs, mod_l)


def _rope_tables(n_tokens, width):
    rope_axis_dim = 32
    freqs = rope_axis_dim // 2
    rows = n_tokens // GRID_W
    row = jnp.repeat(jnp.arange(rows, dtype=F32), GRID_W)
    col = jnp.tile(jnp.arange(GRID_W, dtype=F32), rows)
    inv_freq = 1.0 / (ROPE_BASE ** (jnp.arange(freqs, dtype=F32) * 2.0 / rope_axis_dim))
    ang = jnp.stack([row[:, None] * inv_freq, col[:, None] * inv_freq], axis=1)
    ang = jnp.stack([ang, ang], axis=2).reshape(n_tokens, 4 * freqs)
    ang = jnp.tile(ang, (1, width // (4 * freqs)))
    return jnp.cos(ang), jnp.sin(ang)


def _dense_ffn_layer(xs, mod_l, g, wg, wu, wd, geo, rows):
    tm = geo["tm_ffn"]
    n_tiles = rows // tm
    te = jnp.zeros((n_tiles,), jnp.int32)
    tr = jnp.full((n_tiles,), tm, jnp.int32)
    midx = _mod_index(tm, geo["n_lat_rows"], geo["seq_len"], geo["ctx_row"])
    return _ffn(xs, te, tr, wg, wu, wd, rows=rows, tm=tm, tf=geo["tf"], mod_l=mod_l, g=g, midx=midx)


def _moe_layer(xs, mod_l, g, w_router, wg, wu, wd, geo, rows, final_g=None):
    d = xs.shape[1]
    n_exp = w_router.shape[1]
    tm = geo["tm_ffn"]
    common = dict(rows=rows, n_lat_rows=geo["n_lat_rows"], seq_len=geo["seq_len"], ctx_row=geo["ctx_row"])
    wr = jnp.zeros((d, LANES), F32).at[:, :n_exp].set(w_router)
    h, meta, gw, cnt = _router(xs, mod_l, g, wr, tm=geo["tm_router"], n_exp=n_exp, **common)
    counts = cnt[0, :n_exp].astype(jnp.int32)
    padded = ((counts + tm - 1) // tm) * tm
    ends = jnp.cumsum(padded)
    offs = ends - padded
    n_sorted_tiles = (TOP_K * rows) // tm + n_exp
    tile_start = jnp.arange(n_sorted_tiles, dtype=jnp.int32) * tm
    tile_expert = jnp.minimum(jnp.sum(tile_start[:, None] >= ends[None, :], axis=1), n_exp - 1).astype(jnp.int32)
    tile_rows = jnp.clip((offs + counts)[tile_expert] - tile_start, 0, tm).astype(jnp.int32)
    tile_rows = jnp.where(tile_start < ends[-1], tile_rows, 0)
    e1, e2, r1, r2 = meta[:, 0], meta[:, 1], meta[:, 2], meta[:, 3]
    dest = jnp.stack([offs[e1] + r1, offs[e2] + r2]).astype(jnp.int32)
    hs = _dispatch_rows(h, dest, n_sorted_tiles * tm)
    ys = _ffn(hs, tile_expert, tile_rows, wg, wu, wd, rows=n_sorted_tiles * tm, tm=tm, tf=geo["tf"])
    ypair = _gather_rows(ys, dest.reshape(TOP_K * rows))
    return _combine(xs, ypair, gw, mod_l, tm=geo["tm_row"], final_g=final_g, **common)


def kernel(x, c, ctx, c_ctx, w_mod, b_mod, norm_g, conv_w_in, conv_b_in, conv_w_dw, conv_b_dw,
           conv_ln_g, conv_ln_b, conv_w_out, conv_b_out, fnet_w, fnet_b, attn_w_qkv, attn_lambda,
           attn_subln_g, attn_w_o, ffn_w_gate, ffn_w_up, ffn_w_down, moe_w_router, moe_w_gate,
           moe_w_up, moe_w_down, final_g):
    b_, n, d = x.shape
    n_ctx = ctx.shape[1]
    depth = w_mod.shape[0]
    n_lat = b_ * n
    n_all = n_lat + b_ * n_ctx
    assert b_ < MOD_ROWS and d % LANES == 0 and n % GRID_W == 0 and n % n_ctx == 0
    assert d // DA_HEADS == LANES
    geo = dict(n_lat_rows=n_lat, seq_len=n, ctx_row=b_,
               tm_row=min(512, n_ctx * b_, n), tm_ffn=min(1024, n_ctx * b_, n),
               tm_router=min(512, n_ctx * b_, n), tf=min(512, ffn_w_gate.shape[2]),
               tm_conv=n_ctx)

    xs = jnp.concatenate([x.reshape(n_lat, d), ctx.reshape(b_ * n_ctx, d)], axis=0)
    c_all = jnp.zeros((MOD_ROWS, d), F32).at[:b_].set(c).at[b_].set(c_ctx)
    mod = _modulation(c_all, w_mod, b_mod).reshape(depth, MOD_ROWS, 6, d)

    out = None
    for i in range(depth):
        need_ctx = i < depth - 1
        rows = n_all if need_ctx else n_lat
        kind = i % 3
        mod_l = mod[i]
        common = dict(rows=rows, n_lat_rows=n_lat, seq_len=n, ctx_row=b_)
        g0 = norm_g[i, 0].reshape(1, d)
        g1 = norm_g[i, 1].reshape(1, d)
        j = i // 3
        if kind == 0:
            u = _conv_in(xs, mod_l, g0, conv_w_in[j].astype(BF16), conv_b_in[j].reshape(1, 2 * d),
                         tm=geo["tm_row"], **common)
            xs = _conv_out(u, xs, mod_l, conv_w_dw[j], conv_b_dw[j].reshape(1, d),
                           conv_ln_g[j].reshape(1, d), conv_ln_b[j].reshape(1, d),
                           conv_w_out[j].astype(BF16), conv_b_out[j].reshape(1, d),
                           tm=geo["tm_conv"], **common)
        elif kind == 1:
            gd = d // F_GROUPS
            kk = jnp.arange(gd, dtype=jnp.int32)
            ang = ((kk[:, None] * kk[None, :]) % gd).astype(F32) * (2.0 * math.pi / gd)
            cs = (jnp.concatenate([jnp.cos(ang), jnp.sin(ang)], axis=1) / math.sqrt(gd)).astype(BF16)
            yc, ys = _fnet_a(xs, mod_l, g0, cs, tm=geo["tm_row"], **common)
            wf = fnet_w[j].astype(BF16)
            bf = fnet_b[j].reshape(1, d)
            ct, st = _dft_tables(n)
            xs = _fnet_b(ct, st, yc, ys, xs, mod_l, wf, bf, n_batch=b_, seq_len=n, row_off=0,
                         mod_ctx_row=None, tm=geo["tm_row"])
            if need_ctx:
                ct, st = _dft_tables(n_ctx)
                xs = _fnet_b(ct, st, yc, ys, xs, mod_l, wf, bf, n_batch=b_, seq_len=n_ctx,
                             row_off=n_lat, mod_ctx_row=b_, tm=n_ctx)
        else:
            lam_init = 0.8 - 0.6 * math.exp(-0.3 * i)
            dh = d // DA_HEADS
            cos_t, sin_t = _rope_tables(n, LANES)
            q, k, v = _qkv(xs, mod_l, g0, attn_w_qkv[j].astype(BF16), cos_t, sin_t,
                           tm=geo["tm_row"], q_scale=(dh // 2) ** -0.5 * math.log2(math.e), **common)
            sg = attn_subln_g[j].reshape(1, dh)
            o = jnp.zeros((rows, d), BF16)
            fl = dict(n_batch=b_, n_heads=DA_HEADS, kc_len=n_ctx, kc_off=n_lat, lam_init=lam_init)
            o = _flash(q, k, v, attn_lambda[j], sg, o, q_len=n, q_off=0, kx_len=n,
                       tq=min(512, n), tk=min(1024, n), **fl)
            if need_ctx:
                o = _flash(q, k, v, attn_lambda[j], sg, o, q_len=n_ctx, q_off=n_lat, kx_len=0,
                           tq=n_ctx, tk=n_ctx, **fl)
            xs = _proj_residual(o, attn_w_o[j].astype(BF16), xs, mod_l, tm=geo["tm_row"], **common)

        j = i // 2
        if i % 2 == 0:
            wg, wu, wd = (_cast_layer_bf16(w[:, None], j) for w in (ffn_w_gate, ffn_w_up, ffn_w_down))
            xs = _dense_ffn_layer(xs, mod_l, g1, wg, wu, wd, geo, rows)
        else:
            fg = final_g.reshape(1, d) if i == depth - 1 else None
            xs, w32 = lax.optimization_barrier((xs, (moe_w_gate, moe_w_up, moe_w_down)))
            wg, wu, wd = (_cast_layer_bf16(w, j) for w in w32)
            xs, wg, wu, wd = lax.optimization_barrier((xs, wg, wu, wd))
            res = _moe_layer(xs, mod_l, g1, moe_w_router[j], wg, wu, wd, geo, rows, final_g=fg)
            if fg is not None:
                out = res
            else:
                xs = res
    if out is None:
        ms = jnp.mean(jnp.square(xs[:n_lat]), axis=-1, keepdims=True)
        out = xs[:n_lat] * lax.rsqrt(ms + NORM_EPS) * final_g
    return out.reshape(b_, n, d)
```

```python
import functools
import math

import jax
import jax.numpy as jnp
from jax import lax
from jax.experimental import pallas as pl
from jax.experimental.pallas import tpu as pltpu
from jax.experimental.pallas import tpu_sc as plsc

F32 = jnp.float32
BF16 = jnp.bfloat16

NORM_EPS = 1e-6
LN_EPS = 1e-5
CONV_WIDTH = 31
CONV_PAD = (CONV_WIDTH - 1) // 2
CONV_HALO = 16
GRID_W = 64
F_GROUPS = 8
DA_HEADS = 8
ROPE_BASE = 10000.0
TOP_K = 2
LANES = 128
SUBLANES = 8
MOD_ROWS = 16
DFT_ROWS = 64
SC_CORES = 2
SC_SUBCORES = 16
SC_INDEX_WINDOW = 128
SC_GATHER_ROWS = 32

V7X_VMEM_LIMIT = 56 * 1024 * 1024


def _cparams(sem, vmem=None):
    return pltpu.CompilerParams(dimension_semantics=sem, vmem_limit_bytes=vmem)


def _norm_mod(x, g, scale, shift):
    ms = jnp.mean(x * x, axis=-1, keepdims=True)
    y = x * lax.rsqrt(ms + NORM_EPS)
    return (y * g) * (1.0 + scale) + shift


def _pack_bf16_pairs(h):
    half = h.shape[1] // 2
    hb = h.astype(BF16)
    hi = lax.bitcast_convert_type(hb[:, :half].astype(F32), jnp.uint32)
    lo = lax.bitcast_convert_type(hb[:, half:].astype(F32), jnp.uint32)
    return lax.bitcast_convert_type(hi | (lo >> 16), jnp.int32)


def _unpack_bf16_pairs(words):
    w = lax.bitcast_convert_type(words, jnp.uint32)
    hi = lax.bitcast_convert_type(w & jnp.uint32(0xFFFF0000), F32)
    lo = lax.bitcast_convert_type(w << 16, F32)
    return jnp.concatenate([hi, lo], axis=1).astype(BF16)


def _mod_index(tm, n_lat_rows, seq_len, ctx_row):
    n_lat_tiles = n_lat_rows // tm
    per_seq = seq_len // tm

    def f(i):
        return jnp.where(i < n_lat_tiles, i // per_seq, ctx_row)

    return f


def _mod_kernel(c_ref, w_ref, b_ref, o_ref):
    c = c_ref[...]
    sc = c * jax.nn.sigmoid(c)
    o_ref[...] = jnp.dot(sc, w_ref[...], precision=lax.Precision.HIGHEST,
                         preferred_element_type=F32) + b_ref[...]


def _modulation(c_all, w_mod, b_mod):
    depth, d, nd = w_mod.shape
    tn = 1024
    return pl.pallas_call(
        _mod_kernel,
        out_shape=jax.ShapeDtypeStruct((depth, MOD_ROWS, nd), F32),
        grid=(depth, nd // tn),
        in_specs=[
            pl.BlockSpec((MOD_ROWS, d), lambda l, j: (0, 0)),
            pl.BlockSpec((None, d, tn), lambda l, j: (l, 0, j)),
            pl.BlockSpec((None, 1, tn), lambda l, j: (l, 0, j)),
        ],
        out_specs=pl.BlockSpec((None, MOD_ROWS, tn), lambda l, j: (l, 0, j)),
        compiler_params=_cparams(("arbitrary", "arbitrary")),
        name="modulation",
    )(c_all, w_mod, b_mod.reshape(depth, 1, nd))


def _conv_in_kernel(x_ref, mod_ref, g_ref, w_ref, b_ref, u_ref):
    d = u_ref.shape[1]
    h = _norm_mod(x_ref[...], g_ref[...], mod_ref[1:2, :], mod_ref[0:1, :])
    y = jnp.dot(h.astype(BF16), w_ref[...], preferred_element_type=F32) + b_ref[...]
    u_ref[...] = (y[:, :d] * jax.nn.sigmoid(y[:, d:])).astype(u_ref.dtype)


def _conv_in(xs, mod_l, g, w_in, b_in, *, rows, n_lat_rows, seq_len, ctx_row, tm):
    d = xs.shape[1]
    midx = _mod_index(tm, n_lat_rows, seq_len, ctx_row)
    return pl.pallas_call(
        _conv_in_kernel,
        out_shape=jax.ShapeDtypeStruct((rows, d), BF16),
        grid=(rows // tm,),
        in_specs=[
            pl.BlockSpec((tm, d), lambda i: (i, 0)),
            pl.BlockSpec((None, 6, d), lambda i: (midx(i), 0, 0)),
            pl.BlockSpec((1, d), lambda i: (0, 0)),
            pl.BlockSpec((d, 2 * d), lambda i: (0, 0)),
            pl.BlockSpec((1, 2 * d), lambda i: (0, 0)),
        ],
        out_specs=pl.BlockSpec((tm, d), lambda i: (i, 0)),
        compiler_params=_cparams(("arbitrary",), V7X_VMEM_LIMIT),
        name="conv_in",
    )(xs, mod_l, g, w_in, b_in)


def _conv_out_kernel(u_ref, up_ref, un_ref, x_ref, mod_ref, wdw_ref, bdw_ref, lng_ref, lnb_ref,
                     wo_ref, bo_ref, o_ref, ubuf, shifted, cbuf, *, n_lat_tiles, tiles_per_seq):
    tm, d = x_ref.shape
    i = pl.program_id(0)
    is_ctx = i >= n_lat_tiles
    j = i % tiles_per_seq
    first = jnp.logical_or(is_ctx, j == 0)
    last = jnp.logical_or(is_ctx, j == tiles_per_seq - 1)
    ubuf[0:CONV_HALO, :] = jnp.where(first, 0.0, up_ref[...].astype(F32))
    ubuf[CONV_HALO:CONV_HALO + tm, :] = u_ref[...].astype(F32)
    ubuf[CONV_HALO + tm:, :] = jnp.where(last, 0.0, un_ref[...].astype(F32))

    span = ubuf.shape[0] - SUBLANES
    for s in range(1, SUBLANES):
        shifted[s - 1] = ubuf[s:s + span, :]

    for c in range(d // LANES):
        cols = slice(c * LANES, (c + 1) * LANES)
        acc = jnp.zeros((tm, LANES), F32)
        for k in range(CONV_WIDTH):
            r0 = CONV_HALO - CONV_PAD + k
            s, a0 = r0 % SUBLANES, r0 - r0 % SUBLANES
            win = ubuf[a0:a0 + tm, cols] if s == 0 else shifted[s - 1, a0:a0 + tm, cols]
            acc = acc + wdw_ref[k:k + 1, cols] * win
        cbuf[:, cols] = acc + bdw_ref[:, cols]
    v = cbuf[...]
    mu = jnp.mean(v, axis=-1, keepdims=True)
    vc = v - mu
    var = jnp.mean(vc * vc, axis=-1, keepdims=True)
    y = vc * lax.rsqrt(var + LN_EPS) * lng_ref[...] + lnb_ref[...]
    y = y * jax.nn.sigmoid(y)
    out = jnp.dot(y.astype(BF16), wo_ref[...], preferred_element_type=F32) + bo_ref[...]
    o_ref[...] = x_ref[...] + mod_ref[2:3, :] * out


def _conv_out(u, xs, mod_l, w_dw, b_dw, ln_g, ln_b, w_out, b_out, *, rows, n_lat_rows, seq_len,
              ctx_row, tm):
    d = xs.shape[1]
    n_tiles = rows // tm
    n_lat_tiles = n_lat_rows // tm
    hpt = tm // CONV_HALO
    n_halo = u.shape[0] // CONV_HALO
    midx = _mod_index(tm, n_lat_rows, seq_len, ctx_row)
    kern = functools.partial(_conv_out_kernel, n_lat_tiles=n_lat_tiles, tiles_per_seq=seq_len // tm)
    return pl.pallas_call(
        kern,
        out_shape=jax.ShapeDtypeStruct(xs.shape, F32),
        grid=(n_tiles,),
        in_specs=[
            pl.BlockSpec((tm, d), lambda i: (i, 0)),
            pl.BlockSpec((CONV_HALO, d), lambda i: (jnp.maximum(i * hpt - 1, 0), 0)),
            pl.BlockSpec((CONV_HALO, d), lambda i: (jnp.minimum((i + 1) * hpt, n_halo - 1), 0)),
            pl.BlockSpec((tm, d), lambda i: (i, 0)),
            pl.BlockSpec((None, 6, d), lambda i: (midx(i), 0, 0)),
            pl.BlockSpec((CONV_WIDTH, d), lambda i: (0, 0)),
            pl.BlockSpec((1, d), lambda i: (0, 0)),
            pl.BlockSpec((1, d), lambda i: (0, 0)),
            pl.BlockSpec((1, d), lambda i: (0, 0)),
            pl.BlockSpec((d, d), lambda i: (0, 0)),
            pl.BlockSpec((1, d), lambda i: (0, 0)),
        ],
        out_specs=pl.BlockSpec((tm, d), lambda i: (i, 0)),
        scratch_shapes=[pltpu.VMEM((tm + 2 * CONV_HALO, d), F32),
                        pltpu.VMEM((SUBLANES - 1, tm + 2 * CONV_HALO - SUBLANES, d), F32),
                        pltpu.VMEM((tm, d), F32)],
        input_output_aliases={3: 0},
        compiler_params=_cparams(("arbitrary",), V7X_VMEM_LIMIT),
        name="conv_out",
    )(u, u, u, xs, mod_l, w_dw, b_dw, ln_g, ln_b, w_out, b_out)


def _cast_kernel(w_ref, o_ref):
    o_ref[...] = w_ref[...].astype(o_ref.dtype)


def _cast_layer_bf16(w, layer):
    _, n_e, a, b = w.shape
    ta = min(a, 512)
    return pl.pallas_call(
        _cast_kernel,
        out_shape=jax.ShapeDtypeStruct((n_e, a, b), BF16),
        grid=(n_e, a // ta),
        in_specs=[pl.BlockSpec((None, None, ta, b), lambda e, i: (layer, e, i, 0))],
        out_specs=pl.BlockSpec((None, ta, b), lambda e, i: (e, i, 0)),
        compiler_params=_cparams(("arbitrary", "arbitrary"), V7X_VMEM_LIMIT),
        name="cast_weights",
    )(w)


def _ffn_kernel(te_ref, rv_ref, *refs, fuse_norm, tf):
    if fuse_norm:
        x_ref, mod_ref, g_ref, wg_ref, wu_ref, wd_ref, o_ref = refs
    else:
        x_ref, wg_ref, wu_ref, wd_ref, o_ref = refs
    del te_ref
    n_rows = rv_ref[pl.program_id(0)]
    valid = n_rows > 0

    @pl.when(valid)
    def _():
        if fuse_norm:
            h = _norm_mod(x_ref[...], g_ref[...], mod_ref[4:5, :], mod_ref[3:4, :])
        else:
            row = lax.broadcasted_iota(jnp.int32, (x_ref.shape[0], 1), 0)
            h = _unpack_bf16_pairs(jnp.where(row < n_rows, x_ref[...], 0))
        h = h.astype(BF16)
        acc = None
        for c in range(wg_ref.shape[1] // tf):
            cols = slice(c * tf, (c + 1) * tf)
            gt = jnp.dot(h, wg_ref[:, cols], preferred_element_type=F32)
            up = jnp.dot(h, wu_ref[:, cols], preferred_element_type=F32)
            a = (gt * jax.nn.sigmoid(gt) * up).astype(BF16)
            part = jnp.dot(a, wd_ref[cols, :], preferred_element_type=F32)
            acc = part if acc is None else acc + part
        if fuse_norm:
            o_ref[...] = x_ref[...] + mod_ref[5:6, :] * acc
        else:
            o_ref[...] = acc

    @pl.when(jnp.logical_not(valid))
    def _():
        o_ref[...] = jnp.zeros_like(o_ref)


def _ffn(x, tile_expert, tile_rows, wg, wu, wd, *, rows, tm, tf, mod_l=None, g=None, midx=None):
    d = wg.shape[1]
    f_dim = wg.shape[2]
    fuse_norm = mod_l is not None
    resident = pl.Buffered(1)
    in_specs = [pl.BlockSpec((tm, x.shape[1]), lambda j, te, nv: (j, 0))]
    args = [x]
    if fuse_norm:
        in_specs += [pl.BlockSpec((None, 6, d), lambda j, te, nv: (midx(j), 0, 0)),
                     pl.BlockSpec((1, d), lambda j, te, nv: (0, 0))]
        args += [mod_l, g]
    in_specs += [
        pl.BlockSpec((None, d, f_dim), lambda j, te, nv: (te[j], 0, 0), pipeline_mode=resident),
        pl.BlockSpec((None, d, f_dim), lambda j, te, nv: (te[j], 0, 0), pipeline_mode=resident),
        pl.BlockSpec((None, f_dim, d), lambda j, te, nv: (te[j], 0, 0), pipeline_mode=resident),
    ]
    args += [wg, wu, wd]
    kern = functools.partial(_ffn_kernel, fuse_norm=fuse_norm, tf=tf)
    return pl.pallas_call(
        kern,
        out_shape=jax.ShapeDtypeStruct((x.shape[0], d), F32),
        grid_spec=pltpu.PrefetchScalarGridSpec(
            num_scalar_prefetch=2,
            grid=(rows // tm,),
            in_specs=in_specs,
            out_specs=pl.BlockSpec((tm, d), lambda j, te, nv: (j, 0)),
        ),
        input_output_aliases=({2: 0} if fuse_norm else {}),
        compiler_params=_cparams(("arbitrary",), V7X_VMEM_LIMIT),
        name="ffn_dense" if fuse_norm else "ffn_grouped",
    )(tile_expert, tile_rows, *args)


def _router_kernel(x_ref, mod_ref, g_ref, wr_ref, h_ref, meta_ref, gw_ref, cnt_ref, carry, *, n_exp):
    tm = x_ref.shape[0]
    i = pl.program_id(0)

    @pl.when(i == 0)
    def _():
        carry[...] = jnp.zeros_like(carry)

    h = _norm_mod(x_ref[...], g_ref[...], mod_ref[4:5, :], mod_ref[3:4, :])
    h_ref[...] = _pack_bf16_pairs(h)
    h_hi = h.astype(BF16)
    h_lo = (h - h_hi.astype(F32)).astype(BF16)
    logits = (jnp.dot(h_hi, wr_ref[0], preferred_element_type=F32)
              + (jnp.dot(h_hi, wr_ref[1], preferred_element_type=F32)
                 + jnp.dot(h_lo, wr_ref[0], preferred_element_type=F32)))
    lane = lax.broadcasted_iota(jnp.int32, (tm, LANES), 1).astype(F32)
    neg = jnp.float32(-jnp.inf)
    lg = jnp.where(lane < n_exp, logits, neg)
    m1 = jnp.max(lg, axis=-1, keepdims=True)
    i1 = jnp.min(jnp.where(lg == m1, lane, float(LANES)), axis=-1, keepdims=True)
    lg2 = jnp.where(lane == i1, neg, lg)
    m2 = jnp.max(lg2, axis=-1, keepdims=True)
    i2 = jnp.min(jnp.where(lg2 == m2, lane, float(LANES)), axis=-1, keepdims=True)
    e2 = jnp.exp(m2 - m1)
    w1 = 1.0 / (1.0 + e2)
    w2 = e2 * w1
    sel1 = lane == i1
    sel2 = lane == i2
    onehot = jnp.where(jnp.logical_or(sel1, sel2), 1.0, 0.0)
    rr = lax.broadcasted_iota(jnp.int32, (tm, tm), 0)
    cc = lax.broadcasted_iota(jnp.int32, (tm, tm), 1)
    tri = jnp.where(rr > cc, 1.0, 0.0).astype(BF16)
    cum = jnp.dot(tri, onehot.astype(BF16), preferred_element_type=F32) + carry[...]
    r1 = jnp.sum(jnp.where(sel1, cum, 0.0), axis=-1, keepdims=True)
    r2 = jnp.sum(jnp.where(sel2, cum, 0.0), axis=-1, keepdims=True)
    carry[...] += jnp.sum(onehot, axis=0, keepdims=True)
    meta = jnp.where(lane == 0, i1, jnp.where(lane == 1, i2, jnp.where(lane == 2, r1,
                     jnp.where(lane == 3, r2, 0.0))))
    meta_ref[...] = meta.astype(jnp.int32)
    gw_ref[...] = jnp.where(lane == 0, w1, jnp.where(lane == 1, w2, 0.0))
    cnt_ref[...] = jnp.broadcast_to(carry[...], cnt_ref.shape)


def _router(xs, mod_l, g, w_router_pad, *, rows, n_lat_rows, seq_len, ctx_row, tm, n_exp):
    d = xs.shape[1]
    midx = _mod_index(tm, n_lat_rows, seq_len, ctx_row)
    kern = functools.partial(_router_kernel, n_exp=n_exp)
    return pl.pallas_call(
        kern,
        out_shape=[jax.ShapeDtypeStruct((rows, d // 2), jnp.int32),
                   jax.ShapeDtypeStruct((rows, LANES), jnp.int32),
                   jax.ShapeDtypeStruct((rows, LANES), F32),
                   jax.ShapeDtypeStruct((8, LANES), F32)],
        grid=(rows // tm,),
        in_specs=[
            pl.BlockSpec((tm, d), lambda i: (i, 0)),
            pl.BlockSpec((None, 6, d), lambda i: (midx(i), 0, 0)),
            pl.BlockSpec((1, d), lambda i: (0, 0)),
            pl.BlockSpec((2, d, LANES), lambda i: (0, 0, 0)),
        ],
        out_specs=[pl.BlockSpec((tm, d // 2), lambda i: (i, 0)),
                   pl.BlockSpec((tm, LANES), lambda i: (i, 0)),
                   pl.BlockSpec((tm, LANES), lambda i: (i, 0)),
                   pl.BlockSpec((8, LANES), lambda i: (0, 0))],
        scratch_shapes=[pltpu.VMEM((1, LANES), F32)],
        compiler_params=_cparams(("arbitrary",), V7X_VMEM_LIMIT),
        name="router",
    )(xs, mod_l, g, w_router_pad)


def _dispatch_rows(src, dest, n_out):
    t, d = src.shape
    n_slots = dest.shape[0]
    assert t % SC_INDEX_WINDOW == 0
    n_win = t // SC_INDEX_WINDOW
    n_workers = SC_CORES * SC_SUBCORES
    mesh = plsc.VectorSubcoreMesh(core_axis_name="core", subcore_axis_name="subcore")

    @pl.kernel(out_type=jax.ShapeDtypeStruct((n_out, d), src.dtype), mesh=mesh,
               scratch_types=[pltpu.VMEM((n_slots, SC_INDEX_WINDOW), jnp.int32),
                              pltpu.VMEM((SC_GATHER_ROWS, d), src.dtype)])
    def dispatch(src_hbm, idx_hbm, out_hbm, idx_v, buf):
        wid = lax.axis_index("core") * SC_SUBCORES + lax.axis_index("subcore")

        @pl.loop(0, (n_win - wid + n_workers - 1) // n_workers)
        def _(b):
            base = (b * n_workers + wid) * SC_INDEX_WINDOW
            pltpu.sync_copy(idx_hbm.at[:, pl.ds(base, SC_INDEX_WINDOW)], idx_v)
            for k in range(SC_INDEX_WINDOW // SC_GATHER_ROWS):
                rows = pl.ds(k * SC_GATHER_ROWS, SC_GATHER_ROWS)
                pltpu.sync_copy(src_hbm.at[pl.ds(base + k * SC_GATHER_ROWS, SC_GATHER_ROWS)], buf)
                for s in range(n_slots):
                    pltpu.sync_copy(buf, out_hbm.at[idx_v.at[s, rows]])

    return dispatch(src, dest)


def _gather_rows(src, idx):
    n = idx.shape[0]
    d = src.shape[1]
    n_workers = SC_CORES * SC_SUBCORES
    per = n // n_workers
    assert n % (n_workers * SC_INDEX_WINDOW) == 0, (n, n_workers, SC_INDEX_WINDOW)
    mesh = plsc.VectorSubcoreMesh(core_axis_name="core", subcore_axis_name="subcore")

    @pl.kernel(out_type=jax.ShapeDtypeStruct((n, d), src.dtype), mesh=mesh,
               scratch_types=[pltpu.VMEM((1, SC_INDEX_WINDOW), jnp.int32),
                              pltpu.VMEM((SC_GATHER_ROWS, d), src.dtype)])
    def gather(src_hbm, idx_hbm, out_hbm, idx_v, buf):
        wid = lax.axis_index("core") * SC_SUBCORES + lax.axis_index("subcore")

        @pl.loop(0, per // SC_INDEX_WINDOW)
        def _(b):
            base = wid * per + b * SC_INDEX_WINDOW
            pltpu.sync_copy(idx_hbm.at[:, pl.ds(base, SC_INDEX_WINDOW)], idx_v)
            for k in range(SC_INDEX_WINDOW // SC_GATHER_ROWS):
                rows = pl.ds(k * SC_GATHER_ROWS, SC_GATHER_ROWS)
                pltpu.sync_copy(src_hbm.at[idx_v.at[0, rows]], buf)
                pltpu.sync_copy(buf, out_hbm.at[pl.ds(base + k * SC_GATHER_ROWS, SC_GATHER_ROWS)])

    return gather(src, idx.reshape(1, n))


def _combine_kernel(x_ref, y1_ref, y2_ref, gw_ref, mod_ref, *rest, final):
    if final:
        fg_ref, o_ref = rest
    else:
        (o_ref,) = rest
    gw = gw_ref[...]
    y = gw[:, 0:1] * y1_ref[...] + gw[:, 1:2] * y2_ref[...]
    xn = x_ref[...] + mod_ref[5:6, :] * y
    if final:
        ms = jnp.mean(xn * xn, axis=-1, keepdims=True)
        xn = xn * lax.rsqrt(ms + NORM_EPS) * fg_ref[...]
    o_ref[...] = xn


def _combine(xs, ypair, gw, mod_l, *, rows, n_lat_rows, seq_len, ctx_row, tm, final_g=None):
    d = xs.shape[1]
    n_tiles = rows // tm
    midx = _mod_index(tm, n_lat_rows, seq_len, ctx_row)
    final = final_g is not None
    in_specs = [
        pl.BlockSpec((tm, d), lambda i: (i, 0)),
        pl.BlockSpec((tm, d), lambda i: (i, 0)),
        pl.BlockSpec((tm, d), lambda i: (i + n_tiles, 0)),
        pl.BlockSpec((tm, LANES), lambda i: (i, 0)),
        pl.BlockSpec((None, 6, d), lambda i: (midx(i), 0, 0)),
    ]
    args = [xs, ypair, ypair, gw, mod_l]
    if final:
        in_specs.append(pl.BlockSpec((1, d), lambda i: (0, 0)))
        args.append(final_g)
    out_rows = rows if final else xs.shape[0]
    return pl.pallas_call(
        functools.partial(_combine_kernel, final=final),
        out_shape=jax.ShapeDtypeStruct((out_rows, d), F32),
        grid=(n_tiles,),
        in_specs=in_specs,
        out_specs=pl.BlockSpec((tm, d), lambda i: (i, 0)),
        input_output_aliases=({} if final else {0: 0}),
        compiler_params=_cparams(("arbitrary",), V7X_VMEM_LIMIT),
        name="moe_combine",
    )(*args)


def _fnet_a_kernel(x_ref, mod_ref, g_ref, cs_ref, yc_ref, ys_ref, *, n_groups):
    h = _norm_mod(x_ref[...], g_ref[...], mod_ref[1:2, :], mod_ref[0:1, :]).astype(BF16)
    gd = cs_ref.shape[0]
    for gi in range(n_groups):
        y = jnp.dot(h[:, gi * gd:(gi + 1) * gd], cs_ref[...], preferred_element_type=F32)
        yc_ref[:, gi * gd:(gi + 1) * gd] = y[:, :gd].astype(yc_ref.dtype)
        ys_ref[:, gi * gd:(gi + 1) * gd] = y[:, gd:].astype(ys_ref.dtype)


def _fnet_a(xs, mod_l, g, cs, *, rows, n_lat_rows, seq_len, ctx_row, tm):
    d = xs.shape[1]
    gd = cs.shape[0]
    midx = _mod_index(tm, n_lat_rows, seq_len, ctx_row)
    return pl.pallas_call(
        functools.partial(_fnet_a_kernel, n_groups=d // gd),
        out_shape=[jax.ShapeDtypeStruct((rows, d), BF16)] * 2,
        grid=(rows // tm,),
        in_specs=[
            pl.BlockSpec((tm, d), lambda i: (i, 0)),
            pl.BlockSpec((None, 6, d), lambda i: (midx(i), 0, 0)),
            pl.BlockSpec((1, d), lambda i: (0, 0)),
            pl.BlockSpec((gd, 2 * gd), lambda i: (0, 0)),
        ],
        out_specs=[pl.BlockSpec((tm, d), lambda i: (i, 0))] * 2,
        compiler_params=_cparams(("arbitrary",), V7X_VMEM_LIMIT),
        name="fnet_group_dft",
    )(xs, mod_l, g, cs)


def _dft_table_kernel(ac_ref, as_ref, bc_ref, bs_ref, c_ref, sn_ref):
    ac, as_ = ac_ref[...], as_ref[...]
    bc, bs = bc_ref[...], bs_ref[...]
    c_ref[...] = (bc * ac - bs * as_).astype(c_ref.dtype)
    sn_ref[...] = (-(bs * ac + bc * as_)).astype(sn_ref.dtype)


def _dft_tables(n):
    r = DFT_ROWS
    k = jnp.arange(n, dtype=jnp.int32)[None, :]
    j1 = jnp.arange(r, dtype=jnp.int32)[:, None]
    j0 = (jnp.arange(n // r, dtype=jnp.int32) * r)[:, None]
    ang1 = ((j1 * k) % n).astype(F32) * (2.0 * math.pi / n)
    ang0 = ((j0 * k) % n).astype(F32) * (2.0 * math.pi / n)
    scale = 1.0 / math.sqrt(n)
    ac, as_ = jnp.cos(ang1), jnp.sin(ang1)
    bc = (jnp.cos(ang0) * scale).reshape(n // r, 1, n)
    bs = (jnp.sin(ang0) * scale).reshape(n // r, 1, n)
    return pl.pallas_call(
        _dft_table_kernel,
        out_shape=[jax.ShapeDtypeStruct((n, n), BF16)] * 2,
        grid=(n // r,),
        in_specs=[
            pl.BlockSpec((r, n), lambda i: (0, 0)),
            pl.BlockSpec((r, n), lambda i: (0, 0)),
            pl.BlockSpec((None, 1, n), lambda i: (i, 0, 0)),
            pl.BlockSpec((None, 1, n), lambda i: (i, 0, 0)),
        ],
        out_specs=[pl.BlockSpec((r, n), lambda i: (i, 0))] * 2,
        compiler_params=_cparams(("arbitrary",)),
        name="dft_tables",
    )(ac, as_, bc, bs)


def _fnet_b_kernel(c_ref, sn_ref, yc_ref, ys_ref, x_ref, mod_ref, wf_ref, bf_ref, o_ref):
    z = (jnp.dot(c_ref[...], yc_ref[...], preferred_element_type=F32)
         + jnp.dot(sn_ref[...], ys_ref[...], preferred_element_type=F32))
    out = jnp.dot(z.astype(BF16), wf_ref[...], preferred_element_type=F32) + bf_ref[...]
    o_ref[...] = x_ref[...] + mod_ref[2:3, :] * out


def _fnet_b(ctab, stab, yc, ys, xs, mod_l, wf, bf, *, n_batch, seq_len, row_off, mod_ctx_row, tm):
    d = xs.shape[1]
    n_i = seq_len // tm
    off_m = row_off // tm
    off_b = row_off // seq_len
    resident = pl.Buffered(1)

    def mrow(b):
        return b if mod_ctx_row is None else mod_ctx_row

    return pl.pallas_call(
        _fnet_b_kernel,
        out_shape=jax.ShapeDtypeStruct(xs.shape, F32),
        grid=(n_batch, n_i),
        in_specs=[
            pl.BlockSpec((tm, seq_len), lambda b, i: (i, 0)),
            pl.BlockSpec((tm, seq_len), lambda b, i: (i, 0)),
            pl.BlockSpec((seq_len, d), lambda b, i: (off_b + b, 0), pipeline_mode=resident),
            pl.BlockSpec((seq_len, d), lambda b, i: (off_b + b, 0), pipeline_mode=resident),
            pl.BlockSpec((tm, d), lambda b, i: (off_m + b * n_i + i, 0)),
            pl.BlockSpec((None, 6, d), lambda b, i: (mrow(b), 0, 0)),
            pl.BlockSpec((d, d), lambda b, i: (0, 0)),
            pl.BlockSpec((1, d), lambda b, i: (0, 0)),
        ],
        out_specs=pl.BlockSpec((tm, d), lambda b, i: (off_m + b * n_i + i, 0)),
        input_output_aliases={4: 0},
        compiler_params=_cparams(("arbitrary", "arbitrary"), V7X_VMEM_LIMIT),
        name="fnet_seq_dft",
    )(ctab, stab, yc, ys, xs, mod_l, wf, bf)


def _qkv_kernel(x_ref, mod_ref, g_ref, w_ref, cos_ref, sin_ref, q_ref, k_ref, v_ref, *,
                n_lat_tiles, q_scale):
    tm, d = x_ref.shape
    i = pl.program_id(0)
    is_ctx = i >= n_lat_tiles
    h = _norm_mod(x_ref[...], g_ref[...], mod_ref[1:2, :], mod_ref[0:1, :])
    y = jnp.dot(h.astype(BF16), w_ref[...], preferred_element_type=F32)
    tw = cos_ref.shape[1]
    cos = jnp.where(is_ctx, 1.0, cos_ref[...])
    sin = jnp.where(is_ctx, 0.0, sin_ref[...])
    lane = lax.broadcasted_iota(jnp.int32, (tm, tw), 1)
    half = tw // 8
    lo = (lane % (2 * half)) < half

    def rope(t):
        rot = jnp.where(lo, -pltpu.roll(t, tw - half, 1), pltpu.roll(t, half, 1))
        return t * cos + rot * sin

    for hd in range(d // tw):
        cols = slice(hd * tw, (hd + 1) * tw)
        q_ref[:, cols] = (rope(y[:, hd * tw:(hd + 1) * tw]) * q_scale).astype(q_ref.dtype)
        k_ref[:, cols] = rope(y[:, d + hd * tw:d + (hd + 1) * tw]).astype(k_ref.dtype)
    v_ref[...] = y[:, 2 * d:].astype(v_ref.dtype)


def _qkv(xs, mod_l, g, w_qkv, cos_t, sin_t, *, rows, n_lat_rows, seq_len, ctx_row, tm, q_scale):
    d = xs.shape[1]
    midx = _mod_index(tm, n_lat_rows, seq_len, ctx_row)
    per_seq = seq_len // tm
    n_lat_tiles = n_lat_rows // tm
    tw = cos_t.shape[1]

    def pos(i):
        return jnp.where(i < n_lat_tiles, i % per_seq, 0)

    kern = functools.partial(_qkv_kernel, n_lat_tiles=n_lat_tiles, q_scale=q_scale)
    return pl.pallas_call(
        kern,
        out_shape=[jax.ShapeDtypeStruct((rows, d), BF16)] * 3,
        grid=(rows // tm,),
        in_specs=[
            pl.BlockSpec((tm, d), lambda i: (i, 0)),
            pl.BlockSpec((None, 6, d), lambda i: (midx(i), 0, 0)),
            pl.BlockSpec((1, d), lambda i: (0, 0)),
            pl.BlockSpec((d, 3 * d), lambda i: (0, 0)),
            pl.BlockSpec((tm, tw), lambda i: (pos(i), 0)),
            pl.BlockSpec((tm, tw), lambda i: (pos(i), 0)),
        ],
        out_specs=[pl.BlockSpec((tm, d), lambda i: (i, 0))] * 3,
        compiler_params=_cparams(("arbitrary",), V7X_VMEM_LIMIT),
        name="attn_qkv",
    )(xs, mod_l, g, w_qkv, cos_t, sin_t)


def _dot_nt(a, b):
    return lax.dot_general(a, b, (((1,), (1,)), ((), ())), preferred_element_type=F32)


def _flash_kernel(lam_ref, sg_ref, q_ref, kc_ref, vc_ref, *rest, tk, n_kx, lam_init):
    if n_kx:
        kx_ref, vx_ref, o_ref, s_even, s_odd = rest
    else:
        (o_ref,) = rest
    tq, dh = q_ref.shape
    q = q_ref[...]
    lane = lax.broadcasted_iota(jnp.int32, (tq, dh), 1)
    zero = jnp.zeros_like(q)
    qa = jnp.where(lane < dh // 2, q, zero)
    qb = jnp.where(lane >= dh // 2, q, zero)

    def v_ext(v):
        return jnp.concatenate([v, jnp.ones_like(v)], axis=1)

    def init(k, v):
        ve = v_ext(v)
        out = []
        for qm in (qa, qb):
            s = _dot_nt(qm, k)
            m = jnp.max(s, axis=-1, keepdims=True)
            p = jnp.exp2(s - m).astype(BF16)
            out += [m, jnp.dot(p, ve, preferred_element_type=F32)]
        return tuple(out)

    def scores(t, s_ref):
        k = kx_ref[t * tk:(t + 1) * tk, :]
        s_ref[0] = _dot_nt(qa, k)
        s_ref[1] = _dot_nt(qb, k)

    def update(t, s_ref, carry):
        ve = v_ext(vx_ref[t * tk:(t + 1) * tk, :])
        out = []
        for mp, (m, acc) in enumerate((carry[0:2], carry[2:4])):
            s = s_ref[mp]
            m_new = jnp.maximum(m, jnp.max(s, axis=-1, keepdims=True))
            p = jnp.exp2(s - m_new).astype(BF16)
            acc = jnp.exp2(m - m_new) * acc + jnp.dot(p, ve, preferred_element_type=F32)
            out += [m_new, acc]
        return tuple(out)

    if n_kx:
        bufs = (s_even, s_odd)
        scores(0, bufs[0])
    carry = init(kc_ref[...], vc_ref[...])
    if n_kx:
        for t in range(n_kx):
            if t + 1 < n_kx:
                scores(t + 1, bufs[(t + 1) % 2])
            carry = update(t, bufs[t % 2], carry)
    _, a1, _, a2 = carry
    lv = lam_ref[...]
    lam = (jnp.exp(jnp.sum(lv[0:1, :] * lv[1:2, :], axis=-1, keepdims=True))
           - jnp.exp(jnp.sum(lv[2:3, :] * lv[3:4, :], axis=-1, keepdims=True)) + lam_init)
    o = a1[:, :dh] / a1[:, dh:] - lam * (a2[:, :dh] / a2[:, dh:])
    ms = jnp.mean(o * o, axis=-1, keepdims=True)
    o = o * lax.rsqrt(ms + NORM_EPS) * sg_ref[...] * (1.0 - lam_init)
    o_ref[...] = o.astype(o_ref.dtype)


def _flash(q, k, v, lam_vecs, subln_g, out_init, *, n_batch, n_heads, q_len, q_off, kc_len, kc_off,
           kx_len, tq, tk, lam_init):
    dh = q.shape[1] // n_heads
    n_q = q_len // tq
    qo = q_off // tq
    kco = kc_off // kc_len
    n_kx = kx_len // tk if kx_len else 0
    in_specs = [
        pl.BlockSpec(memory_space=pl.ANY),
        pl.BlockSpec(lam_vecs.shape, lambda b, h, i: (0, 0)),
        pl.BlockSpec((1, dh), lambda b, h, i: (0, 0)),
        pl.BlockSpec((tq, dh), lambda b, h, i: (qo + b * n_q + i, h)),
        pl.BlockSpec((kc_len, dh), lambda b, h, i: (kco + b, h)),
        pl.BlockSpec((kc_len, dh), lambda b, h, i: (kco + b, h)),
    ]
    args = [out_init, lam_vecs, subln_g, q, k, v]
    if n_kx:
        in_specs += [pl.BlockSpec((kx_len, dh), lambda b, h, i: (b, h)),
                     pl.BlockSpec((kx_len, dh), lambda b, h, i: (b, h))]
        args += [k, v]
    kern = functools.partial(_flash_kernel_alias, tk=tk, n_kx=n_kx, lam_init=lam_init)
    return pl.pallas_call(
        kern,
        out_shape=jax.ShapeDtypeStruct(out_init.shape, out_init.dtype),
        grid=(n_batch, n_heads, n_q),
        in_specs=in_specs,
        out_specs=pl.BlockSpec((tq, dh), lambda b, h, i: (qo + b * n_q + i, h)),
        scratch_shapes=([pltpu.VMEM((2, tq, tk), F32)] * 2 if n_kx else []),
        input_output_aliases={0: 0},
        compiler_params=_cparams(("arbitrary", "arbitrary", "arbitrary"), V7X_VMEM_LIMIT),
        name="diff_attn_latent" if n_kx else "diff_attn_ctx",
    )(*args)


def _flash_kernel_alias(out_init_ref, *refs, tk, n_kx, lam_init):
    del out_init_ref
    _flash_kernel(*refs, tk=tk, n_kx=n_kx, lam_init=lam_init)


def _proj_residual_kernel(a_ref, w_ref, x_ref, mod_ref, o_ref):
    out = jnp.dot(a_ref[...], w_ref[...], preferred_element_type=F32)
    o_ref[...] = x_ref[...] + mod_ref[2:3, :] * out


def _proj_residual(a, w, xs, mod_l, *, rows, n_lat_rows, seq_len, ctx_row, tm):
    d = xs.shape[1]
    midx = _mod_index(tm, n_lat_rows, seq_len, ctx_row)
    return pl.pallas_call(
        _proj_residual_kernel,
        out_shape=jax.ShapeDtypeStruct(xs.shape, F32),
        grid=(rows // tm,),
        in_specs=[
            pl.BlockSpec((tm, d), lambda i: (i, 0)),
            pl.BlockSpec((d, d), lambda i: (0, 0)),
            pl.BlockSpec((tm, d), lambda i: (i, 0)),
            pl.BlockSpec((None, 6, d), lambda i: (midx(i), 0, 0)),
        ],
        out_specs=pl.BlockSpec((tm, d), lambda i: (i, 0)),
        input_output_aliases={2: 0},
        compiler_params=_cparams(("arbitrary",), V7X_VMEM_LIMIT),
        name="attn_out_proj",
    )(a, w, xs, mod_l)


def _rope_tables(n_tokens, width):
    rope_axis_dim = 32
    freqs = rope_axis_dim // 2
    rows = n_tokens // GRID_W
    row = jnp.repeat(jnp.arange(rows, dtype=F32), GRID_W)
    col = jnp.tile(jnp.arange(GRID_W, dtype=F32), rows)
    inv_freq = 1.0 / (ROPE_BASE ** (jnp.arange(freqs, dtype=F32) * 2.0 / rope_axis_dim))
    ang = jnp.stack([row[:, None] * inv_freq, col[:, None] * inv_freq], axis=1)
    ang = jnp.stack([ang, ang], axis=2).reshape(n_tokens, 4 * freqs)
    ang = jnp.tile(ang, (1, width // (4 * freqs)))
    return jnp.cos(ang), jnp.sin(ang)


def _dense_ffn_layer(xs, mod_l, g, wg, wu, wd, geo, rows):
    tm = geo["tm_ffn"]
    n_tiles = rows // tm
    te = jnp.zeros((n_tiles,), jnp.int32)
    tr = jnp.full((n_tiles,), tm, jnp.int32)
    midx = _mod_index(tm, geo["n_lat_rows"], geo["seq_len"], geo["ctx_row"])
    return _ffn(xs, te, tr, wg, wu, wd, rows=rows, tm=tm, tf=geo["tf"], mod_l=mod_l, g=g, midx=midx)


def _moe_layer(xs, mod_l, g, w_router, wg, wu, wd, geo, rows, final_g=None):
    d = xs.shape[1]
    n_exp = w_router.shape[1]
    tm = geo["tm_moe"]
    common = dict(rows=rows, n_lat_rows=geo["n_lat_rows"], seq_len=geo["seq_len"], ctx_row=geo["ctx_row"])
    wr = jnp.zeros((d, LANES), F32).at[:, :n_exp].set(w_router)
    wr_hi = wr.astype(BF16)
    wr = jnp.stack([wr_hi, (wr - wr_hi.astype(F32)).astype(BF16)])
    h, meta, gw, cnt = _router(xs, mod_l, g, wr, tm=geo["tm_router"], n_exp=n_exp, **common)
    counts = cnt[0, :n_exp].astype(jnp.int32)
    padded = ((counts + tm - 1) // tm) * tm
    ends = jnp.cumsum(padded)
    offs = ends - padded
    n_sorted_tiles = (TOP_K * rows) // tm + n_exp
    tile_start = jnp.arange(n_sorted_tiles, dtype=jnp.int32) * tm
    tile_expert = jnp.minimum(jnp.sum(tile_start[:, None] >= ends[None, :], axis=1), n_exp - 1).astype(jnp.int32)
    tile_rows = jnp.clip((offs + counts)[tile_expert] - tile_start, 0, tm).astype(jnp.int32)
    tile_rows = jnp.where(tile_start < ends[-1], tile_rows, 0)
    e1, e2, r1, r2 = meta[:, 0], meta[:, 1], meta[:, 2], meta[:, 3]
    dest = jnp.stack([offs[e1] + r1, offs[e2] + r2]).astype(jnp.int32)
    hs = _dispatch_rows(h, dest, n_sorted_tiles * tm)
    ys = _ffn(hs, tile_expert, tile_rows, wg, wu, wd, rows=n_sorted_tiles * tm, tm=tm, tf=geo["tf"])
    ypair = _gather_rows(ys, dest.reshape(TOP_K * rows))
    return _combine(xs, ypair, gw, mod_l, tm=geo["tm_row"], final_g=final_g, **common)


def kernel(x, c, ctx, c_ctx, w_mod, b_mod, norm_g, conv_w_in, conv_b_in, conv_w_dw, conv_b_dw,
           conv_ln_g, conv_ln_b, conv_w_out, conv_b_out, fnet_w, fnet_b, attn_w_qkv, attn_lambda,
           attn_subln_g, attn_w_o, ffn_w_gate, ffn_w_up, ffn_w_down, moe_w_router, moe_w_gate,
           moe_w_up, moe_w_down, final_g):
    b_, n, d = x.shape
    n_ctx = ctx.shape[1]
    depth = w_mod.shape[0]
    n_lat = b_ * n
    n_all = n_lat + b_ * n_ctx
    assert b_ < MOD_ROWS and d % LANES == 0 and n % GRID_W == 0 and n % n_ctx == 0
    assert d // DA_HEADS == LANES
    geo = dict(n_lat_rows=n_lat, seq_len=n, ctx_row=b_,
               tm_row=min(1024, n_ctx * b_, n), tm_dft=min(512, n), tm_ffn=min(1024, n_ctx * b_, n), tm_moe=512,
               tm_router=min(512, n_ctx * b_, n), tf=min(512, ffn_w_gate.shape[2]),
               tm_conv=n_ctx)

    xs = jnp.concatenate([x.reshape(n_lat, d), ctx.reshape(b_ * n_ctx, d)], axis=0)
    c_all = jnp.zeros((MOD_ROWS, d), F32).at[:b_].set(c).at[b_].set(c_ctx)
    mod = _modulation(c_all, w_mod, b_mod).reshape(depth, MOD_ROWS, 6, d)

    out = None
    for i in range(depth):
        need_ctx = i < depth - 1
        rows = n_all if need_ctx else n_lat
        kind = i % 3
        mod_l = mod[i]
        common = dict(rows=rows, n_lat_rows=n_lat, seq_len=n, ctx_row=b_)
        g0 = norm_g[i, 0].reshape(1, d)
        g1 = norm_g[i, 1].reshape(1, d)
        j = i // 3
        if kind == 0:
            u = _conv_in(xs, mod_l, g0, conv_w_in[j].astype(BF16), conv_b_in[j].reshape(1, 2 * d),
                         tm=geo["tm_row"], **common)
            xs = _conv_out(u, xs, mod_l, conv_w_dw[j], conv_b_dw[j].reshape(1, d),
                           conv_ln_g[j].reshape(1, d), conv_ln_b[j].reshape(1, d),
                           conv_w_out[j].astype(BF16), conv_b_out[j].reshape(1, d),
                           tm=geo["tm_conv"], **common)
        elif kind == 1:
            gd = d // F_GROUPS
            kk = jnp.arange(gd, dtype=jnp.int32)
            ang = ((kk[:, None] * kk[None, :]) % gd).astype(F32) * (2.0 * math.pi / gd)
            cs = (jnp.concatenate([jnp.cos(ang), jnp.sin(ang)], axis=1) / math.sqrt(gd)).astype(BF16)
            yc, ys = _fnet_a(xs, mod_l, g0, cs, tm=geo["tm_row"], **common)
            wf = fnet_w[j].astype(BF16)
            bf = fnet_b[j].reshape(1, d)
            ct, st = _dft_tables(n)
            xs = _fnet_b(ct, st, yc, ys, xs, mod_l, wf, bf, n_batch=b_, seq_len=n, row_off=0,
                         mod_ctx_row=None, tm=geo["tm_dft"])
            if need_ctx:
                ct, st = _dft_tables(n_ctx)
                xs = _fnet_b(ct, st, yc, ys, xs, mod_l, wf, bf, n_batch=b_, seq_len=n_ctx,
                             row_off=n_lat, mod_ctx_row=b_, tm=n_ctx)
        else:
            lam_init = 0.8 - 0.6 * math.exp(-0.3 * i)
            dh = d // DA_HEADS
            cos_t, sin_t = _rope_tables(n, LANES)
            q, k, v = _qkv(xs, mod_l, g0, attn_w_qkv[j].astype(BF16), cos_t, sin_t,
                           tm=geo["tm_row"], q_scale=(dh // 2) ** -0.5 * math.log2(math.e), **common)
            sg = attn_subln_g[j].reshape(1, dh)
            o = jnp.zeros((rows, d), BF16)
            fl = dict(n_batch=b_, n_heads=DA_HEADS, kc_len=n_ctx, kc_off=n_lat, lam_init=lam_init)
            o = _flash(q, k, v, attn_lambda[j], sg, o, q_len=n, q_off=0, kx_len=n,
                       tq=min(512, n), tk=min(1024, n), **fl)
            if need_ctx:
                o = _flash(q, k, v, attn_lambda[j], sg, o, q_len=n_ctx, q_off=n_lat, kx_len=0,
                           tq=n_ctx, tk=n_ctx, **fl)
            xs = _proj_residual(o, attn_w_o[j].astype(BF16), xs, mod_l, tm=geo["tm_row"], **common)

        j = i // 2
        if i % 2 == 0:
            wg, wu, wd = (_cast_layer_bf16(w[:, None], j) for w in (ffn_w_gate, ffn_w_up, ffn_w_down))
            xs = _dense_ffn_layer(xs, mod_l, g1, wg, wu, wd, geo, rows)
        else:
            fg = final_g.reshape(1, d) if i == depth - 1 else None
            xs, w32 = lax.optimization_barrier((xs, (moe_w_gate, moe_w_up, moe_w_down)))
            wg, wu, wd = (_cast_layer_bf16(w, j) for w in w32)
            xs, wg, wu, wd = lax.optimization_barrier((xs, wg, wu, wd))
            res = _moe_layer(xs, mod_l, g1, moe_w_router[j], wg, wu, wd, geo, rows, final_g=fg)
            if fg is not None:
                out = res
            else:
                xs = res
    if out is None:
        ms = jnp.mean(jnp.square(xs[:n_lat]), axis=-1, keepdims=True)
        out = xs[:n_lat] * lax.rsqrt(ms + NORM_EPS) * final_g
    return out.reshape(b_, n, d)
```

```python
import functools
import math

import jax
import jax.numpy as jnp
from jax import lax
from jax.experimental import pallas as pl
from jax.experimental.pallas import tpu as pltpu
from jax.experimental.pallas import tpu_sc as plsc

F32 = jnp.float32
BF16 = jnp.bfloat16

NORM_EPS = 1e-6
LN_EPS = 1e-5
CONV_WIDTH = 31
CONV_PAD = (CONV_WIDTH - 1) // 2
CONV_HALO = 16
GRID_W = 64
F_GROUPS = 8
DA_HEADS = 8
ROPE_BASE = 10000.0
TOP_K = 2
LANES = 128
SUBLANES = 8
MOD_ROWS = 16
DFT_ROWS = 64
SC_CORES = 2
SC_SUBCORES = 16
SC_INDEX_WINDOW = 128
SC_GATHER_ROWS = 64

V7X_VMEM_LIMIT = 56 * 1024 * 1024


def _cparams(sem, vmem=None):
    return pltpu.CompilerParams(dimension_semantics=sem, vmem_limit_bytes=vmem)


def _norm_mod(x, g, scale, shift):
    ms = jnp.mean(x * x, axis=-1, keepdims=True)
    y = x * lax.rsqrt(ms + NORM_EPS)
    return (y * g) * (1.0 + scale) + shift


def _pack_bf16_pairs(h):
    half = h.shape[1] // 2
    hb = h.astype(BF16)
    hi = lax.bitcast_convert_type(hb[:, :half].astype(F32), jnp.uint32)
    lo = lax.bitcast_convert_type(hb[:, half:].astype(F32), jnp.uint32)
    return lax.bitcast_convert_type(hi | (lo >> 16), jnp.int32)


def _unpack_bf16_pairs(words):
    w = lax.bitcast_convert_type(words, jnp.uint32)
    hi = lax.bitcast_convert_type(w & jnp.uint32(0xFFFF0000), F32)
    lo = lax.bitcast_convert_type(w << 16, F32)
    return jnp.concatenate([hi, lo], axis=1).astype(BF16)


def _mod_index(tm, n_lat_rows, seq_len, ctx_row):
    n_lat_tiles = n_lat_rows // tm
    per_seq = seq_len // tm

    def f(i):
        return jnp.where(i < n_lat_tiles, i // per_seq, ctx_row)

    return f


def _mod_kernel(c_ref, w_ref, b_ref, o_ref):
    c = c_ref[...]
    sc = c * jax.nn.sigmoid(c)
    o_ref[...] = jnp.dot(sc, w_ref[...], precision=lax.Precision.HIGHEST,
                         preferred_element_type=F32) + b_ref[...]


def _modulation(c_all, w_mod, b_mod):
    depth, d, nd = w_mod.shape
    tn = 1024
    return pl.pallas_call(
        _mod_kernel,
        out_shape=jax.ShapeDtypeStruct((depth, MOD_ROWS, nd), F32),
        grid=(depth, nd // tn),
        in_specs=[
            pl.BlockSpec((MOD_ROWS, d), lambda l, j: (0, 0)),
            pl.BlockSpec((None, d, tn), lambda l, j: (l, 0, j)),
            pl.BlockSpec((None, 1, tn), lambda l, j: (l, 0, j)),
        ],
        out_specs=pl.BlockSpec((None, MOD_ROWS, tn), lambda l, j: (l, 0, j)),
        compiler_params=_cparams(("arbitrary", "arbitrary")),
        name="modulation",
    )(c_all, w_mod, b_mod.reshape(depth, 1, nd))


def _conv_in_kernel(x_ref, mod_ref, g_ref, w_ref, b_ref, u_ref):
    d = u_ref.shape[1]
    h = _norm_mod(x_ref[...], g_ref[...], mod_ref[1:2, :], mod_ref[0:1, :])
    y = jnp.dot(h.astype(BF16), w_ref[...], preferred_element_type=F32) + b_ref[...]
    u_ref[...] = (y[:, :d] * jax.nn.sigmoid(y[:, d:])).astype(u_ref.dtype)


def _conv_in(xs, mod_l, g, w_in, b_in, *, rows, n_lat_rows, seq_len, ctx_row, tm):
    d = xs.shape[1]
    midx = _mod_index(tm, n_lat_rows, seq_len, ctx_row)
    return pl.pallas_call(
        _conv_in_kernel,
        out_shape=jax.ShapeDtypeStruct((rows, d), BF16),
        grid=(rows // tm,),
        in_specs=[
            pl.BlockSpec((tm, d), lambda i: (i, 0)),
            pl.BlockSpec((None, 6, d), lambda i: (midx(i), 0, 0)),
            pl.BlockSpec((1, d), lambda i: (0, 0)),
            pl.BlockSpec((d, 2 * d), lambda i: (0, 0)),
            pl.BlockSpec((1, 2 * d), lambda i: (0, 0)),
        ],
        out_specs=pl.BlockSpec((tm, d), lambda i: (i, 0)),
        compiler_params=_cparams(("arbitrary",), V7X_VMEM_LIMIT),
        name="conv_in",
    )(xs, mod_l, g, w_in, b_in)


def _conv_out_kernel(u_ref, up_ref, un_ref, x_ref, mod_ref, wdw_ref, bdw_ref, lng_ref, lnb_ref,
                     wo_ref, bo_ref, o_ref, ubuf, shifted, cbuf, *, n_lat_tiles, tiles_per_seq):
    tm, d = x_ref.shape
    i = pl.program_id(0)
    is_ctx = i >= n_lat_tiles
    j = i % tiles_per_seq
    first = jnp.logical_or(is_ctx, j == 0)
    last = jnp.logical_or(is_ctx, j == tiles_per_seq - 1)
    ubuf[0:CONV_HALO, :] = jnp.where(first, 0.0, up_ref[...].astype(F32))
    ubuf[CONV_HALO:CONV_HALO + tm, :] = u_ref[...].astype(F32)
    ubuf[CONV_HALO + tm:, :] = jnp.where(last, 0.0, un_ref[...].astype(F32))

    span = ubuf.shape[0] - SUBLANES
    for s in range(1, SUBLANES):
        shifted[s - 1] = ubuf[s:s + span, :]

    for c in range(d // LANES):
        cols = slice(c * LANES, (c + 1) * LANES)
        acc = jnp.zeros((tm, LANES), F32)
        for k in range(CONV_WIDTH):
            r0 = CONV_HALO - CONV_PAD + k
            s, a0 = r0 % SUBLANES, r0 - r0 % SUBLANES
            win = ubuf[a0:a0 + tm, cols] if s == 0 else shifted[s - 1, a0:a0 + tm, cols]
            acc = acc + wdw_ref[k:k + 1, cols] * win
        cbuf[:, cols] = acc + bdw_ref[:, cols]
    v = cbuf[...]
    mu = jnp.mean(v, axis=-1, keepdims=True)
    vc = v - mu
    var = jnp.mean(vc * vc, axis=-1, keepdims=True)
    y = vc * lax.rsqrt(var + LN_EPS) * lng_ref[...] + lnb_ref[...]
    y = y * jax.nn.sigmoid(y)
    out = jnp.dot(y.astype(BF16), wo_ref[...], preferred_element_type=F32) + bo_ref[...]
    o_ref[...] = x_ref[...] + mod_ref[2:3, :] * out


def _conv_out(u, xs, mod_l, w_dw, b_dw, ln_g, ln_b, w_out, b_out, *, rows, n_lat_rows, seq_len,
              ctx_row, tm):
    d = xs.shape[1]
    n_tiles = rows // tm
    n_lat_tiles = n_lat_rows // tm
    hpt = tm // CONV_HALO
    n_halo = u.shape[0] // CONV_HALO
    midx = _mod_index(tm, n_lat_rows, seq_len, ctx_row)
    kern = functools.partial(_conv_out_kernel, n_lat_tiles=n_lat_tiles, tiles_per_seq=seq_len // tm)
    return pl.pallas_call(
        kern,
        out_shape=jax.ShapeDtypeStruct(xs.shape, F32),
        grid=(n_tiles,),
        in_specs=[
            pl.BlockSpec((tm, d), lambda i: (i, 0)),
            pl.BlockSpec((CONV_HALO, d), lambda i: (jnp.maximum(i * hpt - 1, 0), 0)),
            pl.BlockSpec((CONV_HALO, d), lambda i: (jnp.minimum((i + 1) * hpt, n_halo - 1), 0)),
            pl.BlockSpec((tm, d), lambda i: (i, 0)),
            pl.BlockSpec((None, 6, d), lambda i: (midx(i), 0, 0)),
            pl.BlockSpec((CONV_WIDTH, d), lambda i: (0, 0)),
            pl.BlockSpec((1, d), lambda i: (0, 0)),
            pl.BlockSpec((1, d), lambda i: (0, 0)),
            pl.BlockSpec((1, d), lambda i: (0, 0)),
            pl.BlockSpec((d, d), lambda i: (0, 0)),
            pl.BlockSpec((1, d), lambda i: (0, 0)),
        ],
        out_specs=pl.BlockSpec((tm, d), lambda i: (i, 0)),
        scratch_shapes=[pltpu.VMEM((tm + 2 * CONV_HALO, d), F32),
                        pltpu.VMEM((SUBLANES - 1, tm + 2 * CONV_HALO - SUBLANES, d), F32),
                        pltpu.VMEM((tm, d), F32)],
        input_output_aliases={3: 0},
        compiler_params=_cparams(("arbitrary",), V7X_VMEM_LIMIT),
        name="conv_out",
    )(u, u, u, xs, mod_l, w_dw, b_dw, ln_g, ln_b, w_out, b_out)


def _cast_kernel(w_ref, o_ref):
    o_ref[...] = w_ref[...].astype(o_ref.dtype)


def _cast_layer_bf16(w, layer):
    _, n_e, a, b = w.shape
    ta = min(a, 512)
    return pl.pallas_call(
        _cast_kernel,
        out_shape=jax.ShapeDtypeStruct((n_e, a, b), BF16),
        grid=(n_e, a // ta),
        in_specs=[pl.BlockSpec((None, None, ta, b), lambda e, i: (layer, e, i, 0))],
        out_specs=pl.BlockSpec((None, ta, b), lambda e, i: (e, i, 0)),
        compiler_params=_cparams(("arbitrary", "arbitrary"), V7X_VMEM_LIMIT),
        name="cast_weights",
    )(w)


def _ffn_kernel(te_ref, rv_ref, *refs, fuse_norm, tf):
    if fuse_norm:
        x_ref, mod_ref, g_ref, wg_ref, wu_ref, wd_ref, o_ref = refs
    else:
        x_ref, wg_ref, wu_ref, wd_ref, o_ref = refs
    del te_ref
    n_rows = rv_ref[pl.program_id(0)]
    valid = n_rows > 0

    @pl.when(valid)
    def _():
        if fuse_norm:
            h = _norm_mod(x_ref[...], g_ref[...], mod_ref[4:5, :], mod_ref[3:4, :])
        else:
            row = lax.broadcasted_iota(jnp.int32, (x_ref.shape[0], 1), 0)
            h = _unpack_bf16_pairs(jnp.where(row < n_rows, x_ref[...], 0))
        h = h.astype(BF16)
        acc = None
        for c in range(wg_ref.shape[1] // tf):
            cols = slice(c * tf, (c + 1) * tf)
            gt = jnp.dot(h, wg_ref[:, cols], preferred_element_type=F32)
            up = jnp.dot(h, wu_ref[:, cols], preferred_element_type=F32)
            a = (gt * jax.nn.sigmoid(gt) * up).astype(BF16)
            part = jnp.dot(a, wd_ref[cols, :], preferred_element_type=F32)
            acc = part if acc is None else acc + part
        if fuse_norm:
            o_ref[...] = x_ref[...] + mod_ref[5:6, :] * acc
        else:
            o_ref[...] = acc

    @pl.when(jnp.logical_not(valid))
    def _():
        o_ref[...] = jnp.zeros_like(o_ref)


def _ffn(x, tile_expert, tile_rows, wg, wu, wd, *, rows, tm, tf, mod_l=None, g=None, midx=None):
    d = wg.shape[1]
    f_dim = wg.shape[2]
    fuse_norm = mod_l is not None
    resident = pl.Buffered(1)
    in_specs = [pl.BlockSpec((tm, x.shape[1]), lambda j, te, nv: (j, 0))]
    args = [x]
    if fuse_norm:
        in_specs += [pl.BlockSpec((None, 6, d), lambda j, te, nv: (midx(j), 0, 0)),
                     pl.BlockSpec((1, d), lambda j, te, nv: (0, 0))]
        args += [mod_l, g]
    in_specs += [
        pl.BlockSpec((None, d, f_dim), lambda j, te, nv: (te[j], 0, 0), pipeline_mode=resident),
        pl.BlockSpec((None, d, f_dim), lambda j, te, nv: (te[j], 0, 0), pipeline_mode=resident),
        pl.BlockSpec((None, f_dim, d), lambda j, te, nv: (te[j], 0, 0), pipeline_mode=resident),
    ]
    args += [wg, wu, wd]
    kern = functools.partial(_ffn_kernel, fuse_norm=fuse_norm, tf=tf)
    return pl.pallas_call(
        kern,
        out_shape=jax.ShapeDtypeStruct((x.shape[0], d), F32),
        grid_spec=pltpu.PrefetchScalarGridSpec(
            num_scalar_prefetch=2,
            grid=(rows // tm,),
            in_specs=in_specs,
            out_specs=pl.BlockSpec((tm, d), lambda j, te, nv: (j, 0)),
        ),
        input_output_aliases=({2: 0} if fuse_norm else {}),
        compiler_params=_cparams(("arbitrary",), V7X_VMEM_LIMIT),
        name="ffn_dense" if fuse_norm else "ffn_grouped",
    )(tile_expert, tile_rows, *args)


def _router_kernel(x_ref, mod_ref, g_ref, wr_ref, h_ref, meta_ref, gw_ref, cnt_ref, carry, *, n_exp):
    tm = x_ref.shape[0]
    i = pl.program_id(0)

    @pl.when(i == 0)
    def _():
        carry[...] = jnp.zeros_like(carry)

    h = _norm_mod(x_ref[...], g_ref[...], mod_ref[4:5, :], mod_ref[3:4, :])
    h_ref[...] = _pack_bf16_pairs(h)
    h_hi = h.astype(BF16)
    h_lo = (h - h_hi.astype(F32)).astype(BF16)
    logits = (jnp.dot(h_hi, wr_ref[0], preferred_element_type=F32)
              + (jnp.dot(h_hi, wr_ref[1], preferred_element_type=F32)
                 + jnp.dot(h_lo, wr_ref[0], preferred_element_type=F32)))
    lane = lax.broadcasted_iota(jnp.int32, (tm, LANES), 1).astype(F32)
    neg = jnp.float32(-jnp.inf)
    lg = jnp.where(lane < n_exp, logits, neg)
    m1 = jnp.max(lg, axis=-1, keepdims=True)
    i1 = jnp.min(jnp.where(lg == m1, lane, float(LANES)), axis=-1, keepdims=True)
    lg2 = jnp.where(lane == i1, neg, lg)
    m2 = jnp.max(lg2, axis=-1, keepdims=True)
    i2 = jnp.min(jnp.where(lg2 == m2, lane, float(LANES)), axis=-1, keepdims=True)
    e2 = jnp.exp(m2 - m1)
    w1 = 1.0 / (1.0 + e2)
    w2 = e2 * w1
    sel1 = lane == i1
    sel2 = lane == i2
    onehot = jnp.where(jnp.logical_or(sel1, sel2), 1.0, 0.0)
    rr = lax.broadcasted_iota(jnp.int32, (tm, tm), 0)
    cc = lax.broadcasted_iota(jnp.int32, (tm, tm), 1)
    tri = jnp.where(rr > cc, 1.0, 0.0).astype(BF16)
    cum = jnp.dot(tri, onehot.astype(BF16), preferred_element_type=F32) + carry[...]
    r1 = jnp.sum(jnp.where(sel1, cum, 0.0), axis=-1, keepdims=True)
    r2 = jnp.sum(jnp.where(sel2, cum, 0.0), axis=-1, keepdims=True)
    carry[...] += jnp.sum(onehot, axis=0, keepdims=True)
    meta = jnp.where(lane == 0, i1, jnp.where(lane == 1, i2, jnp.where(lane == 2, r1,
                     jnp.where(lane == 3, r2, 0.0))))
    meta_ref[...] = meta.T[:SUBLANES, :].astype(jnp.int32)
    gw_ref[...] = jnp.where(lane == 0, w1, jnp.where(lane == 1, w2, 0.0))
    cnt_ref[...] = jnp.broadcast_to(carry[...], cnt_ref.shape)


def _router(xs, mod_l, g, w_router_pad, *, rows, n_lat_rows, seq_len, ctx_row, tm, n_exp):
    d = xs.shape[1]
    midx = _mod_index(tm, n_lat_rows, seq_len, ctx_row)
    kern = functools.partial(_router_kernel, n_exp=n_exp)
    return pl.pallas_call(
        kern,
        out_shape=[jax.ShapeDtypeStruct((rows, d // 2), jnp.int32),
                   jax.ShapeDtypeStruct((SUBLANES, rows), jnp.int32),
                   jax.ShapeDtypeStruct((rows, LANES), F32),
                   jax.ShapeDtypeStruct((8, LANES), F32)],
        grid=(rows // tm,),
        in_specs=[
            pl.BlockSpec((tm, d), lambda i: (i, 0)),
            pl.BlockSpec((None, 6, d), lambda i: (midx(i), 0, 0)),
            pl.BlockSpec((1, d), lambda i: (0, 0)),
            pl.BlockSpec((2, d, LANES), lambda i: (0, 0, 0)),
        ],
        out_specs=[pl.BlockSpec((tm, d // 2), lambda i: (i, 0)),
                   pl.BlockSpec((SUBLANES, tm), lambda i: (0, i)),
                   pl.BlockSpec((tm, LANES), lambda i: (i, 0)),
                   pl.BlockSpec((8, LANES), lambda i: (0, 0))],
        scratch_shapes=[pltpu.VMEM((1, LANES), F32)],
        compiler_params=_cparams(("arbitrary",), V7X_VMEM_LIMIT),
        name="router",
    )(xs, mod_l, g, w_router_pad)


def _dispatch_rows(src, dest, n_out):
    t, d = src.shape
    n_slots = dest.shape[0]
    assert t % SC_INDEX_WINDOW == 0
    n_win = t // SC_INDEX_WINDOW
    n_workers = SC_CORES * SC_SUBCORES
    mesh = plsc.VectorSubcoreMesh(core_axis_name="core", subcore_axis_name="subcore")

    @pl.kernel(out_type=jax.ShapeDtypeStruct((n_out, d), src.dtype), mesh=mesh,
               scratch_types=[pltpu.VMEM((n_slots, SC_INDEX_WINDOW), jnp.int32),
                              pltpu.VMEM((SC_GATHER_ROWS, d), src.dtype)])
    def dispatch(src_hbm, idx_hbm, out_hbm, idx_v, buf):
        wid = lax.axis_index("core") * SC_SUBCORES + lax.axis_index("subcore")

        @pl.loop(0, (n_win - wid + n_workers - 1) // n_workers)
        def _(b):
            base = (b * n_workers + wid) * SC_INDEX_WINDOW
            pltpu.sync_copy(idx_hbm.at[:, pl.ds(base, SC_INDEX_WINDOW)], idx_v)
            for k in range(SC_INDEX_WINDOW // SC_GATHER_ROWS):
                rows = pl.ds(k * SC_GATHER_ROWS, SC_GATHER_ROWS)
                pltpu.sync_copy(src_hbm.at[pl.ds(base + k * SC_GATHER_ROWS, SC_GATHER_ROWS)], buf)
                for s in range(n_slots):
                    pltpu.sync_copy(buf, out_hbm.at[idx_v.at[s, rows]])

    return dispatch(src, dest)


def _gather_rows(src, idx):
    n = idx.shape[0]
    d = src.shape[1]
    n_workers = SC_CORES * SC_SUBCORES
    per = n // n_workers
    assert n % (n_workers * SC_INDEX_WINDOW) == 0, (n, n_workers, SC_INDEX_WINDOW)
    mesh = plsc.VectorSubcoreMesh(core_axis_name="core", subcore_axis_name="subcore")

    @pl.kernel(out_type=jax.ShapeDtypeStruct((n, d), src.dtype), mesh=mesh,
               scratch_types=[pltpu.VMEM((1, SC_INDEX_WINDOW), jnp.int32),
                              pltpu.VMEM((SC_GATHER_ROWS, d), src.dtype)])
    def gather(src_hbm, idx_hbm, out_hbm, idx_v, buf):
        wid = lax.axis_index("core") * SC_SUBCORES + lax.axis_index("subcore")

        @pl.loop(0, per // SC_INDEX_WINDOW)
        def _(b):
            base = wid * per + b * SC_INDEX_WINDOW
            pltpu.sync_copy(idx_hbm.at[:, pl.ds(base, SC_INDEX_WINDOW)], idx_v)
            for k in range(SC_INDEX_WINDOW // SC_GATHER_ROWS):
                rows = pl.ds(k * SC_GATHER_ROWS, SC_GATHER_ROWS)
                pltpu.sync_copy(src_hbm.at[idx_v.at[0, rows]], buf)
                pltpu.sync_copy(buf, out_hbm.at[pl.ds(base + k * SC_GATHER_ROWS, SC_GATHER_ROWS)])

    return gather(src, idx.reshape(1, n))


def _combine_kernel(x_ref, y1_ref, y2_ref, gw_ref, mod_ref, *rest, final):
    if final:
        fg_ref, o_ref = rest
    else:
        (o_ref,) = rest
    gw = gw_ref[...]
    y = gw[:, 0:1] * y1_ref[...] + gw[:, 1:2] * y2_ref[...]
    xn = x_ref[...] + mod_ref[5:6, :] * y
    if final:
        ms = jnp.mean(xn * xn, axis=-1, keepdims=True)
        xn = xn * lax.rsqrt(ms + NORM_EPS) * fg_ref[...]
    o_ref[...] = xn


def _combine(xs, ypair, gw, mod_l, *, rows, n_lat_rows, seq_len, ctx_row, tm, final_g=None):
    d = xs.shape[1]
    n_tiles = rows // tm
    midx = _mod_index(tm, n_lat_rows, seq_len, ctx_row)
    final = final_g is not None
    in_specs = [
        pl.BlockSpec((tm, d), lambda i: (i, 0)),
        pl.BlockSpec((tm, d), lambda i: (i, 0)),
        pl.BlockSpec((tm, d), lambda i: (i + n_tiles, 0)),
        pl.BlockSpec((tm, LANES), lambda i: (i, 0)),
        pl.BlockSpec((None, 6, d), lambda i: (midx(i), 0, 0)),
    ]
    args = [xs, ypair, ypair, gw, mod_l]
    if final:
        in_specs.append(pl.BlockSpec((1, d), lambda i: (0, 0)))
        args.append(final_g)
    out_rows = rows if final else xs.shape[0]
    return pl.pallas_call(
        functools.partial(_combine_kernel, final=final),
        out_shape=jax.ShapeDtypeStruct((out_rows, d), F32),
        grid=(n_tiles,),
        in_specs=in_specs,
        out_specs=pl.BlockSpec((tm, d), lambda i: (i, 0)),
        input_output_aliases=({} if final else {0: 0}),
        compiler_params=_cparams(("arbitrary",), V7X_VMEM_LIMIT),
        name="moe_combine",
    )(*args)


def _fnet_a_kernel(x_ref, mod_ref, g_ref, cs_ref, yc_ref, ys_ref, *, n_groups):
    h = _norm_mod(x_ref[...], g_ref[...], mod_ref[1:2, :], mod_ref[0:1, :]).astype(BF16)
    gd = cs_ref.shape[0]
    for gi in range(n_groups):
        y = jnp.dot(h[:, gi * gd:(gi + 1) * gd], cs_ref[...], preferred_element_type=F32)
        yc_ref[:, gi * gd:(gi + 1) * gd] = y[:, :gd].astype(yc_ref.dtype)
        ys_ref[:, gi * gd:(gi + 1) * gd] = y[:, gd:].astype(ys_ref.dtype)


def _fnet_a(xs, mod_l, g, cs, *, rows, n_lat_rows, seq_len, ctx_row, tm):
    d = xs.shape[1]
    gd = cs.shape[0]
    midx = _mod_index(tm, n_lat_rows, seq_len, ctx_row)
    return pl.pallas_call(
        functools.partial(_fnet_a_kernel, n_groups=d // gd),
        out_shape=[jax.ShapeDtypeStruct((rows, d), BF16)] * 2,
        grid=(rows // tm,),
        in_specs=[
            pl.BlockSpec((tm, d), lambda i: (i, 0)),
            pl.BlockSpec((None, 6, d), lambda i: (midx(i), 0, 0)),
            pl.BlockSpec((1, d), lambda i: (0, 0)),
            pl.BlockSpec((gd, 2 * gd), lambda i: (0, 0)),
        ],
        out_specs=[pl.BlockSpec((tm, d), lambda i: (i, 0))] * 2,
        compiler_params=_cparams(("arbitrary",), V7X_VMEM_LIMIT),
        name="fnet_group_dft",
    )(xs, mod_l, g, cs)


def _dft_table_kernel(ac_ref, as_ref, bc_ref, bs_ref, c_ref, sn_ref):
    ac, as_ = ac_ref[...], as_ref[...]
    bc, bs = bc_ref[...], bs_ref[...]
    c_ref[...] = (bc * ac - bs * as_).astype(c_ref.dtype)
    sn_ref[...] = (-(bs * ac + bc * as_)).astype(sn_ref.dtype)


def _dft_tables(n):
    r = DFT_ROWS
    k = jnp.arange(n, dtype=jnp.int32)[None, :]
    j1 = jnp.arange(r, dtype=jnp.int32)[:, None]
    j0 = (jnp.arange(n // r, dtype=jnp.int32) * r)[:, None]
    ang1 = ((j1 * k) % n).astype(F32) * (2.0 * math.pi / n)
    ang0 = ((j0 * k) % n).astype(F32) * (2.0 * math.pi / n)
    scale = 1.0 / math.sqrt(n)
    ac, as_ = jnp.cos(ang1), jnp.sin(ang1)
    bc = (jnp.cos(ang0) * scale).reshape(n // r, 1, n)
    bs = (jnp.sin(ang0) * scale).reshape(n // r, 1, n)
    return pl.pallas_call(
        _dft_table_kernel,
        out_shape=[jax.ShapeDtypeStruct((n, n), BF16)] * 2,
        grid=(n // r,),
        in_specs=[
            pl.BlockSpec((r, n), lambda i: (0, 0)),
            pl.BlockSpec((r, n), lambda i: (0, 0)),
            pl.BlockSpec((None, 1, n), lambda i: (i, 0, 0)),
            pl.BlockSpec((None, 1, n), lambda i: (i, 0, 0)),
        ],
        out_specs=[pl.BlockSpec((r, n), lambda i: (i, 0))] * 2,
        compiler_params=_cparams(("arbitrary",)),
        name="dft_tables",
    )(ac, as_, bc, bs)


def _fnet_b_kernel(c_ref, sn_ref, yc_ref, ys_ref, x_ref, mod_ref, wf_ref, bf_ref, o_ref):
    z = (jnp.dot(c_ref[...], yc_ref[...], preferred_element_type=F32)
         + jnp.dot(sn_ref[...], ys_ref[...], preferred_element_type=F32))
    out = jnp.dot(z.astype(BF16), wf_ref[...], preferred_element_type=F32) + bf_ref[...]
    o_ref[...] = x_ref[...] + mod_ref[2:3, :] * out


def _fnet_b(ctab, stab, yc, ys, xs, mod_l, wf, bf, *, n_batch, seq_len, row_off, mod_ctx_row, tm):
    d = xs.shape[1]
    n_i = seq_len // tm
    off_m = row_off // tm
    off_b = row_off // seq_len
    resident = pl.Buffered(1)

    def mrow(b):
        return b if mod_ctx_row is None else mod_ctx_row

    return pl.pallas_call(
        _fnet_b_kernel,
        out_shape=jax.ShapeDtypeStruct(xs.shape, F32),
        grid=(n_batch, n_i),
        in_specs=[
            pl.BlockSpec((tm, seq_len), lambda b, i: (i, 0)),
            pl.BlockSpec((tm, seq_len), lambda b, i: (i, 0)),
            pl.BlockSpec((seq_len, d), lambda b, i: (off_b + b, 0), pipeline_mode=resident),
            pl.BlockSpec((seq_len, d), lambda b, i: (off_b + b, 0), pipeline_mode=resident),
            pl.BlockSpec((tm, d), lambda b, i: (off_m + b * n_i + i, 0)),
            pl.BlockSpec((None, 6, d), lambda b, i: (mrow(b), 0, 0)),
            pl.BlockSpec((d, d), lambda b, i: (0, 0)),
            pl.BlockSpec((1, d), lambda b, i: (0, 0)),
        ],
        out_specs=pl.BlockSpec((tm, d), lambda b, i: (off_m + b * n_i + i, 0)),
        input_output_aliases={4: 0},
        compiler_params=_cparams(("arbitrary", "arbitrary"), V7X_VMEM_LIMIT),
        name="fnet_seq_dft",
    )(ctab, stab, yc, ys, xs, mod_l, wf, bf)


def _qkv_kernel(x_ref, mod_ref, g_ref, w_ref, cos_ref, sin_ref, q_ref, k_ref, v_ref, *,
                n_lat_tiles, q_scale):
    tm, d = x_ref.shape
    i = pl.program_id(0)
    is_ctx = i >= n_lat_tiles
    h = _norm_mod(x_ref[...], g_ref[...], mod_ref[1:2, :], mod_ref[0:1, :])
    y = jnp.dot(h.astype(BF16), w_ref[...], preferred_element_type=F32)
    tw = cos_ref.shape[1]
    cos = jnp.where(is_ctx, 1.0, cos_ref[...])
    sin = jnp.where(is_ctx, 0.0, sin_ref[...])
    lane = lax.broadcasted_iota(jnp.int32, (tm, tw), 1)
    half = tw // 8
    lo = (lane % (2 * half)) < half

    def rope(t):
        rot = jnp.where(lo, -pltpu.roll(t, tw - half, 1), pltpu.roll(t, half, 1))
        return t * cos + rot * sin

    for hd in range(d // tw):
        cols = slice(hd * tw, (hd + 1) * tw)
        q_ref[:, cols] = (rope(y[:, hd * tw:(hd + 1) * tw]) * q_scale).astype(q_ref.dtype)
        k_ref[:, cols] = rope(y[:, d + hd * tw:d + (hd + 1) * tw]).astype(k_ref.dtype)
    v_ref[...] = y[:, 2 * d:].astype(v_ref.dtype)


def _qkv(xs, mod_l, g, w_qkv, cos_t, sin_t, *, rows, n_lat_rows, seq_len, ctx_row, tm, q_scale):
    d = xs.shape[1]
    midx = _mod_index(tm, n_lat_rows, seq_len, ctx_row)
    per_seq = seq_len // tm
    n_lat_tiles = n_lat_rows // tm
    tw = cos_t.shape[1]

    def pos(i):
        return jnp.where(i < n_lat_tiles, i % per_seq, 0)

    kern = functools.partial(_qkv_kernel, n_lat_tiles=n_lat_tiles, q_scale=q_scale)
    return pl.pallas_call(
        kern,
        out_shape=[jax.ShapeDtypeStruct((rows, d), BF16)] * 3,
        grid=(rows // tm,),
        in_specs=[
            pl.BlockSpec((tm, d), lambda i: (i, 0)),
            pl.BlockSpec((None, 6, d), lambda i: (midx(i), 0, 0)),
            pl.BlockSpec((1, d), lambda i: (0, 0)),
            pl.BlockSpec((d, 3 * d), lambda i: (0, 0)),
            pl.BlockSpec((tm, tw), lambda i: (pos(i), 0)),
            pl.BlockSpec((tm, tw), lambda i: (pos(i), 0)),
        ],
        out_specs=[pl.BlockSpec((tm, d), lambda i: (i, 0))] * 3,
        compiler_params=_cparams(("arbitrary",), V7X_VMEM_LIMIT),
        name="attn_qkv",
    )(xs, mod_l, g, w_qkv, cos_t, sin_t)


def _dot_nt(a, b):
    return lax.dot_general(a, b, (((1,), (1,)), ((), ())), preferred_element_type=F32)


def _flash_kernel(lam_ref, sg_ref, q_ref, kc_ref, vc_ref, *rest, tk, n_kx, lam_init):
    if n_kx:
        kx_ref, vx_ref, o_ref, s_even, s_odd = rest
    else:
        (o_ref,) = rest
    tq, dh = q_ref.shape
    q = q_ref[...]
    lane = lax.broadcasted_iota(jnp.int32, (tq, dh), 1)
    zero = jnp.zeros_like(q)
    qa = jnp.where(lane < dh // 2, q, zero)
    qb = jnp.where(lane >= dh // 2, q, zero)

    def v_ext(v):
        return jnp.concatenate([v, jnp.ones_like(v)], axis=1)

    def init(k, v):
        ve = v_ext(v)
        out = []
        for qm in (qa, qb):
            s = _dot_nt(qm, k)
            m = jnp.max(s, axis=-1, keepdims=True)
            p = jnp.exp2(s - m).astype(BF16)
            out += [m, jnp.dot(p, ve, preferred_element_type=F32)]
        return tuple(out)

    def scores(t, s_ref):
        k = kx_ref[t * tk:(t + 1) * tk, :]
        s_ref[0] = _dot_nt(qa, k)
        s_ref[1] = _dot_nt(qb, k)

    def update(t, s_ref, carry):
        ve = v_ext(vx_ref[t * tk:(t + 1) * tk, :])
        out = []
        for mp, (m, acc) in enumerate((carry[0:2], carry[2:4])):
            s = s_ref[mp]
            m_new = jnp.maximum(m, jnp.max(s, axis=-1, keepdims=True))
            p = jnp.exp2(s - m_new).astype(BF16)
            acc = jnp.exp2(m - m_new) * acc + jnp.dot(p, ve, preferred_element_type=F32)
            out += [m_new, acc]
        return tuple(out)

    if n_kx:
        bufs = (s_even, s_odd)
        scores(0, bufs[0])
    carry = init(kc_ref[...], vc_ref[...])
    if n_kx:
        for t in range(n_kx):
            if t + 1 < n_kx:
                scores(t + 1, bufs[(t + 1) % 2])
            carry = update(t, bufs[t % 2], carry)
    _, a1, _, a2 = carry
    lv = lam_ref[...]
    lam = (jnp.exp(jnp.sum(lv[0:1, :] * lv[1:2, :], axis=-1, keepdims=True))
           - jnp.exp(jnp.sum(lv[2:3, :] * lv[3:4, :], axis=-1, keepdims=True)) + lam_init)
    o = a1[:, :dh] / a1[:, dh:] - lam * (a2[:, :dh] / a2[:, dh:])
    ms = jnp.mean(o * o, axis=-1, keepdims=True)
    o = o * lax.rsqrt(ms + NORM_EPS) * sg_ref[...] * (1.0 - lam_init)
    o_ref[...] = o.astype(o_ref.dtype)


def _flash(q, k, v, lam_vecs, subln_g, out_init, *, n_batch, n_heads, q_len, q_off, kc_len, kc_off,
           kx_len, tq, tk, lam_init):
    dh = q.shape[1] // n_heads
    n_q = q_len // tq
    qo = q_off // tq
    kco = kc_off // kc_len
    n_kx = kx_len // tk if kx_len else 0
    in_specs = [
        pl.BlockSpec(lam_vecs.shape, lambda b, h, i: (0, 0)),
        pl.BlockSpec((1, dh), lambda b, h, i: (0, 0)),
        pl.BlockSpec((tq, dh), lambda b, h, i: (qo + b * n_q + i, h)),
        pl.BlockSpec((kc_len, dh), lambda b, h, i: (kco + b, h)),
        pl.BlockSpec((kc_len, dh), lambda b, h, i: (kco + b, h)),
    ]
    args = [lam_vecs, subln_g, q, k, v]
    if n_kx:
        in_specs += [pl.BlockSpec((kx_len, dh), lambda b, h, i: (b, h)),
                     pl.BlockSpec((kx_len, dh), lambda b, h, i: (b, h))]
        args += [k, v]
    kern = functools.partial(_flash_kernel, tk=tk, n_kx=n_kx, lam_init=lam_init)
    if out_init is not None:
        in_specs.insert(0, pl.BlockSpec(memory_space=pl.ANY))
        args.insert(0, out_init)
        kern = functools.partial(_flash_kernel_alias, tk=tk, n_kx=n_kx, lam_init=lam_init)
    return pl.pallas_call(
        kern,
        out_shape=jax.ShapeDtypeStruct(q.shape, q.dtype),
        grid=(n_batch, n_heads, n_q),
        in_specs=in_specs,
        out_specs=pl.BlockSpec((tq, dh), lambda b, h, i: (qo + b * n_q + i, h)),
        scratch_shapes=([pltpu.VMEM((2, tq, tk), F32)] * 2 if n_kx else []),
        input_output_aliases=({0: 0} if out_init is not None else {}),
        compiler_params=_cparams(("arbitrary", "arbitrary", "arbitrary"), V7X_VMEM_LIMIT),
        name="diff_attn_latent" if n_kx else "diff_attn_ctx",
    )(*args)


def _flash_kernel_alias(out_init_ref, *refs, tk, n_kx, lam_init):
    del out_init_ref
    _flash_kernel(*refs, tk=tk, n_kx=n_kx, lam_init=lam_init)


def _proj_residual_kernel(a_ref, w_ref, x_ref, mod_ref, o_ref):
    out = jnp.dot(a_ref[...], w_ref[...], preferred_element_type=F32)
    o_ref[...] = x_ref[...] + mod_ref[2:3, :] * out


def _proj_residual(a, w, xs, mod_l, *, rows, n_lat_rows, seq_len, ctx_row, tm):
    d = xs.shape[1]
    midx = _mod_index(tm, n_lat_rows, seq_len, ctx_row)
    return pl.pallas_call(
        _proj_residual_kernel,
        out_shape=jax.ShapeDtypeStruct(xs.shape, F32),
        grid=(rows // tm,),
        in_specs=[
            pl.BlockSpec((tm, d), lambda i: (i, 0)),
            pl.BlockSpec((d, d), lambda i: (0, 0)),
            pl.BlockSpec((tm, d), lambda i: (i, 0)),
            pl.BlockSpec((None, 6, d), lambda i: (midx(i), 0, 0)),
        ],
        out_specs=pl.BlockSpec((tm, d), lambda i: (i, 0)),
        input_output_aliases={2: 0},
        compiler_params=_cparams(("arbitrary",), V7X_VMEM_LIMIT),
        name="attn_out_proj",
    )(a, w, xs, mod_l)


def _rope_tables(n_tokens, width):
    rope_axis_dim = 32
    freqs = rope_axis_dim // 2
    rows = n_tokens // GRID_W
    row = jnp.repeat(jnp.arange(rows, dtype=F32), GRID_W)
    col = jnp.tile(jnp.arange(GRID_W, dtype=F32), rows)
    inv_freq = 1.0 / (ROPE_BASE ** (jnp.arange(freqs, dtype=F32) * 2.0 / rope_axis_dim))
    ang = jnp.stack([row[:, None] * inv_freq, col[:, None] * inv_freq], axis=1)
    ang = jnp.stack([ang, ang], axis=2).reshape(n_tokens, 4 * freqs)
    ang = jnp.tile(ang, (1, width // (4 * freqs)))
    return jnp.cos(ang), jnp.sin(ang)


def _dense_ffn_layer(xs, mod_l, g, wg, wu, wd, geo, rows):
    tm = geo["tm_ffn"]
    n_tiles = rows // tm
    te = jnp.zeros((n_tiles,), jnp.int32)
    tr = jnp.full((n_tiles,), tm, jnp.int32)
    midx = _mod_index(tm, geo["n_lat_rows"], geo["seq_len"], geo["ctx_row"])
    return _ffn(xs, te, tr, wg, wu, wd, rows=rows, tm=tm, tf=geo["tf"], mod_l=mod_l, g=g, midx=midx)


def _moe_layer(xs, mod_l, g, w_router, wg, wu, wd, geo, rows, final_g=None):
    d = xs.shape[1]
    n_exp = w_router.shape[1]
    tm = geo["tm_moe"]
    common = dict(rows=rows, n_lat_rows=geo["n_lat_rows"], seq_len=geo["seq_len"], ctx_row=geo["ctx_row"])
    wr = jnp.zeros((d, LANES), F32).at[:, :n_exp].set(w_router)
    wr_hi = wr.astype(BF16)
    wr = jnp.stack([wr_hi, (wr - wr_hi.astype(F32)).astype(BF16)])
    h, meta, gw, cnt = _router(xs, mod_l, g, wr, tm=geo["tm_router"], n_exp=n_exp, **common)
    counts = cnt[0, :n_exp].astype(jnp.int32)
    padded = ((counts + tm - 1) // tm) * tm
    ends = jnp.cumsum(padded)
    offs = ends - padded
    n_sorted_tiles = (TOP_K * rows) // tm + n_exp
    tile_start = jnp.arange(n_sorted_tiles, dtype=jnp.int32) * tm
    tile_expert = jnp.minimum(jnp.sum(tile_start[:, None] >= ends[None, :], axis=1), n_exp - 1).astype(jnp.int32)
    tile_rows = jnp.clip((offs + counts)[tile_expert] - tile_start, 0, tm).astype(jnp.int32)
    tile_rows = jnp.where(tile_start < ends[-1], tile_rows, 0)
    e1, e2, r1, r2 = meta[0], meta[1], meta[2], meta[3]
    dest = jnp.stack([offs[e1] + r1, offs[e2] + r2]).astype(jnp.int32)
    hs = _dispatch_rows(h, dest, n_sorted_tiles * tm)
    ys = _ffn(hs, tile_expert, tile_rows, wg, wu, wd, rows=n_sorted_tiles * tm, tm=tm, tf=geo["tf"])
    ypair = _gather_rows(ys, dest.reshape(TOP_K * rows))
    return _combine(xs, ypair, gw, mod_l, tm=geo["tm_row"], final_g=final_g, **common)


def kernel(x, c, ctx, c_ctx, w_mod, b_mod, norm_g, conv_w_in, conv_b_in, conv_w_dw, conv_b_dw,
           conv_ln_g, conv_ln_b, conv_w_out, conv_b_out, fnet_w, fnet_b, attn_w_qkv, attn_lambda,
           attn_subln_g, attn_w_o, ffn_w_gate, ffn_w_up, ffn_w_down, moe_w_router, moe_w_gate,
           moe_w_up, moe_w_down, final_g):
    b_, n, d = x.shape
    n_ctx = ctx.shape[1]
    depth = w_mod.shape[0]
    n_lat = b_ * n
    n_all = n_lat + b_ * n_ctx
    assert b_ < MOD_ROWS and d % LANES == 0 and n % GRID_W == 0 and n % n_ctx == 0
    assert d // DA_HEADS == LANES
    geo = dict(n_lat_rows=n_lat, seq_len=n, ctx_row=b_,
               tm_row=min(1024, n_ctx * b_, n), tm_dft=min(512, n), tm_ffn=min(1024, n_ctx * b_, n), tm_moe=512,
               tm_router=min(512, n_ctx * b_, n), tf=min(512, ffn_w_gate.shape[2]),
               tm_conv=n_ctx)

    xs = jnp.concatenate([x.reshape(n_lat, d), ctx.reshape(b_ * n_ctx, d)], axis=0)
    c_all = jnp.zeros((MOD_ROWS, d), F32).at[:b_].set(c).at[b_].set(c_ctx)
    mod = _modulation(c_all, w_mod, b_mod).reshape(depth, MOD_ROWS, 6, d)

    out = None
    for i in range(depth):
        need_ctx = i < depth - 1
        rows = n_all if need_ctx else n_lat
        kind = i % 3
        mod_l = mod[i]
        common = dict(rows=rows, n_lat_rows=n_lat, seq_len=n, ctx_row=b_)
        g0 = norm_g[i, 0].reshape(1, d)
        g1 = norm_g[i, 1].reshape(1, d)
        j = i // 3
        if kind == 0:
            u = _conv_in(xs, mod_l, g0, conv_w_in[j].astype(BF16), conv_b_in[j].reshape(1, 2 * d),
                         tm=geo["tm_row"], **common)
            xs = _conv_out(u, xs, mod_l, conv_w_dw[j], conv_b_dw[j].reshape(1, d),
                           conv_ln_g[j].reshape(1, d), conv_ln_b[j].reshape(1, d),
                           conv_w_out[j].astype(BF16), conv_b_out[j].reshape(1, d),
                           tm=geo["tm_conv"], **common)
        elif kind == 1:
            gd = d // F_GROUPS
            kk = jnp.arange(gd, dtype=jnp.int32)
            ang = ((kk[:, None] * kk[None, :]) % gd).astype(F32) * (2.0 * math.pi / gd)
            cs = (jnp.concatenate([jnp.cos(ang), jnp.sin(ang)], axis=1) / math.sqrt(gd)).astype(BF16)
            yc, ys = _fnet_a(xs, mod_l, g0, cs, tm=geo["tm_row"], **common)
            wf = fnet_w[j].astype(BF16)
            bf = fnet_b[j].reshape(1, d)
            ct, st = _dft_tables(n)
            xs = _fnet_b(ct, st, yc, ys, xs, mod_l, wf, bf, n_batch=b_, seq_len=n, row_off=0,
                         mod_ctx_row=None, tm=geo["tm_dft"])
            if need_ctx:
                ct, st = _dft_tables(n_ctx)
                xs = _fnet_b(ct, st, yc, ys, xs, mod_l, wf, bf, n_batch=b_, seq_len=n_ctx,
                             row_off=n_lat, mod_ctx_row=b_, tm=n_ctx)
        else:
            lam_init = 0.8 - 0.6 * math.exp(-0.3 * i)
            dh = d // DA_HEADS
            cos_t, sin_t = _rope_tables(n, LANES)
            q, k, v = _qkv(xs, mod_l, g0, attn_w_qkv[j].astype(BF16), cos_t, sin_t,
                           tm=geo["tm_row"], q_scale=(dh // 2) ** -0.5 * math.log2(math.e), **common)
            sg = attn_subln_g[j].reshape(1, dh)
            fl = dict(n_batch=b_, n_heads=DA_HEADS, kc_len=n_ctx, kc_off=n_lat, lam_init=lam_init)
            o = _flash(q, k, v, attn_lambda[j], sg, None, q_len=n, q_off=0, kx_len=n,
                       tq=min(512, n), tk=min(2048, n), **fl)
            if need_ctx:
                o = _flash(q, k, v, attn_lambda[j], sg, o, q_len=n_ctx, q_off=n_lat, kx_len=0,
                           tq=n_ctx, tk=n_ctx, **fl)
            xs = _proj_residual(o, attn_w_o[j].astype(BF16), xs, mod_l, tm=geo["tm_row"], **common)

        j = i // 2
        if i % 2 == 0:
            wg, wu, wd = (_cast_layer_bf16(w[:, None], j) for w in (ffn_w_gate, ffn_w_up, ffn_w_down))
            xs = _dense_ffn_layer(xs, mod_l, g1, wg, wu, wd, geo, rows)
        else:
            fg = final_g.reshape(1, d) if i == depth - 1 else None
            xs, w32 = lax.optimization_barrier((xs, (moe_w_gate, moe_w_up, moe_w_down)))
            wg, wu, wd = (_cast_layer_bf16(w, j) for w in w32)
            xs, wg, wu, wd = lax.optimization_barrier((xs, wg, wu, wd))
            res = _moe_layer(xs, mod_l, g1, moe_w_router[j], wg, wu, wd, geo, rows, final_g=fg)
            if fg is not None:
                out = res
            else:
                xs = res
    if out is None:
        ms = jnp.mean(jnp.square(xs[:n_lat]), axis=-1, keepdims=True)
        out = xs[:n_lat] * lax.rsqrt(ms + NORM_EPS) * final_g
    return out.reshape(b_, n, d)
```

```python
import functools
import math

import jax
import jax.numpy as jnp
from jax import lax
from jax.experimental import pallas as pl
from jax.experimental.pallas import tpu as pltpu
from jax.experimental.pallas import tpu_sc as plsc

F32 = jnp.float32
BF16 = jnp.bfloat16

NORM_EPS = 1e-6
LN_EPS = 1e-5
CONV_WIDTH = 31
CONV_PAD = (CONV_WIDTH - 1) // 2
CONV_HALO = 16
GRID_W = 64
F_GROUPS = 8
DA_HEADS = 8
ROPE_BASE = 10000.0
TOP_K = 2
LANES = 128
SUBLANES = 8
MOD_ROWS = 16
DFT_ROWS = 64
SC_CORES = 2
SC_SUBCORES = 16
SC_INDEX_WINDOW = 128
SC_ROW_BUFFER_BYTES = 256 * 1024

V7X_VMEM_LIMIT = 56 * 1024 * 1024


def _cparams(sem, vmem=None):
    return pltpu.CompilerParams(dimension_semantics=sem, vmem_limit_bytes=vmem)


def _norm_mod(x, g, scale, shift):
    ms = jnp.mean(x * x, axis=-1, keepdims=True)
    y = x * lax.rsqrt(ms + NORM_EPS)
    return (y * g) * (1.0 + scale) + shift


def _pack_bf16_pairs(h):
    half = h.shape[1] // 2
    hb = h.astype(BF16)
    hi = lax.bitcast_convert_type(hb[:, :half].astype(F32), jnp.uint32)
    lo = lax.bitcast_convert_type(hb[:, half:].astype(F32), jnp.uint32)
    return lax.bitcast_convert_type(hi | (lo >> 16), jnp.int32)


def _unpack_bf16_pairs(words):
    w = lax.bitcast_convert_type(words, jnp.uint32)
    hi = lax.bitcast_convert_type(w & jnp.uint32(0xFFFF0000), F32)
    lo = lax.bitcast_convert_type(w << 16, F32)
    return jnp.concatenate([hi, lo], axis=1).astype(BF16)


def _mod_index(tm, n_lat_rows, seq_len, ctx_row):
    n_lat_tiles = n_lat_rows // tm
    per_seq = seq_len // tm

    def f(i):
        return jnp.where(i < n_lat_tiles, i // per_seq, ctx_row)

    return f


def _mod_kernel(c_ref, w_ref, b_ref, o_ref):
    c = c_ref[...]
    sc = c * jax.nn.sigmoid(c)
    o_ref[...] = jnp.dot(sc, w_ref[...], precision=lax.Precision.HIGHEST,
                         preferred_element_type=F32) + b_ref[...]


def _modulation(c_all, w_mod, b_mod):
    depth, d, nd = w_mod.shape
    tn = 1024
    return pl.pallas_call(
        _mod_kernel,
        out_shape=jax.ShapeDtypeStruct((depth, MOD_ROWS, nd), F32),
        grid=(depth, nd // tn),
        in_specs=[
            pl.BlockSpec((MOD_ROWS, d), lambda l, j: (0, 0)),
            pl.BlockSpec((None, d, tn), lambda l, j: (l, 0, j)),
            pl.BlockSpec((None, 1, tn), lambda l, j: (l, 0, j)),
        ],
        out_specs=pl.BlockSpec((None, MOD_ROWS, tn), lambda l, j: (l, 0, j)),
        compiler_params=_cparams(("arbitrary", "arbitrary")),
        name="modulation",
    )(c_all, w_mod, b_mod.reshape(depth, 1, nd))


def _conv_in_kernel(x_ref, mod_ref, g_ref, w_ref, b_ref, u_ref):
    d = u_ref.shape[1]
    h = _norm_mod(x_ref[...], g_ref[...], mod_ref[1:2, :], mod_ref[0:1, :])
    y = jnp.dot(h.astype(BF16), w_ref[...], preferred_element_type=F32) + b_ref[...]
    u_ref[...] = (y[:, :d] * jax.nn.sigmoid(y[:, d:])).astype(u_ref.dtype)


def _conv_in(xs, mod_l, g, w_in, b_in, *, rows, n_lat_rows, seq_len, ctx_row, tm):
    d = xs.shape[1]
    midx = _mod_index(tm, n_lat_rows, seq_len, ctx_row)
    return pl.pallas_call(
        _conv_in_kernel,
        out_shape=jax.ShapeDtypeStruct((rows, d), BF16),
        grid=(rows // tm,),
        in_specs=[
            pl.BlockSpec((tm, d), lambda i: (i, 0)),
            pl.BlockSpec((None, 6, d), lambda i: (midx(i), 0, 0)),
            pl.BlockSpec((1, d), lambda i: (0, 0)),
            pl.BlockSpec((d, 2 * d), lambda i: (0, 0)),
            pl.BlockSpec((1, 2 * d), lambda i: (0, 0)),
        ],
        out_specs=pl.BlockSpec((tm, d), lambda i: (i, 0)),
        compiler_params=_cparams(("arbitrary",), V7X_VMEM_LIMIT),
        name="conv_in",
    )(xs, mod_l, g, w_in, b_in)


def _conv_out_kernel(u_ref, up_ref, un_ref, x_ref, mod_ref, wdw_ref, bdw_ref, lng_ref, lnb_ref,
                     wo_ref, bo_ref, o_ref, ubuf, shifted, cbuf, *, n_lat_tiles, tiles_per_seq):
    tm, d = x_ref.shape
    i = pl.program_id(0)
    is_ctx = i >= n_lat_tiles
    j = i % tiles_per_seq
    first = jnp.logical_or(is_ctx, j == 0)
    last = jnp.logical_or(is_ctx, j == tiles_per_seq - 1)
    ubuf[0:CONV_HALO, :] = jnp.where(first, 0.0, up_ref[...].astype(F32))
    ubuf[CONV_HALO:CONV_HALO + tm, :] = u_ref[...].astype(F32)
    ubuf[CONV_HALO + tm:, :] = jnp.where(last, 0.0, un_ref[...].astype(F32))

    span = ubuf.shape[0] - SUBLANES
    for s in range(1, SUBLANES):
        shifted[s - 1] = ubuf[s:s + span, :]

    for c in range(d // LANES):
        cols = slice(c * LANES, (c + 1) * LANES)
        acc = jnp.zeros((tm, LANES), F32)
        for k in range(CONV_WIDTH):
            r0 = CONV_HALO - CONV_PAD + k
            s, a0 = r0 % SUBLANES, r0 - r0 % SUBLANES
            win = ubuf[a0:a0 + tm, cols] if s == 0 else shifted[s - 1, a0:a0 + tm, cols]
            acc = acc + wdw_ref[k:k + 1, cols] * win
        cbuf[:, cols] = acc + bdw_ref[:, cols]
    v = cbuf[...]
    mu = jnp.mean(v, axis=-1, keepdims=True)
    vc = v - mu
    var = jnp.mean(vc * vc, axis=-1, keepdims=True)
    y = vc * lax.rsqrt(var + LN_EPS) * lng_ref[...] + lnb_ref[...]
    y = y * jax.nn.sigmoid(y)
    out = jnp.dot(y.astype(BF16), wo_ref[...], preferred_element_type=F32) + bo_ref[...]
    o_ref[...] = x_ref[...] + mod_ref[2:3, :] * out


def _conv_out(u, xs, mod_l, w_dw, b_dw, ln_g, ln_b, w_out, b_out, *, rows, n_lat_rows, seq_len,
              ctx_row, tm):
    d = xs.shape[1]
    n_tiles = rows // tm
    n_lat_tiles = n_lat_rows // tm
    hpt = tm // CONV_HALO
    n_halo = u.shape[0] // CONV_HALO
    midx = _mod_index(tm, n_lat_rows, seq_len, ctx_row)
    kern = functools.partial(_conv_out_kernel, n_lat_tiles=n_lat_tiles, tiles_per_seq=seq_len // tm)
    return pl.pallas_call(
        kern,
        out_shape=jax.ShapeDtypeStruct(xs.shape, F32),
        grid=(n_tiles,),
        in_specs=[
            pl.BlockSpec((tm, d), lambda i: (i, 0)),
            pl.BlockSpec((CONV_HALO, d), lambda i: (jnp.maximum(i * hpt - 1, 0), 0)),
            pl.BlockSpec((CONV_HALO, d), lambda i: (jnp.minimum((i + 1) * hpt, n_halo - 1), 0)),
            pl.BlockSpec((tm, d), lambda i: (i, 0)),
            pl.BlockSpec((None, 6, d), lambda i: (midx(i), 0, 0)),
            pl.BlockSpec((CONV_WIDTH, d), lambda i: (0, 0)),
            pl.BlockSpec((1, d), lambda i: (0, 0)),
            pl.BlockSpec((1, d), lambda i: (0, 0)),
            pl.BlockSpec((1, d), lambda i: (0, 0)),
            pl.BlockSpec((d, d), lambda i: (0, 0)),
            pl.BlockSpec((1, d), lambda i: (0, 0)),
        ],
        out_specs=pl.BlockSpec((tm, d), lambda i: (i, 0)),
        scratch_shapes=[pltpu.VMEM((tm + 2 * CONV_HALO, d), F32),
                        pltpu.VMEM((SUBLANES - 1, tm + 2 * CONV_HALO - SUBLANES, d), F32),
                        pltpu.VMEM((tm, d), F32)],
        input_output_aliases={3: 0},
        compiler_params=_cparams(("arbitrary",), V7X_VMEM_LIMIT),
        name="conv_out",
    )(u, u, u, xs, mod_l, w_dw, b_dw, ln_g, ln_b, w_out, b_out)


def _cast_kernel(w_ref, o_ref):
    o_ref[...] = w_ref[...].astype(o_ref.dtype)


def _cast_layer_bf16(w, layer):
    _, n_e, a, b = w.shape
    ta = min(a, 512)
    return pl.pallas_call(
        _cast_kernel,
        out_shape=jax.ShapeDtypeStruct((n_e, a, b), BF16),
        grid=(n_e, a // ta),
        in_specs=[pl.BlockSpec((None, None, ta, b), lambda e, i: (layer, e, i, 0))],
        out_specs=pl.BlockSpec((None, ta, b), lambda e, i: (e, i, 0)),
        compiler_params=_cparams(("arbitrary", "arbitrary"), V7X_VMEM_LIMIT),
        name="cast_weights",
    )(w)


def _ffn_kernel(te_ref, rv_ref, *refs, fuse_norm, tf):
    if fuse_norm:
        x_ref, mod_ref, g_ref, wg_ref, wu_ref, wd_ref, o_ref = refs
    else:
        x_ref, wg_ref, wu_ref, wd_ref, o_ref = refs
    del te_ref
    n_rows = rv_ref[pl.program_id(0)]
    valid = n_rows > 0

    @pl.when(valid)
    def _():
        if fuse_norm:
            h = _norm_mod(x_ref[...], g_ref[...], mod_ref[4:5, :], mod_ref[3:4, :])
        else:
            row = lax.broadcasted_iota(jnp.int32, (x_ref.shape[0], 1), 0)
            h = _unpack_bf16_pairs(jnp.where(row < n_rows, x_ref[...], 0))
        h = h.astype(BF16)
        acc = None
        for c in range(wg_ref.shape[1] // tf):
            cols = slice(c * tf, (c + 1) * tf)
            gt = jnp.dot(h, wg_ref[:, cols], preferred_element_type=F32)
            up = jnp.dot(h, wu_ref[:, cols], preferred_element_type=F32)
            a = (gt * jax.nn.sigmoid(gt) * up).astype(BF16)
            part = jnp.dot(a, wd_ref[cols, :], preferred_element_type=F32)
            acc = part if acc is None else acc + part
        if fuse_norm:
            o_ref[...] = x_ref[...] + mod_ref[5:6, :] * acc
        else:
            o_ref[...] = acc

    @pl.when(jnp.logical_not(valid))
    def _():
        o_ref[...] = jnp.zeros_like(o_ref)


def _ffn(x, tile_expert, tile_rows, wg, wu, wd, *, rows, tm, tf, mod_l=None, g=None, midx=None):
    d = wg.shape[1]
    f_dim = wg.shape[2]
    fuse_norm = mod_l is not None
    resident = pl.Buffered(1)
    in_specs = [pl.BlockSpec((tm, x.shape[1]), lambda j, te, nv: (j, 0))]
    args = [x]
    if fuse_norm:
        in_specs += [pl.BlockSpec((None, 6, d), lambda j, te, nv: (midx(j), 0, 0)),
                     pl.BlockSpec((1, d), lambda j, te, nv: (0, 0))]
        args += [mod_l, g]
    in_specs += [
        pl.BlockSpec((None, d, f_dim), lambda j, te, nv: (te[j], 0, 0), pipeline_mode=resident),
        pl.BlockSpec((None, d, f_dim), lambda j, te, nv: (te[j], 0, 0), pipeline_mode=resident),
        pl.BlockSpec((None, f_dim, d), lambda j, te, nv: (te[j], 0, 0), pipeline_mode=resident),
    ]
    args += [wg, wu, wd]
    kern = functools.partial(_ffn_kernel, fuse_norm=fuse_norm, tf=tf)
    return pl.pallas_call(
        kern,
        out_shape=jax.ShapeDtypeStruct((x.shape[0], d), F32),
        grid_spec=pltpu.PrefetchScalarGridSpec(
            num_scalar_prefetch=2,
            grid=(rows // tm,),
            in_specs=in_specs,
            out_specs=pl.BlockSpec((tm, d), lambda j, te, nv: (j, 0)),
        ),
        input_output_aliases=({2: 0} if fuse_norm else {}),
        compiler_params=_cparams(("arbitrary",), V7X_VMEM_LIMIT),
        name="ffn_dense" if fuse_norm else "ffn_grouped",
    )(tile_expert, tile_rows, *args)


def _router_kernel(x_ref, mod_ref, g_ref, wr_ref, h_ref, meta_ref, gw_ref, cnt_ref, carry, *, n_exp):
    tm = x_ref.shape[0]
    i = pl.program_id(0)

    @pl.when(i == 0)
    def _():
        carry[...] = jnp.zeros_like(carry)

    h = _norm_mod(x_ref[...], g_ref[...], mod_ref[4:5, :], mod_ref[3:4, :])
    h_ref[...] = _pack_bf16_pairs(h)
    h_hi = h.astype(BF16)
    h_lo = (h - h_hi.astype(F32)).astype(BF16)
    logits = (jnp.dot(h_hi, wr_ref[0], preferred_element_type=F32)
              + (jnp.dot(h_hi, wr_ref[1], preferred_element_type=F32)
                 + jnp.dot(h_lo, wr_ref[0], preferred_element_type=F32)))
    lane = lax.broadcasted_iota(jnp.int32, (tm, LANES), 1).astype(F32)
    neg = jnp.float32(-jnp.inf)
    lg = jnp.where(lane < n_exp, logits, neg)
    m1 = jnp.max(lg, axis=-1, keepdims=True)
    i1 = jnp.min(jnp.where(lg == m1, lane, float(LANES)), axis=-1, keepdims=True)
    lg2 = jnp.where(lane == i1, neg, lg)
    m2 = jnp.max(lg2, axis=-1, keepdims=True)
    i2 = jnp.min(jnp.where(lg2 == m2, lane, float(LANES)), axis=-1, keepdims=True)
    e2 = jnp.exp(m2 - m1)
    w1 = 1.0 / (1.0 + e2)
    w2 = e2 * w1
    sel1 = lane == i1
    sel2 = lane == i2
    onehot = jnp.where(jnp.logical_or(sel1, sel2), 1.0, 0.0)
    rr = lax.broadcasted_iota(jnp.int32, (tm, tm), 0)
    cc = lax.broadcasted_iota(jnp.int32, (tm, tm), 1)
    tri = jnp.where(rr > cc, 1.0, 0.0).astype(BF16)
    cum = jnp.dot(tri, onehot.astype(BF16), preferred_element_type=F32) + carry[...]
    r1 = jnp.sum(jnp.where(sel1, cum, 0.0), axis=-1, keepdims=True)
    r2 = jnp.sum(jnp.where(sel2, cum, 0.0), axis=-1, keepdims=True)
    carry[...] += jnp.sum(onehot, axis=0, keepdims=True)
    meta = jnp.where(lane == 0, i1, jnp.where(lane == 1, i2, jnp.where(lane == 2, r1,
                     jnp.where(lane == 3, r2, 0.0))))
    meta_ref[...] = meta.T[:SUBLANES, :].astype(jnp.int32)
    gw_ref[...] = jnp.where(lane == 0, w1, jnp.where(lane == 1, w2, 0.0))
    cnt_ref[...] = jnp.broadcast_to(carry[...], cnt_ref.shape)


def _router(xs, mod_l, g, w_router_pad, *, rows, n_lat_rows, seq_len, ctx_row, tm, n_exp):
    d = xs.shape[1]
    midx = _mod_index(tm, n_lat_rows, seq_len, ctx_row)
    kern = functools.partial(_router_kernel, n_exp=n_exp)
    return pl.pallas_call(
        kern,
        out_shape=[jax.ShapeDtypeStruct((rows, d // 2), jnp.int32),
                   jax.ShapeDtypeStruct((SUBLANES, rows), jnp.int32),
                   jax.ShapeDtypeStruct((rows, LANES), F32),
                   jax.ShapeDtypeStruct((8, LANES), F32)],
        grid=(rows // tm,),
        in_specs=[
            pl.BlockSpec((tm, d), lambda i: (i, 0)),
            pl.BlockSpec((None, 6, d), lambda i: (midx(i), 0, 0)),
            pl.BlockSpec((1, d), lambda i: (0, 0)),
            pl.BlockSpec((2, d, LANES), lambda i: (0, 0, 0)),
        ],
        out_specs=[pl.BlockSpec((tm, d // 2), lambda i: (i, 0)),
                   pl.BlockSpec((SUBLANES, tm), lambda i: (0, i)),
                   pl.BlockSpec((tm, LANES), lambda i: (i, 0)),
                   pl.BlockSpec((8, LANES), lambda i: (0, 0))],
        scratch_shapes=[pltpu.VMEM((1, LANES), F32)],
        compiler_params=_cparams(("arbitrary",), V7X_VMEM_LIMIT),
        name="router",
    )(xs, mod_l, g, w_router_pad)


def _sc_rows_per_copy(d, dtype):
    return min(SC_INDEX_WINDOW, SC_ROW_BUFFER_BYTES // (d * jnp.dtype(dtype).itemsize))


def _dispatch_rows(src, dest, n_out):
    t, d = src.shape
    n_slots = dest.shape[0]
    assert t % SC_INDEX_WINDOW == 0
    n_win = t // SC_INDEX_WINDOW
    n_workers = SC_CORES * SC_SUBCORES
    n_rows = _sc_rows_per_copy(d, src.dtype)
    mesh = plsc.VectorSubcoreMesh(core_axis_name="core", subcore_axis_name="subcore")

    @pl.kernel(out_type=jax.ShapeDtypeStruct((n_out, d), src.dtype), mesh=mesh,
               scratch_types=[pltpu.VMEM((n_slots, SC_INDEX_WINDOW), jnp.int32),
                              pltpu.VMEM((n_rows, d), src.dtype)])
    def dispatch(src_hbm, idx_hbm, out_hbm, idx_v, buf):
        wid = lax.axis_index("core") * SC_SUBCORES + lax.axis_index("subcore")

        @pl.loop(0, (n_win - wid + n_workers - 1) // n_workers)
        def _(b):
            base = (b * n_workers + wid) * SC_INDEX_WINDOW
            pltpu.sync_copy(idx_hbm.at[:, pl.ds(base, SC_INDEX_WINDOW)], idx_v)
            for k in range(SC_INDEX_WINDOW // n_rows):
                rows = pl.ds(k * n_rows, n_rows)
                pltpu.sync_copy(src_hbm.at[pl.ds(base + k * n_rows, n_rows)], buf)
                for s in range(n_slots):
                    pltpu.sync_copy(buf, out_hbm.at[idx_v.at[s, rows]])

    return dispatch(src, dest)


def _gather_rows(src, idx):
    n = idx.shape[0]
    d = src.shape[1]
    n_workers = SC_CORES * SC_SUBCORES
    per = n // n_workers
    assert n % (n_workers * SC_INDEX_WINDOW) == 0, (n, n_workers, SC_INDEX_WINDOW)
    n_rows = _sc_rows_per_copy(d, src.dtype)
    mesh = plsc.VectorSubcoreMesh(core_axis_name="core", subcore_axis_name="subcore")

    @pl.kernel(out_type=jax.ShapeDtypeStruct((n, d), src.dtype), mesh=mesh,
               scratch_types=[pltpu.VMEM((1, SC_INDEX_WINDOW), jnp.int32),
                              pltpu.VMEM((n_rows, d), src.dtype)])
    def gather(src_hbm, idx_hbm, out_hbm, idx_v, buf):
        wid = lax.axis_index("core") * SC_SUBCORES + lax.axis_index("subcore")

        @pl.loop(0, per // SC_INDEX_WINDOW)
        def _(b):
            base = wid * per + b * SC_INDEX_WINDOW
            pltpu.sync_copy(idx_hbm.at[:, pl.ds(base, SC_INDEX_WINDOW)], idx_v)
            for k in range(SC_INDEX_WINDOW // n_rows):
                rows = pl.ds(k * n_rows, n_rows)
                pltpu.sync_copy(src_hbm.at[idx_v.at[0, rows]], buf)
                pltpu.sync_copy(buf, out_hbm.at[pl.ds(base + k * n_rows, n_rows)])

    return gather(src, idx.reshape(1, n))


def _combine_kernel(x_ref, y1_ref, y2_ref, gw_ref, mod_ref, *rest, final):
    if final:
        fg_ref, o_ref = rest
    else:
        (o_ref,) = rest
    gw = gw_ref[...]
    y = gw[:, 0:1] * y1_ref[...] + gw[:, 1:2] * y2_ref[...]
    xn = x_ref[...] + mod_ref[5:6, :] * y
    if final:
        ms = jnp.mean(xn * xn, axis=-1, keepdims=True)
        xn = xn * lax.rsqrt(ms + NORM_EPS) * fg_ref[...]
    o_ref[...] = xn


def _combine(xs, ypair, gw, mod_l, *, rows, n_lat_rows, seq_len, ctx_row, tm, final_g=None):
    d = xs.shape[1]
    n_tiles = rows // tm
    midx = _mod_index(tm, n_lat_rows, seq_len, ctx_row)
    final = final_g is not None
    in_specs = [
        pl.BlockSpec((tm, d), lambda i: (i, 0)),
        pl.BlockSpec((tm, d), lambda i: (i, 0)),
        pl.BlockSpec((tm, d), lambda i: (i + n_tiles, 0)),
        pl.BlockSpec((tm, LANES), lambda i: (i, 0)),
        pl.BlockSpec((None, 6, d), lambda i: (midx(i), 0, 0)),
    ]
    args = [xs, ypair, ypair, gw, mod_l]
    if final:
        in_specs.append(pl.BlockSpec((1, d), lambda i: (0, 0)))
        args.append(final_g)
    out_rows = rows if final else xs.shape[0]
    return pl.pallas_call(
        functools.partial(_combine_kernel, final=final),
        out_shape=jax.ShapeDtypeStruct((out_rows, d), F32),
        grid=(n_tiles,),
        in_specs=in_specs,
        out_specs=pl.BlockSpec((tm, d), lambda i: (i, 0)),
        input_output_aliases=({} if final else {0: 0}),
        compiler_params=_cparams(("arbitrary",), V7X_VMEM_LIMIT),
        name="moe_combine",
    )(*args)


def _fnet_a_kernel(x_ref, mod_ref, g_ref, cs_ref, yc_ref, ys_ref, *, n_groups):
    h = _norm_mod(x_ref[...], g_ref[...], mod_ref[1:2, :], mod_ref[0:1, :]).astype(BF16)
    gd = cs_ref.shape[0]
    for gi in range(n_groups):
        y = jnp.dot(h[:, gi * gd:(gi + 1) * gd], cs_ref[...], preferred_element_type=F32)
        yc_ref[:, gi * gd:(gi + 1) * gd] = y[:, :gd].astype(yc_ref.dtype)
        ys_ref[:, gi * gd:(gi + 1) * gd] = y[:, gd:].astype(ys_ref.dtype)


def _fnet_a(xs, mod_l, g, cs, *, rows, n_lat_rows, seq_len, ctx_row, tm):
    d = xs.shape[1]
    gd = cs.shape[0]
    midx = _mod_index(tm, n_lat_rows, seq_len, ctx_row)
    return pl.pallas_call(
        functools.partial(_fnet_a_kernel, n_groups=d // gd),
        out_shape=[jax.ShapeDtypeStruct((rows, d), BF16)] * 2,
        grid=(rows // tm,),
        in_specs=[
            pl.BlockSpec((tm, d), lambda i: (i, 0)),
            pl.BlockSpec((None, 6, d), lambda i: (midx(i), 0, 0)),
            pl.BlockSpec((1, d), lambda i: (0, 0)),
            pl.BlockSpec((gd, 2 * gd), lambda i: (0, 0)),
        ],
        out_specs=[pl.BlockSpec((tm, d), lambda i: (i, 0))] * 2,
        compiler_params=_cparams(("arbitrary",), V7X_VMEM_LIMIT),
        name="fnet_group_dft",
    )(xs, mod_l, g, cs)


def _dft_table_kernel(ac_ref, as_ref, bc_ref, bs_ref, c_ref, sn_ref):
    ac, as_ = ac_ref[...], as_ref[...]
    bc, bs = bc_ref[...], bs_ref[...]
    c_ref[...] = (bc * ac - bs * as_).astype(c_ref.dtype)
    sn_ref[...] = (-(bs * ac + bc * as_)).astype(sn_ref.dtype)


def _dft_tables(n):
    r = DFT_ROWS
    k = jnp.arange(n, dtype=jnp.int32)[None, :]
    j1 = jnp.arange(r, dtype=jnp.int32)[:, None]
    j0 = (jnp.arange(n // r, dtype=jnp.int32) * r)[:, None]
    ang1 = ((j1 * k) % n).astype(F32) * (2.0 * math.pi / n)
    ang0 = ((j0 * k) % n).astype(F32) * (2.0 * math.pi / n)
    scale = 1.0 / math.sqrt(n)
    ac, as_ = jnp.cos(ang1), jnp.sin(ang1)
    bc = (jnp.cos(ang0) * scale).reshape(n // r, 1, n)
    bs = (jnp.sin(ang0) * scale).reshape(n // r, 1, n)
    return pl.pallas_call(
        _dft_table_kernel,
        out_shape=[jax.ShapeDtypeStruct((n, n), BF16)] * 2,
        grid=(n // r,),
        in_specs=[
            pl.BlockSpec((r, n), lambda i: (0, 0)),
            pl.BlockSpec((r, n), lambda i: (0, 0)),
            pl.BlockSpec((None, 1, n), lambda i: (i, 0, 0)),
            pl.BlockSpec((None, 1, n), lambda i: (i, 0, 0)),
        ],
        out_specs=[pl.BlockSpec((r, n), lambda i: (i, 0))] * 2,
        compiler_params=_cparams(("arbitrary",)),
        name="dft_tables",
    )(ac, as_, bc, bs)


def _fnet_b_kernel(c_ref, sn_ref, yc_ref, ys_ref, x_ref, mod_ref, wf_ref, bf_ref, o_ref):
    z = (jnp.dot(c_ref[...], yc_ref[...], preferred_element_type=F32)
         + jnp.dot(sn_ref[...], ys_ref[...], preferred_element_type=F32))
    out = jnp.dot(z.astype(BF16), wf_ref[...], preferred_element_type=F32) + bf_ref[...]
    o_ref[...] = x_ref[...] + mod_ref[2:3, :] * out


def _fnet_b(ctab, stab, yc, ys, xs, mod_l, wf, bf, *, n_batch, seq_len, row_off, mod_ctx_row, tm):
    d = xs.shape[1]
    n_i = seq_len // tm
    off_m = row_off // tm
    off_b = row_off // seq_len
    resident = pl.Buffered(1)

    def mrow(b):
        return b if mod_ctx_row is None else mod_ctx_row

    return pl.pallas_call(
        _fnet_b_kernel,
        out_shape=jax.ShapeDtypeStruct(xs.shape, F32),
        grid=(n_batch, n_i),
        in_specs=[
            pl.BlockSpec((tm, seq_len), lambda b, i: (i, 0)),
            pl.BlockSpec((tm, seq_len), lambda b, i: (i, 0)),
            pl.BlockSpec((seq_len, d), lambda b, i: (off_b + b, 0), pipeline_mode=resident),
            pl.BlockSpec((seq_len, d), lambda b, i: (off_b + b, 0), pipeline_mode=resident),
            pl.BlockSpec((tm, d), lambda b, i: (off_m + b * n_i + i, 0)),
            pl.BlockSpec((None, 6, d), lambda b, i: (mrow(b), 0, 0)),
            pl.BlockSpec((d, d), lambda b, i: (0, 0)),
            pl.BlockSpec((1, d), lambda b, i: (0, 0)),
        ],
        out_specs=pl.BlockSpec((tm, d), lambda b, i: (off_m + b * n_i + i, 0)),
        input_output_aliases={4: 0},
        compiler_params=_cparams(("arbitrary", "arbitrary"), V7X_VMEM_LIMIT),
        name="fnet_seq_dft",
    )(ctab, stab, yc, ys, xs, mod_l, wf, bf)


def _qkv_kernel(x_ref, mod_ref, g_ref, w_ref, cos_ref, sin_ref, q_ref, k_ref, v_ref, *,
                n_lat_tiles, q_scale):
    tm, d = x_ref.shape
    i = pl.program_id(0)
    is_ctx = i >= n_lat_tiles
    h = _norm_mod(x_ref[...], g_ref[...], mod_ref[1:2, :], mod_ref[0:1, :])
    y = jnp.dot(h.astype(BF16), w_ref[...], preferred_element_type=F32)
    tw = cos_ref.shape[1]
    cos = jnp.where(is_ctx, 1.0, cos_ref[...])
    sin = jnp.where(is_ctx, 0.0, sin_ref[...])
    lane = lax.broadcasted_iota(jnp.int32, (tm, tw), 1)
    half = tw // 8
    lo = (lane % (2 * half)) < half

    def rope(t):
        rot = jnp.where(lo, -pltpu.roll(t, tw - half, 1), pltpu.roll(t, half, 1))
        return t * cos + rot * sin

    for hd in range(d // tw):
        cols = slice(hd * tw, (hd + 1) * tw)
        q_ref[:, cols] = (rope(y[:, hd * tw:(hd + 1) * tw]) * q_scale).astype(q_ref.dtype)
        k_ref[:, cols] = rope(y[:, d + hd * tw:d + (hd + 1) * tw]).astype(k_ref.dtype)
    v_ref[...] = y[:, 2 * d:].astype(v_ref.dtype)


def _qkv(xs, mod_l, g, w_qkv, cos_t, sin_t, *, rows, n_lat_rows, seq_len, ctx_row, tm, q_scale):
    d = xs.shape[1]
    midx = _mod_index(tm, n_lat_rows, seq_len, ctx_row)
    per_seq = seq_len // tm
    n_lat_tiles = n_lat_rows // tm
    tw = cos_t.shape[1]

    def pos(i):
        return jnp.where(i < n_lat_tiles, i % per_seq, 0)

    kern = functools.partial(_qkv_kernel, n_lat_tiles=n_lat_tiles, q_scale=q_scale)
    return pl.pallas_call(
        kern,
        out_shape=[jax.ShapeDtypeStruct((rows, d), BF16)] * 3,
        grid=(rows // tm,),
        in_specs=[
            pl.BlockSpec((tm, d), lambda i: (i, 0)),
            pl.BlockSpec((None, 6, d), lambda i: (midx(i), 0, 0)),
            pl.BlockSpec((1, d), lambda i: (0, 0)),
            pl.BlockSpec((d, 3 * d), lambda i: (0, 0)),
            pl.BlockSpec((tm, tw), lambda i: (pos(i), 0)),
            pl.BlockSpec((tm, tw), lambda i: (pos(i), 0)),
        ],
        out_specs=[pl.BlockSpec((tm, d), lambda i: (i, 0))] * 3,
        compiler_params=_cparams(("arbitrary",), V7X_VMEM_LIMIT),
        name="attn_qkv",
    )(xs, mod_l, g, w_qkv, cos_t, sin_t)


def _dot_nt(a, b):
    return lax.dot_general(a, b, (((1,), (1,)), ((), ())), preferred_element_type=F32)


def _flash_kernel(lam_ref, sg_ref, q_ref, kc_ref, vc_ref, *rest, tk, n_kx, lam_init):
    if n_kx:
        kx_ref, vx_ref, o_ref, s_even, s_odd = rest
    else:
        (o_ref,) = rest
    tq, dh = q_ref.shape
    q = q_ref[...]
    lane = lax.broadcasted_iota(jnp.int32, (tq, dh), 1)
    zero = jnp.zeros_like(q)
    qa = jnp.where(lane < dh // 2, q, zero)
    qb = jnp.where(lane >= dh // 2, q, zero)

    def v_ext(v):
        return jnp.concatenate([v, jnp.ones_like(v)], axis=1)

    def init(k, v):
        ve = v_ext(v)
        out = []
        for qm in (qa, qb):
            s = _dot_nt(qm, k)
            m = jnp.max(s, axis=-1, keepdims=True)
            p = jnp.exp2(s - m).astype(BF16)
            out += [m, jnp.dot(p, ve, preferred_element_type=F32)]
        return tuple(out)

    def scores(t, s_ref):
        k = kx_ref[t * tk:(t + 1) * tk, :]
        s_ref[0] = _dot_nt(qa, k)
        s_ref[1] = _dot_nt(qb, k)

    def update(t, s_ref, carry):
        ve = v_ext(vx_ref[t * tk:(t + 1) * tk, :])
        out = []
        for mp, (m, acc) in enumerate((carry[0:2], carry[2:4])):
            s = s_ref[mp]
            m_new = jnp.maximum(m, jnp.max(s, axis=-1, keepdims=True))
            p = jnp.exp2(s - m_new).astype(BF16)
            acc = jnp.exp2(m - m_new) * acc + jnp.dot(p, ve, preferred_element_type=F32)
            out += [m_new, acc]
        return tuple(out)

    if n_kx:
        bufs = (s_even, s_odd)
        scores(0, bufs[0])
    carry = init(kc_ref[...], vc_ref[...])
    if n_kx:
        for t in range(n_kx):
            if t + 1 < n_kx:
                scores(t + 1, bufs[(t + 1) % 2])
            carry = update(t, bufs[t % 2], carry)
    _, a1, _, a2 = carry
    lv = lam_ref[...]
    lam = (jnp.exp(jnp.sum(lv[0:1, :] * lv[1:2, :], axis=-1, keepdims=True))
           - jnp.exp(jnp.sum(lv[2:3, :] * lv[3:4, :], axis=-1, keepdims=True)) + lam_init)
    o = a1[:, :dh] / a1[:, dh:] - lam * (a2[:, :dh] / a2[:, dh:])
    ms = jnp.mean(o * o, axis=-1, keepdims=True)
    o = o * lax.rsqrt(ms + NORM_EPS) * sg_ref[...] * (1.0 - lam_init)
    o_ref[...] = o.astype(o_ref.dtype)


def _flash(q, k, v, lam_vecs, subln_g, out_init, *, n_batch, n_heads, q_len, q_off, kc_len, kc_off,
           kx_len, tq, tk, lam_init):
    dh = q.shape[1] // n_heads
    n_q = q_len // tq
    qo = q_off // tq
    kco = kc_off // kc_len
    n_kx = kx_len // tk if kx_len else 0
    in_specs = [
        pl.BlockSpec(lam_vecs.shape, lambda b, h, i: (0, 0)),
        pl.BlockSpec((1, dh), lambda b, h, i: (0, 0)),
        pl.BlockSpec((tq, dh), lambda b, h, i: (qo + b * n_q + i, h)),
        pl.BlockSpec((kc_len, dh), lambda b, h, i: (kco + b, h)),
        pl.BlockSpec((kc_len, dh), lambda b, h, i: (kco + b, h)),
    ]
    args = [lam_vecs, subln_g, q, k, v]
    if n_kx:
        in_specs += [pl.BlockSpec((kx_len, dh), lambda b, h, i: (b, h)),
                     pl.BlockSpec((kx_len, dh), lambda b, h, i: (b, h))]
        args += [k, v]
    kern = functools.partial(_flash_kernel, tk=tk, n_kx=n_kx, lam_init=lam_init)
    if out_init is not None:
        in_specs.insert(0, pl.BlockSpec(memory_space=pl.ANY))
        args.insert(0, out_init)
        kern = functools.partial(_flash_kernel_alias, tk=tk, n_kx=n_kx, lam_init=lam_init)
    return pl.pallas_call(
        kern,
        out_shape=jax.ShapeDtypeStruct(q.shape, q.dtype),
        grid=(n_batch, n_heads, n_q),
        in_specs=in_specs,
        out_specs=pl.BlockSpec((tq, dh), lambda b, h, i: (qo + b * n_q + i, h)),
        scratch_shapes=([pltpu.VMEM((2, tq, tk), F32)] * 2 if n_kx else []),
        input_output_aliases=({0: 0} if out_init is not None else {}),
        compiler_params=_cparams(("arbitrary", "arbitrary", "arbitrary"), V7X_VMEM_LIMIT),
        name="diff_attn_latent" if n_kx else "diff_attn_ctx",
    )(*args)


def _flash_kernel_alias(out_init_ref, *refs, tk, n_kx, lam_init):
    del out_init_ref
    _flash_kernel(*refs, tk=tk, n_kx=n_kx, lam_init=lam_init)


def _proj_residual_kernel(a_ref, w_ref, x_ref, mod_ref, o_ref):
    out = jnp.dot(a_ref[...], w_ref[...], preferred_element_type=F32)
    o_ref[...] = x_ref[...] + mod_ref[2:3, :] * out


def _proj_residual(a, w, xs, mod_l, *, rows, n_lat_rows, seq_len, ctx_row, tm):
    d = xs.shape[1]
    midx = _mod_index(tm, n_lat_rows, seq_len, ctx_row)
    return pl.pallas_call(
        _proj_residual_kernel,
        out_shape=jax.ShapeDtypeStruct(xs.shape, F32),
        grid=(rows // tm,),
        in_specs=[
            pl.BlockSpec((tm, d), lambda i: (i, 0)),
            pl.BlockSpec((d, d), lambda i: (0, 0)),
            pl.BlockSpec((tm, d), lambda i: (i, 0)),
            pl.BlockSpec((None, 6, d), lambda i: (midx(i), 0, 0)),
        ],
        out_specs=pl.BlockSpec((tm, d), lambda i: (i, 0)),
        input_output_aliases={2: 0},
        compiler_params=_cparams(("arbitrary",), V7X_VMEM_LIMIT),
        name="attn_out_proj",
    )(a, w, xs, mod_l)


def _rope_tables(n_tokens, width):
    rope_axis_dim = 32
    freqs = rope_axis_dim // 2
    rows = n_tokens // GRID_W
    row = jnp.repeat(jnp.arange(rows, dtype=F32), GRID_W)
    col = jnp.tile(jnp.arange(GRID_W, dtype=F32), rows)
    inv_freq = 1.0 / (ROPE_BASE ** (jnp.arange(freqs, dtype=F32) * 2.0 / rope_axis_dim))
    ang = jnp.stack([row[:, None] * inv_freq, col[:, None] * inv_freq], axis=1)
    ang = jnp.stack([ang, ang], axis=2).reshape(n_tokens, 4 * freqs)
    ang = jnp.tile(ang, (1, width // (4 * freqs)))
    return jnp.cos(ang), jnp.sin(ang)


def _dense_ffn_layer(xs, mod_l, g, wg, wu, wd, geo, rows):
    tm = geo["tm_ffn"]
    n_tiles = rows // tm
    te = jnp.zeros((n_tiles,), jnp.int32)
    tr = jnp.full((n_tiles,), tm, jnp.int32)
    midx = _mod_index(tm, geo["n_lat_rows"], geo["seq_len"], geo["ctx_row"])
    return _ffn(xs, te, tr, wg, wu, wd, rows=rows, tm=tm, tf=geo["tf"], mod_l=mod_l, g=g, midx=midx)


def _moe_layer(xs, mod_l, g, w_router, wg, wu, wd, geo, rows, final_g=None):
    d = xs.shape[1]
    n_exp = w_router.shape[1]
    tm = geo["tm_moe"]
    common = dict(rows=rows, n_lat_rows=geo["n_lat_rows"], seq_len=geo["seq_len"], ctx_row=geo["ctx_row"])
    wr = jnp.zeros((d, LANES), F32).at[:, :n_exp].set(w_router)
    wr_hi = wr.astype(BF16)
    wr = jnp.stack([wr_hi, (wr - wr_hi.astype(F32)).astype(BF16)])
    h, meta, gw, cnt = _router(xs, mod_l, g, wr, tm=geo["tm_router"], n_exp=n_exp, **common)
    counts = cnt[0, :n_exp].astype(jnp.int32)
    padded = ((counts + tm - 1) // tm) * tm
    ends = jnp.cumsum(padded)
    offs = ends - padded
    n_sorted_tiles = (TOP_K * rows) // tm + n_exp
    tile_start = jnp.arange(n_sorted_tiles, dtype=jnp.int32) * tm
    tile_expert = jnp.minimum(jnp.sum(tile_start[:, None] >= ends[None, :], axis=1), n_exp - 1).astype(jnp.int32)
    tile_rows = jnp.clip((offs + counts)[tile_expert] - tile_start, 0, tm).astype(jnp.int32)
    tile_rows = jnp.where(tile_start < ends[-1], tile_rows, 0)
    e1, e2, r1, r2 = meta[0], meta[1], meta[2], meta[3]
    dest = jnp.stack([offs[e1] + r1, offs[e2] + r2]).astype(jnp.int32)
    hs = _dispatch_rows(h, dest, n_sorted_tiles * tm)
    ys = _ffn(hs, tile_expert, tile_rows, wg, wu, wd, rows=n_sorted_tiles * tm, tm=tm, tf=geo["tf"])
    ypair = _gather_rows(ys, dest.reshape(TOP_K * rows))
    return _combine(xs, ypair, gw, mod_l, tm=geo["tm_row"], final_g=final_g, **common)


def kernel(x, c, ctx, c_ctx, w_mod, b_mod, norm_g, conv_w_in, conv_b_in, conv_w_dw, conv_b_dw,
           conv_ln_g, conv_ln_b, conv_w_out, conv_b_out, fnet_w, fnet_b, attn_w_qkv, attn_lambda,
           attn_subln_g, attn_w_o, ffn_w_gate, ffn_w_up, ffn_w_down, moe_w_router, moe_w_gate,
           moe_w_up, moe_w_down, final_g):
    b_, n, d = x.shape
    n_ctx = ctx.shape[1]
    depth = w_mod.shape[0]
    n_lat = b_ * n
    n_all = n_lat + b_ * n_ctx
    assert b_ < MOD_ROWS and d % LANES == 0 and n % GRID_W == 0 and n % n_ctx == 0
    assert d // DA_HEADS == LANES
    geo = dict(n_lat_rows=n_lat, seq_len=n, ctx_row=b_,
               tm_row=min(1024, n_ctx * b_, n), tm_dft=min(512, n), tm_ffn=min(1024, n_ctx * b_, n), tm_moe=512,
               tm_router=min(512, n_ctx * b_, n), tf=min(512, ffn_w_gate.shape[2]),
               tm_conv=n_ctx)

    xs = jnp.concatenate([x.reshape(n_lat, d), ctx.reshape(b_ * n_ctx, d)], axis=0)
    c_all = jnp.zeros((MOD_ROWS, d), F32).at[:b_].set(c).at[b_].set(c_ctx)
    mod = _modulation(c_all, w_mod, b_mod).reshape(depth, MOD_ROWS, 6, d)

    out = None
    for i in range(depth):
        need_ctx = i < depth - 1
        rows = n_all if need_ctx else n_lat
        kind = i % 3
        mod_l = mod[i]
        common = dict(rows=rows, n_lat_rows=n_lat, seq_len=n, ctx_row=b_)
        g0 = norm_g[i, 0].reshape(1, d)
        g1 = norm_g[i, 1].reshape(1, d)
        j = i // 3
        if kind == 0:
            u = _conv_in(xs, mod_l, g0, conv_w_in[j].astype(BF16), conv_b_in[j].reshape(1, 2 * d),
                         tm=geo["tm_row"], **common)
            xs = _conv_out(u, xs, mod_l, conv_w_dw[j], conv_b_dw[j].reshape(1, d),
                           conv_ln_g[j].reshape(1, d), conv_ln_b[j].reshape(1, d),
                           conv_w_out[j].astype(BF16), conv_b_out[j].reshape(1, d),
                           tm=geo["tm_conv"], **common)
        elif kind == 1:
            gd = d // F_GROUPS
            kk = jnp.arange(gd, dtype=jnp.int32)
            ang = ((kk[:, None] * kk[None, :]) % gd).astype(F32) * (2.0 * math.pi / gd)
            cs = (jnp.concatenate([jnp.cos(ang), jnp.sin(ang)], axis=1) / math.sqrt(gd)).astype(BF16)
            yc, ys = _fnet_a(xs, mod_l, g0, cs, tm=geo["tm_row"], **common)
            wf = fnet_w[j].astype(BF16)
            bf = fnet_b[j].reshape(1, d)
            ct, st = _dft_tables(n)
            xs = _fnet_b(ct, st, yc, ys, xs, mod_l, wf, bf, n_batch=b_, seq_len=n, row_off=0,
                         mod_ctx_row=None, tm=geo["tm_dft"])
            if need_ctx:
                ct, st = _dft_tables(n_ctx)
                xs = _fnet_b(ct, st, yc, ys, xs, mod_l, wf, bf, n_batch=b_, seq_len=n_ctx,
                             row_off=n_lat, mod_ctx_row=b_, tm=n_ctx)
        else:
            lam_init = 0.8 - 0.6 * math.exp(-0.3 * i)
            dh = d // DA_HEADS
            cos_t, sin_t = _rope_tables(n, LANES)
            q, k, v = _qkv(xs, mod_l, g0, attn_w_qkv[j].astype(BF16), cos_t, sin_t,
                           tm=geo["tm_row"], q_scale=(dh // 2) ** -0.5 * math.log2(math.e), **common)
            sg = attn_subln_g[j].reshape(1, dh)
            fl = dict(n_batch=b_, n_heads=DA_HEADS, kc_len=n_ctx, kc_off=n_lat, lam_init=lam_init)
            o = _flash(q, k, v, attn_lambda[j], sg, None, q_len=n, q_off=0, kx_len=n,
                       tq=min(512, n), tk=min(2048, n), **fl)
            if need_ctx:
                o = _flash(q, k, v, attn_lambda[j], sg, o, q_len=n_ctx, q_off=n_lat, kx_len=0,
                           tq=n_ctx, tk=n_ctx, **fl)
            xs = _proj_residual(o, attn_w_o[j].astype(BF16), xs, mod_l, tm=geo["tm_row"], **common)

        j = i // 2
        if i % 2 == 0:
            wg, wu, wd = (_cast_layer_bf16(w[:, None], j) for w in (ffn_w_gate, ffn_w_up, ffn_w_down))
            xs = _dense_ffn_layer(xs, mod_l, g1, wg, wu, wd, geo, rows)
        else:
            fg = final_g.reshape(1, d) if i == depth - 1 else None
            xs, w32 = lax.optimization_barrier((xs, (moe_w_gate, moe_w_up, moe_w_down)))
            wg, wu, wd = (_cast_layer_bf16(w, j) for w in w32)
            xs, wg, wu, wd = lax.optimization_barrier((xs, wg, wu, wd))
            res = _moe_layer(xs, mod_l, g1, moe_w_router[j], wg, wu, wd, geo, rows, final_g=fg)
            if fg is not None:
                out = res
            else:
                xs = res
    assert out is not None, "the final rms_norm is fused into the last layer's MoE combine (odd last layer index)"
    return out.reshape(b_, n, d)
```

```python
import functools
import math

import jax
import jax.numpy as jnp
from jax import lax
from jax.experimental import pallas as pl
from jax.experimental.pallas import tpu as pltpu
from jax.experimental.pallas import tpu_sc as plsc

F32 = jnp.float32
BF16 = jnp.bfloat16

NORM_EPS = 1e-6
LN_EPS = 1e-5
CONV_WIDTH = 31
CONV_PAD = (CONV_WIDTH - 1) // 2
CONV_HALO = 16
GRID_W = 64
F_GROUPS = 8
DA_HEADS = 8
ROPE_BASE = 10000.0
TOP_K = 2
LANES = 128
SUBLANES = 8
MOD_ROWS = 16
DFT_ROWS = 64
SC_CORES = 2
SC_SUBCORES = 16
SC_INDEX_WINDOW = 128
SC_ROW_BUFFER_BYTES = 256 * 1024

V7X_VMEM_LIMIT = 56 * 1024 * 1024


def _cparams(sem, vmem=None):
    return pltpu.CompilerParams(dimension_semantics=sem, vmem_limit_bytes=vmem)


def _norm_mod(x, g, scale, shift):
    ms = jnp.mean(x * x, axis=-1, keepdims=True)
    y = x * lax.rsqrt(ms + NORM_EPS)
    return (y * g) * (1.0 + scale) + shift


def _pack_bf16_pairs(h):
    half = h.shape[1] // 2
    hb = h.astype(BF16)
    hi = lax.bitcast_convert_type(hb[:, :half].astype(F32), jnp.uint32)
    lo = lax.bitcast_convert_type(hb[:, half:].astype(F32), jnp.uint32)
    return lax.bitcast_convert_type(hi | (lo >> 16), jnp.int32)


def _unpack_bf16_pairs(words):
    w = lax.bitcast_convert_type(words, jnp.uint32)
    hi = lax.bitcast_convert_type(w & jnp.uint32(0xFFFF0000), F32)
    lo = lax.bitcast_convert_type(w << 16, F32)
    return jnp.concatenate([hi, lo], axis=1).astype(BF16)


def _mod_index(tm, n_lat_rows, seq_len, ctx_row):
    n_lat_tiles = n_lat_rows // tm
    per_seq = seq_len // tm

    def f(i):
        return jnp.where(i < n_lat_tiles, i // per_seq, ctx_row)

    return f


def _mod_kernel(c_ref, w_ref, b_ref, o_ref):
    c = c_ref[...]
    sc = c * jax.nn.sigmoid(c)
    o_ref[...] = jnp.dot(sc, w_ref[...], precision=lax.Precision.HIGHEST,
                         preferred_element_type=F32) + b_ref[...]


def _modulation(c_all, w_mod, b_mod):
    depth, d, nd = w_mod.shape
    tn = 1024
    return pl.pallas_call(
        _mod_kernel,
        out_shape=jax.ShapeDtypeStruct((depth, MOD_ROWS, nd), F32),
        grid=(depth, nd // tn),
        in_specs=[
            pl.BlockSpec((MOD_ROWS, d), lambda l, j: (0, 0)),
            pl.BlockSpec((None, d, tn), lambda l, j: (l, 0, j)),
            pl.BlockSpec((None, 1, tn), lambda l, j: (l, 0, j)),
        ],
        out_specs=pl.BlockSpec((None, MOD_ROWS, tn), lambda l, j: (l, 0, j)),
        compiler_params=_cparams(("arbitrary", "arbitrary")),
        name="modulation",
    )(c_all, w_mod, b_mod.reshape(depth, 1, nd))


def _conv_in_kernel(x_ref, mod_ref, g_ref, w_ref, b_ref, u_ref):
    d = u_ref.shape[1]
    h = _norm_mod(x_ref[...], g_ref[...], mod_ref[1:2, :], mod_ref[0:1, :])
    y = jnp.dot(h.astype(BF16), w_ref[...], preferred_element_type=F32) + b_ref[...]
    u_ref[...] = (y[:, :d] * jax.nn.sigmoid(y[:, d:])).astype(u_ref.dtype)


def _conv_in(xs, mod_l, g, w_in, b_in, *, rows, n_lat_rows, seq_len, ctx_row, tm):
    d = xs.shape[1]
    midx = _mod_index(tm, n_lat_rows, seq_len, ctx_row)
    return pl.pallas_call(
        _conv_in_kernel,
        out_shape=jax.ShapeDtypeStruct((rows, d), BF16),
        grid=(rows // tm,),
        in_specs=[
            pl.BlockSpec((tm, d), lambda i: (i, 0)),
            pl.BlockSpec((None, 6, d), lambda i: (midx(i), 0, 0)),
            pl.BlockSpec((1, d), lambda i: (0, 0)),
            pl.BlockSpec((d, 2 * d), lambda i: (0, 0)),
            pl.BlockSpec((1, 2 * d), lambda i: (0, 0)),
        ],
        out_specs=pl.BlockSpec((tm, d), lambda i: (i, 0)),
        compiler_params=_cparams(("arbitrary",), V7X_VMEM_LIMIT),
        name="conv_in",
    )(xs, mod_l, g, w_in, b_in)


def _conv_out_kernel(u_ref, up_ref, un_ref, x_ref, mod_ref, wdw_ref, bdw_ref, lng_ref, lnb_ref,
                     wo_ref, bo_ref, o_ref, ubuf, shifted, cbuf, *, n_lat_tiles, tiles_per_seq):
    tm, d = x_ref.shape
    i = pl.program_id(0)
    is_ctx = i >= n_lat_tiles
    j = i % tiles_per_seq
    first = jnp.logical_or(is_ctx, j == 0)
    last = jnp.logical_or(is_ctx, j == tiles_per_seq - 1)
    ubuf[0:CONV_HALO, :] = jnp.where(first, 0.0, up_ref[...].astype(F32))
    ubuf[CONV_HALO:CONV_HALO + tm, :] = u_ref[...].astype(F32)
    ubuf[CONV_HALO + tm:, :] = jnp.where(last, 0.0, un_ref[...].astype(F32))

    span = ubuf.shape[0] - SUBLANES
    for s in range(1, SUBLANES):
        shifted[s - 1] = ubuf[s:s + span, :]

    for c in range(d // LANES):
        cols = slice(c * LANES, (c + 1) * LANES)
        acc = jnp.zeros((tm, LANES), F32)
        for k in range(CONV_WIDTH):
            r0 = CONV_HALO - CONV_PAD + k
            s, a0 = r0 % SUBLANES, r0 - r0 % SUBLANES
            win = ubuf[a0:a0 + tm, cols] if s == 0 else shifted[s - 1, a0:a0 + tm, cols]
            acc = acc + wdw_ref[k:k + 1, cols] * win
        cbuf[:, cols] = acc + bdw_ref[:, cols]
    v = cbuf[...]
    mu = jnp.mean(v, axis=-1, keepdims=True)
    vc = v - mu
    var = jnp.mean(vc * vc, axis=-1, keepdims=True)
    y = vc * lax.rsqrt(var + LN_EPS) * lng_ref[...] + lnb_ref[...]
    y = y * jax.nn.sigmoid(y)
    out = jnp.dot(y.astype(BF16), wo_ref[...], preferred_element_type=F32) + bo_ref[...]
    o_ref[...] = x_ref[...] + mod_ref[2:3, :] * out


def _conv_out(u, xs, mod_l, w_dw, b_dw, ln_g, ln_b, w_out, b_out, *, rows, n_lat_rows, seq_len,
              ctx_row, tm):
    d = xs.shape[1]
    n_tiles = rows // tm
    n_lat_tiles = n_lat_rows // tm
    hpt = tm // CONV_HALO
    n_halo = u.shape[0] // CONV_HALO
    midx = _mod_index(tm, n_lat_rows, seq_len, ctx_row)
    kern = functools.partial(_conv_out_kernel, n_lat_tiles=n_lat_tiles, tiles_per_seq=seq_len // tm)
    return pl.pallas_call(
        kern,
        out_shape=jax.ShapeDtypeStruct(xs.shape, F32),
        grid=(n_tiles,),
        in_specs=[
            pl.BlockSpec((tm, d), lambda i: (i, 0)),
            pl.BlockSpec((CONV_HALO, d), lambda i: (jnp.maximum(i * hpt - 1, 0), 0)),
            pl.BlockSpec((CONV_HALO, d), lambda i: (jnp.minimum((i + 1) * hpt, n_halo - 1), 0)),
            pl.BlockSpec((tm, d), lambda i: (i, 0)),
            pl.BlockSpec((None, 6, d), lambda i: (midx(i), 0, 0)),
            pl.BlockSpec((CONV_WIDTH, d), lambda i: (0, 0)),
            pl.BlockSpec((1, d), lambda i: (0, 0)),
            pl.BlockSpec((1, d), lambda i: (0, 0)),
            pl.BlockSpec((1, d), lambda i: (0, 0)),
            pl.BlockSpec((d, d), lambda i: (0, 0)),
            pl.BlockSpec((1, d), lambda i: (0, 0)),
        ],
        out_specs=pl.BlockSpec((tm, d), lambda i: (i, 0)),
        scratch_shapes=[pltpu.VMEM((tm + 2 * CONV_HALO, d), F32),
                        pltpu.VMEM((SUBLANES - 1, tm + 2 * CONV_HALO - SUBLANES, d), F32),
                        pltpu.VMEM((tm, d), F32)],
        input_output_aliases={3: 0},
        compiler_params=_cparams(("arbitrary",), V7X_VMEM_LIMIT),
        name="conv_out",
    )(u, u, u, xs, mod_l, w_dw, b_dw, ln_g, ln_b, w_out, b_out)


def _cast_kernel(w_ref, o_ref):
    o_ref[...] = w_ref[...].astype(o_ref.dtype)


def _cast_layer_bf16(w, layer):
    _, n_e, a, b = w.shape
    ta = min(a, 512)
    return pl.pallas_call(
        _cast_kernel,
        out_shape=jax.ShapeDtypeStruct((n_e, a, b), BF16),
        grid=(n_e, a // ta),
        in_specs=[pl.BlockSpec((None, None, ta, b), lambda e, i: (layer, e, i, 0))],
        out_specs=pl.BlockSpec((None, ta, b), lambda e, i: (e, i, 0)),
        compiler_params=_cparams(("arbitrary", "arbitrary"), V7X_VMEM_LIMIT),
        name="cast_weights",
    )(w)


def _ffn_kernel(te_ref, rv_ref, *refs, fuse_norm, tf):
    if fuse_norm:
        x_ref, mod_ref, g_ref, wg_ref, wu_ref, wd_ref, o_ref = refs
    else:
        x_ref, wg_ref, wu_ref, wd_ref, o_ref = refs
    del te_ref
    n_rows = rv_ref[pl.program_id(0)]
    valid = n_rows > 0

    @pl.when(valid)
    def _():
        if fuse_norm:
            h = _norm_mod(x_ref[...], g_ref[...], mod_ref[4:5, :], mod_ref[3:4, :])
        else:
            row = lax.broadcasted_iota(jnp.int32, (x_ref.shape[0], 1), 0)
            h = _unpack_bf16_pairs(jnp.where(row < n_rows, x_ref[...], 0))
        h = h.astype(BF16)
        acc = None
        for c in range(wg_ref.shape[1] // tf):
            cols = slice(c * tf, (c + 1) * tf)
            gt = jnp.dot(h, wg_ref[:, cols], preferred_element_type=F32)
            up = jnp.dot(h, wu_ref[:, cols], preferred_element_type=F32)
            a = (gt * jax.nn.sigmoid(gt) * up).astype(BF16)
            part = jnp.dot(a, wd_ref[cols, :], preferred_element_type=F32)
            acc = part if acc is None else acc + part
        if fuse_norm:
            o_ref[...] = x_ref[...] + mod_ref[5:6, :] * acc
        else:
            o_ref[...] = acc

    @pl.when(jnp.logical_not(valid))
    def _():
        o_ref[...] = jnp.zeros_like(o_ref)


def _ffn(x, tile_expert, tile_rows, wg, wu, wd, *, rows, tm, tf, mod_l=None, g=None, midx=None):
    d = wg.shape[1]
    f_dim = wg.shape[2]
    fuse_norm = mod_l is not None
    resident = pl.Buffered(1)
    in_specs = [pl.BlockSpec((tm, x.shape[1]), lambda j, te, nv: (j, 0))]
    args = [x]
    if fuse_norm:
        in_specs += [pl.BlockSpec((None, 6, d), lambda j, te, nv: (midx(j), 0, 0)),
                     pl.BlockSpec((1, d), lambda j, te, nv: (0, 0))]
        args += [mod_l, g]
    in_specs += [
        pl.BlockSpec((None, d, f_dim), lambda j, te, nv: (te[j], 0, 0), pipeline_mode=resident),
        pl.BlockSpec((None, d, f_dim), lambda j, te, nv: (te[j], 0, 0), pipeline_mode=resident),
        pl.BlockSpec((None, f_dim, d), lambda j, te, nv: (te[j], 0, 0), pipeline_mode=resident),
    ]
    args += [wg, wu, wd]
    kern = functools.partial(_ffn_kernel, fuse_norm=fuse_norm, tf=tf)
    return pl.pallas_call(
        kern,
        out_shape=jax.ShapeDtypeStruct((x.shape[0], d), F32),
        grid_spec=pltpu.PrefetchScalarGridSpec(
            num_scalar_prefetch=2,
            grid=(rows // tm,),
            in_specs=in_specs,
            out_specs=pl.BlockSpec((tm, d), lambda j, te, nv: (j, 0)),
        ),
        input_output_aliases=({2: 0} if fuse_norm else {}),
        compiler_params=_cparams(("arbitrary",), V7X_VMEM_LIMIT),
        name="ffn_dense" if fuse_norm else "ffn_grouped",
    )(tile_expert, tile_rows, *args)


def _router_kernel(x_ref, mod_ref, g_ref, wr_ref, h_ref, meta_ref, gw_ref, cnt_ref, carry, *, n_exp):
    tm = x_ref.shape[0]
    i = pl.program_id(0)

    @pl.when(i == 0)
    def _():
        carry[...] = jnp.zeros_like(carry)

    h = _norm_mod(x_ref[...], g_ref[...], mod_ref[4:5, :], mod_ref[3:4, :])
    h_ref[...] = _pack_bf16_pairs(h)
    h_hi = h.astype(BF16)
    h_lo = (h - h_hi.astype(F32)).astype(BF16)
    logits = (jnp.dot(h_hi, wr_ref[0], preferred_element_type=F32)
              + (jnp.dot(h_hi, wr_ref[1], preferred_element_type=F32)
                 + jnp.dot(h_lo, wr_ref[0], preferred_element_type=F32)))
    lane = lax.broadcasted_iota(jnp.int32, (tm, LANES), 1).astype(F32)
    neg = jnp.float32(-jnp.inf)
    lg = jnp.where(lane < n_exp, logits, neg)
    m1 = jnp.max(lg, axis=-1, keepdims=True)
    i1 = jnp.min(jnp.where(lg == m1, lane, float(LANES)), axis=-1, keepdims=True)
    lg2 = jnp.where(lane == i1, neg, lg)
    m2 = jnp.max(lg2, axis=-1, keepdims=True)
    i2 = jnp.min(jnp.where(lg2 == m2, lane, float(LANES)), axis=-1, keepdims=True)
    e2 = jnp.exp(m2 - m1)
    w1 = 1.0 / (1.0 + e2)
    w2 = e2 * w1
    sel1 = lane == i1
    sel2 = lane == i2
    onehot = jnp.where(jnp.logical_or(sel1, sel2), 1.0, 0.0)
    rr = lax.broadcasted_iota(jnp.int32, (tm, tm), 0)
    cc = lax.broadcasted_iota(jnp.int32, (tm, tm), 1)
    tri = jnp.where(rr > cc, 1.0, 0.0).astype(BF16)
    cum = jnp.dot(tri, onehot.astype(BF16), preferred_element_type=F32) + carry[...]
    r1 = jnp.sum(jnp.where(sel1, cum, 0.0), axis=-1, keepdims=True)
    r2 = jnp.sum(jnp.where(sel2, cum, 0.0), axis=-1, keepdims=True)
    carry[...] += jnp.sum(onehot, axis=0, keepdims=True)
    meta = jnp.where(lane == 0, i1, jnp.where(lane == 1, i2, jnp.where(lane == 2, r1,
                     jnp.where(lane == 3, r2, 0.0))))
    meta_ref[...] = meta.T[:SUBLANES, :].astype(jnp.int32)
    gw_ref[...] = jnp.where(lane == 0, w1, jnp.where(lane == 1, w2, 0.0))
    cnt_ref[...] = jnp.broadcast_to(carry[...], cnt_ref.shape)


def _router(xs, mod_l, g, w_router_pad, *, rows, n_lat_rows, seq_len, ctx_row, tm, n_exp):
    d = xs.shape[1]
    midx = _mod_index(tm, n_lat_rows, seq_len, ctx_row)
    kern = functools.partial(_router_kernel, n_exp=n_exp)
    return pl.pallas_call(
        kern,
        out_shape=[jax.ShapeDtypeStruct((rows, d // 2), jnp.int32),
                   jax.ShapeDtypeStruct((SUBLANES, rows), jnp.int32),
                   jax.ShapeDtypeStruct((rows, LANES), F32),
                   jax.ShapeDtypeStruct((8, LANES), F32)],
        grid=(rows // tm,),
        in_specs=[
            pl.BlockSpec((tm, d), lambda i: (i, 0)),
            pl.BlockSpec((None, 6, d), lambda i: (midx(i), 0, 0)),
            pl.BlockSpec((1, d), lambda i: (0, 0)),
            pl.BlockSpec((2, d, LANES), lambda i: (0, 0, 0)),
        ],
        out_specs=[pl.BlockSpec((tm, d // 2), lambda i: (i, 0)),
                   pl.BlockSpec((SUBLANES, tm), lambda i: (0, i)),
                   pl.BlockSpec((tm, LANES), lambda i: (i, 0)),
                   pl.BlockSpec((8, LANES), lambda i: (0, 0))],
        scratch_shapes=[pltpu.VMEM((1, LANES), F32)],
        compiler_params=_cparams(("arbitrary",), V7X_VMEM_LIMIT),
        name="router",
    )(xs, mod_l, g, w_router_pad)


def _sc_rows_per_copy(d, dtype):
    return min(SC_INDEX_WINDOW, SC_ROW_BUFFER_BYTES // (d * jnp.dtype(dtype).itemsize))


def _dispatch_rows(src, dest, n_out):
    t, d = src.shape
    n_slots = dest.shape[0]
    assert t % SC_INDEX_WINDOW == 0
    n_win = t // SC_INDEX_WINDOW
    n_workers = SC_CORES * SC_SUBCORES
    n_rows = _sc_rows_per_copy(d, src.dtype)
    mesh = plsc.VectorSubcoreMesh(core_axis_name="core", subcore_axis_name="subcore")

    @pl.kernel(out_type=jax.ShapeDtypeStruct((n_out, d), src.dtype), mesh=mesh,
               scratch_types=[pltpu.VMEM((n_slots, SC_INDEX_WINDOW), jnp.int32),
                              pltpu.VMEM((n_rows, d), src.dtype)])
    def dispatch(src_hbm, idx_hbm, out_hbm, idx_v, buf):
        wid = lax.axis_index("core") * SC_SUBCORES + lax.axis_index("subcore")

        @pl.loop(0, (n_win - wid + n_workers - 1) // n_workers)
        def _(b):
            base = (b * n_workers + wid) * SC_INDEX_WINDOW
            pltpu.sync_copy(idx_hbm.at[:, pl.ds(base, SC_INDEX_WINDOW)], idx_v)
            for k in range(SC_INDEX_WINDOW // n_rows):
                rows = pl.ds(k * n_rows, n_rows)
                pltpu.sync_copy(src_hbm.at[pl.ds(base + k * n_rows, n_rows)], buf)
                for s in range(n_slots):
                    pltpu.sync_copy(buf, out_hbm.at[idx_v.at[s, rows]])

    return dispatch(src, dest)


def _gather_rows(src, idx):
    n = idx.shape[0]
    d = src.shape[1]
    n_workers = SC_CORES * SC_SUBCORES
    per = n // n_workers
    assert n % (n_workers * SC_INDEX_WINDOW) == 0, (n, n_workers, SC_INDEX_WINDOW)
    n_rows = _sc_rows_per_copy(d, src.dtype)
    mesh = plsc.VectorSubcoreMesh(core_axis_name="core", subcore_axis_name="subcore")

    @pl.kernel(out_type=jax.ShapeDtypeStruct((n, d), src.dtype), mesh=mesh,
               scratch_types=[pltpu.VMEM((1, SC_INDEX_WINDOW), jnp.int32),
                              pltpu.VMEM((n_rows, d), src.dtype)])
    def gather(src_hbm, idx_hbm, out_hbm, idx_v, buf):
        wid = lax.axis_index("core") * SC_SUBCORES + lax.axis_index("subcore")

        @pl.loop(0, per // SC_INDEX_WINDOW)
        def _(b):
            base = wid * per + b * SC_INDEX_WINDOW
            pltpu.sync_copy(idx_hbm.at[:, pl.ds(base, SC_INDEX_WINDOW)], idx_v)
            for k in range(SC_INDEX_WINDOW // n_rows):
                rows = pl.ds(k * n_rows, n_rows)
                pltpu.sync_copy(src_hbm.at[idx_v.at[0, rows]], buf)
                pltpu.sync_copy(buf, out_hbm.at[pl.ds(base + k * n_rows, n_rows)])

    return gather(src, idx.reshape(1, n))


def _combine_kernel(x_ref, y1_ref, y2_ref, gw_ref, mod_ref, *rest, final):
    if final:
        fg_ref, o_ref = rest
    else:
        (o_ref,) = rest
    gw = gw_ref[...]
    y = gw[:, 0:1] * y1_ref[...] + gw[:, 1:2] * y2_ref[...]
    xn = x_ref[...] + mod_ref[5:6, :] * y
    if final:
        ms = jnp.mean(xn * xn, axis=-1, keepdims=True)
        xn = xn * lax.rsqrt(ms + NORM_EPS) * fg_ref[...]
    o_ref[...] = xn


def _combine(xs, ypair, gw, mod_l, *, rows, n_lat_rows, seq_len, ctx_row, tm, final_g=None):
    d = xs.shape[1]
    n_tiles = rows // tm
    midx = _mod_index(tm, n_lat_rows, seq_len, ctx_row)
    final = final_g is not None
    in_specs = [
        pl.BlockSpec((tm, d), lambda i: (i, 0)),
        pl.BlockSpec((tm, d), lambda i: (i, 0)),
        pl.BlockSpec((tm, d), lambda i: (i + n_tiles, 0)),
        pl.BlockSpec((tm, LANES), lambda i: (i, 0)),
        pl.BlockSpec((None, 6, d), lambda i: (midx(i), 0, 0)),
    ]
    args = [xs, ypair, ypair, gw, mod_l]
    if final:
        in_specs.append(pl.BlockSpec((1, d), lambda i: (0, 0)))
        args.append(final_g)
    out_rows = rows if final else xs.shape[0]
    return pl.pallas_call(
        functools.partial(_combine_kernel, final=final),
        out_shape=jax.ShapeDtypeStruct((out_rows, d), F32),
        grid=(n_tiles,),
        in_specs=in_specs,
        out_specs=pl.BlockSpec((tm, d), lambda i: (i, 0)),
        input_output_aliases=({} if final else {0: 0}),
        compiler_params=_cparams(("arbitrary",), V7X_VMEM_LIMIT),
        name="moe_combine",
    )(*args)


def _fnet_a_kernel(x_ref, mod_ref, g_ref, cs_ref, yc_ref, ys_ref, *, n_groups):
    h = _norm_mod(x_ref[...], g_ref[...], mod_ref[1:2, :], mod_ref[0:1, :]).astype(BF16)
    gd = cs_ref.shape[0]
    for gi in range(n_groups):
        y = jnp.dot(h[:, gi * gd:(gi + 1) * gd], cs_ref[...], preferred_element_type=F32)
        yc_ref[:, gi * gd:(gi + 1) * gd] = y[:, :gd].astype(yc_ref.dtype)
        ys_ref[:, gi * gd:(gi + 1) * gd] = y[:, gd:].astype(ys_ref.dtype)


def _fnet_a(xs, mod_l, g, cs, *, rows, n_lat_rows, seq_len, ctx_row, tm):
    d = xs.shape[1]
    gd = cs.shape[0]
    midx = _mod_index(tm, n_lat_rows, seq_len, ctx_row)
    return pl.pallas_call(
        functools.partial(_fnet_a_kernel, n_groups=d // gd),
        out_shape=[jax.ShapeDtypeStruct((rows, d), BF16)] * 2,
        grid=(rows // tm,),
        in_specs=[
            pl.BlockSpec((tm, d), lambda i: (i, 0)),
            pl.BlockSpec((None, 6, d), lambda i: (midx(i), 0, 0)),
            pl.BlockSpec((1, d), lambda i: (0, 0)),
            pl.BlockSpec((gd, 2 * gd), lambda i: (0, 0)),
        ],
        out_specs=[pl.BlockSpec((tm, d), lambda i: (i, 0))] * 2,
        compiler_params=_cparams(("arbitrary",), V7X_VMEM_LIMIT),
        name="fnet_group_dft",
    )(xs, mod_l, g, cs)


def _dft_table_kernel(ac_ref, as_ref, bc_ref, bs_ref, c_ref, sn_ref):
    ac, as_ = ac_ref[...], as_ref[...]
    bc, bs = bc_ref[...], bs_ref[...]
    c_ref[...] = (bc * ac - bs * as_).astype(c_ref.dtype)
    sn_ref[...] = (-(bs * ac + bc * as_)).astype(sn_ref.dtype)


def _dft_tables(n):
    r = DFT_ROWS
    k = jnp.arange(n, dtype=jnp.int32)[None, :]
    j1 = jnp.arange(r, dtype=jnp.int32)[:, None]
    j0 = (jnp.arange(n // r, dtype=jnp.int32) * r)[:, None]
    ang1 = ((j1 * k) % n).astype(F32) * (2.0 * math.pi / n)
    ang0 = ((j0 * k) % n).astype(F32) * (2.0 * math.pi / n)
    scale = 1.0 / math.sqrt(n)
    ac, as_ = jnp.cos(ang1), jnp.sin(ang1)
    bc = (jnp.cos(ang0) * scale).reshape(n // r, 1, n)
    bs = (jnp.sin(ang0) * scale).reshape(n // r, 1, n)
    return pl.pallas_call(
        _dft_table_kernel,
        out_shape=[jax.ShapeDtypeStruct((n, n), BF16)] * 2,
        grid=(n // r,),
        in_specs=[
            pl.BlockSpec((r, n), lambda i: (0, 0)),
            pl.BlockSpec((r, n), lambda i: (0, 0)),
            pl.BlockSpec((None, 1, n), lambda i: (i, 0, 0)),
            pl.BlockSpec((None, 1, n), lambda i: (i, 0, 0)),
        ],
        out_specs=[pl.BlockSpec((r, n), lambda i: (i, 0))] * 2,
        compiler_params=_cparams(("arbitrary",)),
        name="dft_tables",
    )(ac, as_, bc, bs)


def _fnet_b_kernel(c_ref, sn_ref, yc_ref, ys_ref, x_ref, mod_ref, wf_ref, bf_ref, o_ref):
    z = (jnp.dot(c_ref[...], yc_ref[...], preferred_element_type=F32)
         + jnp.dot(sn_ref[...], ys_ref[...], preferred_element_type=F32))
    out = jnp.dot(z.astype(BF16), wf_ref[...], preferred_element_type=F32) + bf_ref[...]
    o_ref[...] = x_ref[...] + mod_ref[2:3, :] * out


def _fnet_b(ctab, stab, yc, ys, xs, mod_l, wf, bf, *, n_batch, seq_len, row_off, mod_ctx_row, tm):
    d = xs.shape[1]
    n_i = seq_len // tm
    off_m = row_off // tm
    off_b = row_off // seq_len
    resident = pl.Buffered(1)

    def mrow(b):
        return b if mod_ctx_row is None else mod_ctx_row

    return pl.pallas_call(
        _fnet_b_kernel,
        out_shape=jax.ShapeDtypeStruct(xs.shape, F32),
        grid=(n_batch, n_i),
        in_specs=[
            pl.BlockSpec((tm, seq_len), lambda b, i: (i, 0)),
            pl.BlockSpec((tm, seq_len), lambda b, i: (i, 0)),
            pl.BlockSpec((seq_len, d), lambda b, i: (off_b + b, 0), pipeline_mode=resident),
            pl.BlockSpec((seq_len, d), lambda b, i: (off_b + b, 0), pipeline_mode=resident),
            pl.BlockSpec((tm, d), lambda b, i: (off_m + b * n_i + i, 0)),
            pl.BlockSpec((None, 6, d), lambda b, i: (mrow(b), 0, 0)),
            pl.BlockSpec((d, d), lambda b, i: (0, 0)),
            pl.BlockSpec((1, d), lambda b, i: (0, 0)),
        ],
        out_specs=pl.BlockSpec((tm, d), lambda b, i: (off_m + b * n_i + i, 0)),
        input_output_aliases={4: 0},
        compiler_params=_cparams(("arbitrary", "arbitrary"), V7X_VMEM_LIMIT),
        name="fnet_seq_dft",
    )(ctab, stab, yc, ys, xs, mod_l, wf, bf)


def _qkv_kernel(x_ref, mod_ref, g_ref, w_ref, cos_ref, sin_ref, q_ref, k_ref, v_ref, *,
                n_lat_tiles, q_scale):
    tm, d = x_ref.shape
    i = pl.program_id(0)
    is_ctx = i >= n_lat_tiles
    h = _norm_mod(x_ref[...], g_ref[...], mod_ref[1:2, :], mod_ref[0:1, :])
    y = jnp.dot(h.astype(BF16), w_ref[...], preferred_element_type=F32)
    tw = cos_ref.shape[1]
    cos = jnp.where(is_ctx, 1.0, cos_ref[...])
    sin = jnp.where(is_ctx, 0.0, sin_ref[...])
    lane = lax.broadcasted_iota(jnp.int32, (tm, tw), 1)
    half = tw // 8
    lo = (lane % (2 * half)) < half

    def rope(t):
        rot = jnp.where(lo, -pltpu.roll(t, tw - half, 1), pltpu.roll(t, half, 1))
        return t * cos + rot * sin

    for hd in range(d // tw):
        cols = slice(hd * tw, (hd + 1) * tw)
        q_ref[:, cols] = (rope(y[:, hd * tw:(hd + 1) * tw]) * q_scale).astype(q_ref.dtype)
        k_ref[:, cols] = rope(y[:, d + hd * tw:d + (hd + 1) * tw]).astype(k_ref.dtype)
    v_ref[...] = y[:, 2 * d:].astype(v_ref.dtype)


def _qkv(xs, mod_l, g, w_qkv, cos_t, sin_t, *, rows, n_lat_rows, seq_len, ctx_row, tm, q_scale):
    d = xs.shape[1]
    midx = _mod_index(tm, n_lat_rows, seq_len, ctx_row)
    per_seq = seq_len // tm
    n_lat_tiles = n_lat_rows // tm
    tw = cos_t.shape[1]

    def pos(i):
        return jnp.where(i < n_lat_tiles, i % per_seq, 0)

    kern = functools.partial(_qkv_kernel, n_lat_tiles=n_lat_tiles, q_scale=q_scale)
    return pl.pallas_call(
        kern,
        out_shape=[jax.ShapeDtypeStruct((rows, d), BF16)] * 3,
        grid=(rows // tm,),
        in_specs=[
            pl.BlockSpec((tm, d), lambda i: (i, 0)),
            pl.BlockSpec((None, 6, d), lambda i: (midx(i), 0, 0)),
            pl.BlockSpec((1, d), lambda i: (0, 0)),
            pl.BlockSpec((d, 3 * d), lambda i: (0, 0)),
            pl.BlockSpec((tm, tw), lambda i: (pos(i), 0)),
            pl.BlockSpec((tm, tw), lambda i: (pos(i), 0)),
        ],
        out_specs=[pl.BlockSpec((tm, d), lambda i: (i, 0))] * 3,
        compiler_params=_cparams(("arbitrary",), V7X_VMEM_LIMIT),
        name="attn_qkv",
    )(xs, mod_l, g, w_qkv, cos_t, sin_t)


def _dot_nt(a, b):
    return lax.dot_general(a, b, (((1,), (1,)), ((), ())), preferred_element_type=F32)


def _flash_kernel(lam_ref, sg_ref, q_ref, kc_ref, vc_ref, *rest, tk, n_kx, lam_init):
    if n_kx:
        kx_ref, vx_ref, o_ref, s_even, s_odd = rest
    else:
        (o_ref,) = rest
    tq, dh = q_ref.shape
    q = q_ref[...]
    lane = lax.broadcasted_iota(jnp.int32, (tq, dh), 1)
    zero = jnp.zeros_like(q)
    qa = jnp.where(lane < dh // 2, q, zero)
    qb = jnp.where(lane >= dh // 2, q, zero)

    def v_ext(v):
        return jnp.concatenate([v, jnp.ones_like(v)], axis=1)

    def init(k, v):
        ve = v_ext(v)
        out = []
        for qm in (qa, qb):
            s = _dot_nt(qm, k)
            m = jnp.max(s, axis=-1, keepdims=True)
            p = jnp.exp2(s - m).astype(BF16)
            out += [m, jnp.dot(p, ve, preferred_element_type=F32)]
        return tuple(out)

    def scores(t, s_ref):
        k = kx_ref[t * tk:(t + 1) * tk, :]
        s_ref[0] = _dot_nt(qa, k)
        s_ref[1] = _dot_nt(qb, k)

    def update(t, s_ref, carry):
        ve = v_ext(vx_ref[t * tk:(t + 1) * tk, :])
        out = []
        for mp, (m, acc) in enumerate((carry[0:2], carry[2:4])):
            s = s_ref[mp]
            m_new = jnp.maximum(m, jnp.max(s, axis=-1, keepdims=True))
            p = jnp.exp2(s - m_new).astype(BF16)
            acc = jnp.exp2(m - m_new) * acc + jnp.dot(p, ve, preferred_element_type=F32)
            out += [m_new, acc]
        return tuple(out)

    if n_kx:
        bufs = (s_even, s_odd)
        scores(0, bufs[0])
    carry = init(kc_ref[...], vc_ref[...])
    if n_kx:
        for t in range(n_kx):
            if t + 1 < n_kx:
                scores(t + 1, bufs[(t + 1) % 2])
            carry = update(t, bufs[t % 2], carry)
    _, a1, _, a2 = carry
    lv = lam_ref[...]
    lam = (jnp.exp(jnp.sum(lv[0:1, :] * lv[1:2, :], axis=-1, keepdims=True))
           - jnp.exp(jnp.sum(lv[2:3, :] * lv[3:4, :], axis=-1, keepdims=True)) + lam_init)
    o = a1[:, :dh] / a1[:, dh:] - lam * (a2[:, :dh] / a2[:, dh:])
    ms = jnp.mean(o * o, axis=-1, keepdims=True)
    o = o * lax.rsqrt(ms + NORM_EPS) * sg_ref[...] * (1.0 - lam_init)
    o_ref[...] = o.astype(o_ref.dtype)


def _flash(q, k, v, lam_vecs, subln_g, out_init, *, n_batch, n_heads, q_len, q_off, kc_len, kc_off,
           kx_len, tq, tk, lam_init):
    dh = q.shape[1] // n_heads
    n_q = q_len // tq
    qo = q_off // tq
    kco = kc_off // kc_len
    n_kx = kx_len // tk if kx_len else 0
    in_specs = [
        pl.BlockSpec(lam_vecs.shape, lambda b, h, i: (0, 0)),
        pl.BlockSpec((1, dh), lambda b, h, i: (0, 0)),
        pl.BlockSpec((tq, dh), lambda b, h, i: (qo + b * n_q + i, h)),
        pl.BlockSpec((kc_len, dh), lambda b, h, i: (kco + b, h)),
        pl.BlockSpec((kc_len, dh), lambda b, h, i: (kco + b, h)),
    ]
    args = [lam_vecs, subln_g, q, k, v]
    if n_kx:
        in_specs += [pl.BlockSpec((kx_len, dh), lambda b, h, i: (b, h)),
                     pl.BlockSpec((kx_len, dh), lambda b, h, i: (b, h))]
        args += [k, v]
    kern = functools.partial(_flash_kernel, tk=tk, n_kx=n_kx, lam_init=lam_init)
    if out_init is not None:
        in_specs.insert(0, pl.BlockSpec(memory_space=pl.ANY))
        args.insert(0, out_init)
        kern = functools.partial(_flash_kernel_alias, tk=tk, n_kx=n_kx, lam_init=lam_init)
    return pl.pallas_call(
        kern,
        out_shape=jax.ShapeDtypeStruct(q.shape, q.dtype),
        grid=(n_batch, n_heads, n_q),
        in_specs=in_specs,
        out_specs=pl.BlockSpec((tq, dh), lambda b, h, i: (qo + b * n_q + i, h)),
        scratch_shapes=([pltpu.VMEM((2, tq, tk), F32)] * 2 if n_kx else []),
        input_output_aliases=({0: 0} if out_init is not None else {}),
        compiler_params=_cparams(("arbitrary", "arbitrary", "arbitrary"), V7X_VMEM_LIMIT),
        name="diff_attn_latent" if n_kx else "diff_attn_ctx",
    )(*args)


def _flash_kernel_alias(out_init_ref, *refs, tk, n_kx, lam_init):
    del out_init_ref
    _flash_kernel(*refs, tk=tk, n_kx=n_kx, lam_init=lam_init)


def _proj_residual_kernel(a_ref, w_ref, x_ref, mod_ref, o_ref):
    out = jnp.dot(a_ref[...], w_ref[...], preferred_element_type=F32)
    o_ref[...] = x_ref[...] + mod_ref[2:3, :] * out


def _proj_residual(a, w, xs, mod_l, *, rows, n_lat_rows, seq_len, ctx_row, tm):
    d = xs.shape[1]
    midx = _mod_index(tm, n_lat_rows, seq_len, ctx_row)
    return pl.pallas_call(
        _proj_residual_kernel,
        out_shape=jax.ShapeDtypeStruct(xs.shape, F32),
        grid=(rows // tm,),
        in_specs=[
            pl.BlockSpec((tm, d), lambda i: (i, 0)),
            pl.BlockSpec((d, d), lambda i: (0, 0)),
            pl.BlockSpec((tm, d), lambda i: (i, 0)),
            pl.BlockSpec((None, 6, d), lambda i: (midx(i), 0, 0)),
        ],
        out_specs=pl.BlockSpec((tm, d), lambda i: (i, 0)),
        input_output_aliases={2: 0},
        compiler_params=_cparams(("arbitrary",), V7X_VMEM_LIMIT),
        name="attn_out_proj",
    )(a, w, xs, mod_l)


def _rope_tables(n_tokens, width):
    rope_axis_dim = 32
    freqs = rope_axis_dim // 2
    rows = n_tokens // GRID_W
    row = jnp.repeat(jnp.arange(rows, dtype=F32), GRID_W)
    col = jnp.tile(jnp.arange(GRID_W, dtype=F32), rows)
    inv_freq = 1.0 / (ROPE_BASE ** (jnp.arange(freqs, dtype=F32) * 2.0 / rope_axis_dim))
    ang = jnp.stack([row[:, None] * inv_freq, col[:, None] * inv_freq], axis=1)
    ang = jnp.stack([ang, ang], axis=2).reshape(n_tokens, 4 * freqs)
    ang = jnp.tile(ang, (1, width // (4 * freqs)))
    return jnp.cos(ang), jnp.sin(ang)


def _dense_ffn_layer(xs, mod_l, g, wg, wu, wd, geo, rows):
    tm = geo["tm_ffn"]
    n_tiles = rows // tm
    te = jnp.zeros((n_tiles,), jnp.int32)
    tr = jnp.full((n_tiles,), tm, jnp.int32)
    midx = _mod_index(tm, geo["n_lat_rows"], geo["seq_len"], geo["ctx_row"])
    return _ffn(xs, te, tr, wg, wu, wd, rows=rows, tm=tm, tf=geo["tf"], mod_l=mod_l, g=g, midx=midx)


def _moe_layer(xs, mod_l, g, w_router, wg, wu, wd, geo, rows, final_g=None):
    d = xs.shape[1]
    n_exp = w_router.shape[1]
    tm = geo["tm_moe"]
    common = dict(rows=rows, n_lat_rows=geo["n_lat_rows"], seq_len=geo["seq_len"], ctx_row=geo["ctx_row"])
    wr = jnp.zeros((d, LANES), F32).at[:, :n_exp].set(w_router)
    wr_hi = wr.astype(BF16)
    wr = jnp.stack([wr_hi, (wr - wr_hi.astype(F32)).astype(BF16)])
    h, meta, gw, cnt = _router(xs, mod_l, g, wr, tm=geo["tm_router"], n_exp=n_exp, **common)
    counts = cnt[0, :n_exp].astype(jnp.int32)
    padded = ((counts + tm - 1) // tm) * tm
    ends = jnp.cumsum(padded)
    offs = ends - padded
    n_sorted_tiles = (TOP_K * rows) // tm + n_exp
    tile_start = jnp.arange(n_sorted_tiles, dtype=jnp.int32) * tm
    tile_expert = jnp.minimum(jnp.sum(tile_start[:, None] >= ends[None, :], axis=1), n_exp - 1).astype(jnp.int32)
    tile_rows = jnp.clip((offs + counts)[tile_expert] - tile_start, 0, tm).astype(jnp.int32)
    tile_rows = jnp.where(tile_start < ends[-1], tile_rows, 0)
    e1, e2, r1, r2 = meta[0], meta[1], meta[2], meta[3]
    dest = jnp.stack([offs[e1] + r1, offs[e2] + r2]).astype(jnp.int32)
    hs = _dispatch_rows(h, dest, n_sorted_tiles * tm)
    ys = _ffn(hs, tile_expert, tile_rows, wg, wu, wd, rows=n_sorted_tiles * tm, tm=tm, tf=geo["tf"])
    ypair = _gather_rows(ys, dest.reshape(TOP_K * rows))
    return _combine(xs, ypair, gw, mod_l, tm=geo["tm_row"], final_g=final_g, **common)


def kernel(x, c, ctx, c_ctx, w_mod, b_mod, norm_g, conv_w_in, conv_b_in, conv_w_dw, conv_b_dw,
           conv_ln_g, conv_ln_b, conv_w_out, conv_b_out, fnet_w, fnet_b, attn_w_qkv, attn_lambda,
           attn_subln_g, attn_w_o, ffn_w_gate, ffn_w_up, ffn_w_down, moe_w_router, moe_w_gate,
           moe_w_up, moe_w_down, final_g):
    b_, n, d = x.shape
    n_ctx = ctx.shape[1]
    depth = w_mod.shape[0]
    n_lat = b_ * n
    n_all = n_lat + b_ * n_ctx
    assert b_ < MOD_ROWS and d % LANES == 0 and n % GRID_W == 0 and n % n_ctx == 0
    assert d // DA_HEADS == LANES
    geo = dict(n_lat_rows=n_lat, seq_len=n, ctx_row=b_,
               tm_row=min(1024, n_ctx * b_, n), tm_dft=min(512, n), tm_ffn=min(1024, n_ctx * b_, n), tm_moe=512,
               tm_router=min(512, n_ctx * b_, n), tf=min(256, ffn_w_gate.shape[2]),
               tm_conv=n_ctx)

    xs = jnp.concatenate([x.reshape(n_lat, d), ctx.reshape(b_ * n_ctx, d)], axis=0)
    c_all = jnp.zeros((MOD_ROWS, d), F32).at[:b_].set(c).at[b_].set(c_ctx)
    mod = _modulation(c_all, w_mod, b_mod).reshape(depth, MOD_ROWS, 6, d)

    out = None
    for i in range(depth):
        need_ctx = i < depth - 1
        rows = n_all if need_ctx else n_lat
        kind = i % 3
        mod_l = mod[i]
        common = dict(rows=rows, n_lat_rows=n_lat, seq_len=n, ctx_row=b_)
        g0 = norm_g[i, 0].reshape(1, d)
        g1 = norm_g[i, 1].reshape(1, d)
        j = i // 3
        if kind == 0:
            u = _conv_in(xs, mod_l, g0, conv_w_in[j].astype(BF16), conv_b_in[j].reshape(1, 2 * d),
                         tm=geo["tm_row"], **common)
            xs = _conv_out(u, xs, mod_l, conv_w_dw[j], conv_b_dw[j].reshape(1, d),
                           conv_ln_g[j].reshape(1, d), conv_ln_b[j].reshape(1, d),
                           conv_w_out[j].astype(BF16), conv_b_out[j].reshape(1, d),
                           tm=geo["tm_conv"], **common)
        elif kind == 1:
            gd = d // F_GROUPS
            kk = jnp.arange(gd, dtype=jnp.int32)
            ang = ((kk[:, None] * kk[None, :]) % gd).astype(F32) * (2.0 * math.pi / gd)
            cs = (jnp.concatenate([jnp.cos(ang), jnp.sin(ang)], axis=1) / math.sqrt(gd)).astype(BF16)
            yc, ys = _fnet_a(xs, mod_l, g0, cs, tm=geo["tm_row"], **common)
            wf = fnet_w[j].astype(BF16)
            bf = fnet_b[j].reshape(1, d)
            ct, st = _dft_tables(n)
            xs = _fnet_b(ct, st, yc, ys, xs, mod_l, wf, bf, n_batch=b_, seq_len=n, row_off=0,
                         mod_ctx_row=None, tm=geo["tm_dft"])
            if need_ctx:
                ct, st = _dft_tables(n_ctx)
                xs = _fnet_b(ct, st, yc, ys, xs, mod_l, wf, bf, n_batch=b_, seq_len=n_ctx,
                             row_off=n_lat, mod_ctx_row=b_, tm=n_ctx)
        else:
            lam_init = 0.8 - 0.6 * math.exp(-0.3 * i)
            dh = d // DA_HEADS
            cos_t, sin_t = _rope_tables(n, LANES)
            q, k, v = _qkv(xs, mod_l, g0, attn_w_qkv[j].astype(BF16), cos_t, sin_t,
                           tm=geo["tm_row"], q_scale=(dh // 2) ** -0.5 * math.log2(math.e), **common)
            sg = attn_subln_g[j].reshape(1, dh)
            fl = dict(n_batch=b_, n_heads=DA_HEADS, kc_len=n_ctx, kc_off=n_lat, lam_init=lam_init)
            o = _flash(q, k, v, attn_lambda[j], sg, None, q_len=n, q_off=0, kx_len=n,
                       tq=min(512, n), tk=min(2048, n), **fl)
            if need_ctx:
                o = _flash(q, k, v, attn_lambda[j], sg, o, q_len=n_ctx, q_off=n_lat, kx_len=0,
                           tq=n_ctx, tk=n_ctx, **fl)
            xs = _proj_residual(o, attn_w_o[j].astype(BF16), xs, mod_l, tm=geo["tm_row"], **common)

        j = i // 2
        if i % 2 == 0:
            wg, wu, wd = (_cast_layer_bf16(w[:, None], j) for w in (ffn_w_gate, ffn_w_up, ffn_w_down))
            xs = _dense_ffn_layer(xs, mod_l, g1, wg, wu, wd, geo, rows)
        else:
            fg = final_g.reshape(1, d) if i == depth - 1 else None
            xs, w32 = lax.optimization_barrier((xs, (moe_w_gate, moe_w_up, moe_w_down)))
            wg, wu, wd = (_cast_layer_bf16(w, j) for w in w32)
            xs, wg, wu, wd = lax.optimization_barrier((xs, wg, wu, wd))
            res = _moe_layer(xs, mod_l, g1, moe_w_router[j], wg, wu, wd, geo, rows, final_g=fg)
            if fg is not None:
                out = res
            else:
                xs = res
    assert out is not None, "the final rms_norm is fused into the last layer's MoE combine (odd last layer index)"
    return out.reshape(b_, n, d)
```

```python
import functools
import math

import jax
import jax.numpy as jnp
from jax import lax
from jax.experimental import pallas as pl
from jax.experimental.pallas import tpu as pltpu
from jax.experimental.pallas import tpu_sc as plsc

F32 = jnp.float32
BF16 = jnp.bfloat16

NORM_EPS = 1e-6
LN_EPS = 1e-5
CONV_WIDTH = 31
CONV_PAD = (CONV_WIDTH - 1) // 2
CONV_HALO = 16
GRID_W = 64
F_GROUPS = 8
DA_HEADS = 8
ROPE_BASE = 10000.0
TOP_K = 2
LANES = 128
SUBLANES = 8
MOD_ROWS = 16
DFT_ROWS = 64
SC_CORES = 2
SC_SUBCORES = 16
SC_INDEX_WINDOW = 128
SC_ROW_BUFFER_BYTES = 256 * 1024

V7X_VMEM_LIMIT = 56 * 1024 * 1024


def _cparams(sem, vmem=None):
    return pltpu.CompilerParams(dimension_semantics=sem, vmem_limit_bytes=vmem)


def _norm_mod(x, g, scale, shift):
    ms = jnp.mean(x * x, axis=-1, keepdims=True)
    y = x * lax.rsqrt(ms + NORM_EPS)
    return (y * g) * (1.0 + scale) + shift


def _pack_bf16_pairs(h):
    half = h.shape[1] // 2
    hb = h.astype(BF16)
    hi = lax.bitcast_convert_type(hb[:, :half].astype(F32), jnp.uint32)
    lo = lax.bitcast_convert_type(hb[:, half:].astype(F32), jnp.uint32)
    return lax.bitcast_convert_type(hi | (lo >> 16), jnp.int32)


def _unpack_bf16_pairs(words):
    w = lax.bitcast_convert_type(words, jnp.uint32)
    hi = lax.bitcast_convert_type(w & jnp.uint32(0xFFFF0000), F32)
    lo = lax.bitcast_convert_type(w << 16, F32)
    return jnp.concatenate([hi, lo], axis=1).astype(BF16)


def _mod_index(tm, n_lat_rows, seq_len, ctx_row):
    n_lat_tiles = n_lat_rows // tm
    per_seq = seq_len // tm

    def f(i):
        return jnp.where(i < n_lat_tiles, i // per_seq, ctx_row)

    return f


def _mod_kernel(c_ref, w_ref, b_ref, o_ref):
    c = c_ref[...]
    sc = c * jax.nn.sigmoid(c)
    o_ref[...] = jnp.dot(sc, w_ref[...], precision=lax.Precision.HIGHEST,
                         preferred_element_type=F32) + b_ref[...]


def _modulation(c_all, w_mod, b_mod):
    depth, d, nd = w_mod.shape
    tn = 1024
    return pl.pallas_call(
        _mod_kernel,
        out_shape=jax.ShapeDtypeStruct((depth, MOD_ROWS, nd), F32),
        grid=(depth, nd // tn),
        in_specs=[
            pl.BlockSpec((MOD_ROWS, d), lambda l, j: (0, 0)),
            pl.BlockSpec((None, d, tn), lambda l, j: (l, 0, j)),
            pl.BlockSpec((None, 1, tn), lambda l, j: (l, 0, j)),
        ],
        out_specs=pl.BlockSpec((None, MOD_ROWS, tn), lambda l, j: (l, 0, j)),
        compiler_params=_cparams(("arbitrary", "arbitrary")),
        name="modulation",
    )(c_all, w_mod, b_mod.reshape(depth, 1, nd))


def _conv_in_kernel(x_ref, mod_ref, g_ref, w_ref, b_ref, u_ref):
    d = u_ref.shape[1]
    h = _norm_mod(x_ref[...], g_ref[...], mod_ref[1:2, :], mod_ref[0:1, :])
    y = jnp.dot(h.astype(BF16), w_ref[...], preferred_element_type=F32) + b_ref[...]
    u_ref[...] = (y[:, :d] * jax.nn.sigmoid(y[:, d:])).astype(u_ref.dtype)


def _conv_in(xs, mod_l, g, w_in, b_in, *, rows, n_lat_rows, seq_len, ctx_row, tm):
    d = xs.shape[1]
    midx = _mod_index(tm, n_lat_rows, seq_len, ctx_row)
    return pl.pallas_call(
        _conv_in_kernel,
        out_shape=jax.ShapeDtypeStruct((rows, d), BF16),
        grid=(rows // tm,),
        in_specs=[
            pl.BlockSpec((tm, d), lambda i: (i, 0)),
            pl.BlockSpec((None, 6, d), lambda i: (midx(i), 0, 0)),
            pl.BlockSpec((1, d), lambda i: (0, 0)),
            pl.BlockSpec((d, 2 * d), lambda i: (0, 0)),
            pl.BlockSpec((1, 2 * d), lambda i: (0, 0)),
        ],
        out_specs=pl.BlockSpec((tm, d), lambda i: (i, 0)),
        compiler_params=_cparams(("arbitrary",), V7X_VMEM_LIMIT),
        name="conv_in",
    )(xs, mod_l, g, w_in, b_in)


def _conv_out_kernel(u_ref, up_ref, un_ref, x_ref, mod_ref, wdw_ref, bdw_ref, lng_ref, lnb_ref,
                     wo_ref, bo_ref, o_ref, ubuf, shifted, cbuf, *, n_lat_tiles, tiles_per_seq):
    tm, d = x_ref.shape
    i = pl.program_id(0)
    is_ctx = i >= n_lat_tiles
    j = i % tiles_per_seq
    first = jnp.logical_or(is_ctx, j == 0)
    last = jnp.logical_or(is_ctx, j == tiles_per_seq - 1)
    ubuf[0:CONV_HALO, :] = jnp.where(first, 0.0, up_ref[...].astype(F32))
    ubuf[CONV_HALO:CONV_HALO + tm, :] = u_ref[...].astype(F32)
    ubuf[CONV_HALO + tm:, :] = jnp.where(last, 0.0, un_ref[...].astype(F32))

    span = ubuf.shape[0] - SUBLANES
    for s in range(1, SUBLANES):
        shifted[s - 1] = ubuf[s:s + span, :]

    for c in range(d // LANES):
        cols = slice(c * LANES, (c + 1) * LANES)
        acc = jnp.zeros((tm, LANES), F32)
        for k in range(CONV_WIDTH):
            r0 = CONV_HALO - CONV_PAD + k
            s, a0 = r0 % SUBLANES, r0 - r0 % SUBLANES
            win = ubuf[a0:a0 + tm, cols] if s == 0 else shifted[s - 1, a0:a0 + tm, cols]
            acc = acc + wdw_ref[k:k + 1, cols] * win
        cbuf[:, cols] = acc + bdw_ref[:, cols]
    v = cbuf[...]
    mu = jnp.mean(v, axis=-1, keepdims=True)
    vc = v - mu
    var = jnp.mean(vc * vc, axis=-1, keepdims=True)
    y = vc * lax.rsqrt(var + LN_EPS) * lng_ref[...] + lnb_ref[...]
    y = y * jax.nn.sigmoid(y)
    out = jnp.dot(y.astype(BF16), wo_ref[...], preferred_element_type=F32) + bo_ref[...]
    o_ref[...] = x_ref[...] + mod_ref[2:3, :] * out


def _conv_out(u, xs, mod_l, w_dw, b_dw, ln_g, ln_b, w_out, b_out, *, rows, n_lat_rows, seq_len,
              ctx_row, tm):
    d = xs.shape[1]
    n_tiles = rows // tm
    n_lat_tiles = n_lat_rows // tm
    hpt = tm // CONV_HALO
    n_halo = u.shape[0] // CONV_HALO
    midx = _mod_index(tm, n_lat_rows, seq_len, ctx_row)
    kern = functools.partial(_conv_out_kernel, n_lat_tiles=n_lat_tiles, tiles_per_seq=seq_len // tm)
    return pl.pallas_call(
        kern,
        out_shape=jax.ShapeDtypeStruct(xs.shape, F32),
        grid=(n_tiles,),
        in_specs=[
            pl.BlockSpec((tm, d), lambda i: (i, 0)),
            pl.BlockSpec((CONV_HALO, d), lambda i: (jnp.maximum(i * hpt - 1, 0), 0)),
            pl.BlockSpec((CONV_HALO, d), lambda i: (jnp.minimum((i + 1) * hpt, n_halo - 1), 0)),
            pl.BlockSpec((tm, d), lambda i: (i, 0)),
            pl.BlockSpec((None, 6, d), lambda i: (midx(i), 0, 0)),
            pl.BlockSpec((CONV_WIDTH, d), lambda i: (0, 0)),
            pl.BlockSpec((1, d), lambda i: (0, 0)),
            pl.BlockSpec((1, d), lambda i: (0, 0)),
            pl.BlockSpec((1, d), lambda i: (0, 0)),
            pl.BlockSpec((d, d), lambda i: (0, 0)),
            pl.BlockSpec((1, d), lambda i: (0, 0)),
        ],
        out_specs=pl.BlockSpec((tm, d), lambda i: (i, 0)),
        scratch_shapes=[pltpu.VMEM((tm + 2 * CONV_HALO, d), F32),
                        pltpu.VMEM((SUBLANES - 1, tm + 2 * CONV_HALO - SUBLANES, d), F32),
                        pltpu.VMEM((tm, d), F32)],
        input_output_aliases={3: 0},
        compiler_params=_cparams(("arbitrary",), V7X_VMEM_LIMIT),
        name="conv_out",
    )(u, u, u, xs, mod_l, w_dw, b_dw, ln_g, ln_b, w_out, b_out)


def _cast_kernel(w_ref, o_ref):
    o_ref[...] = w_ref[...].astype(o_ref.dtype)


def _cast_layer_bf16(w, layer):
    _, n_e, a, b = w.shape
    ta = min(a, 512)
    return pl.pallas_call(
        _cast_kernel,
        out_shape=jax.ShapeDtypeStruct((n_e, a, b), BF16),
        grid=(n_e, a // ta),
        in_specs=[pl.BlockSpec((None, None, ta, b), lambda e, i: (layer, e, i, 0))],
        out_specs=pl.BlockSpec((None, ta, b), lambda e, i: (e, i, 0)),
        compiler_params=_cparams(("arbitrary", "arbitrary"), V7X_VMEM_LIMIT),
        name="cast_weights",
    )(w)


def _ffn_kernel(te_ref, rv_ref, *refs, fuse_norm, tf):
    if fuse_norm:
        x_ref, mod_ref, g_ref, wg_ref, wu_ref, wd_ref, o_ref = refs
    else:
        x_ref, wg_ref, wu_ref, wd_ref, o_ref = refs
    del te_ref
    n_rows = rv_ref[pl.program_id(0)]
    valid = n_rows > 0

    @pl.when(valid)
    def _():
        if fuse_norm:
            h = _norm_mod(x_ref[...], g_ref[...], mod_ref[4:5, :], mod_ref[3:4, :])
        else:
            row = lax.broadcasted_iota(jnp.int32, (x_ref.shape[0], 1), 0)
            h = _unpack_bf16_pairs(jnp.where(row < n_rows, x_ref[...], 0))
        h = h.astype(BF16)
        acc = None
        for c in range(wg_ref.shape[1] // tf):
            cols = slice(c * tf, (c + 1) * tf)
            gt = jnp.dot(h, wg_ref[:, cols], preferred_element_type=F32)
            up = jnp.dot(h, wu_ref[:, cols], preferred_element_type=F32)
            a = (gt * jax.nn.sigmoid(gt) * up).astype(BF16)
            part = jnp.dot(a, wd_ref[cols, :], preferred_element_type=F32)
            acc = part if acc is None else acc + part
        if fuse_norm:
            o_ref[...] = x_ref[...] + mod_ref[5:6, :] * acc
        else:
            o_ref[...] = acc

    @pl.when(jnp.logical_not(valid))
    def _():
        o_ref[...] = jnp.zeros_like(o_ref)


def _ffn(x, tile_expert, tile_rows, wg, wu, wd, *, rows, tm, tf, mod_l=None, g=None, midx=None):
    d = wg.shape[1]
    f_dim = wg.shape[2]
    fuse_norm = mod_l is not None
    resident = pl.Buffered(1)
    in_specs = [pl.BlockSpec((tm, x.shape[1]), lambda j, te, nv: (j, 0))]
    args = [x]
    if fuse_norm:
        in_specs += [pl.BlockSpec((None, 6, d), lambda j, te, nv: (midx(j), 0, 0)),
                     pl.BlockSpec((1, d), lambda j, te, nv: (0, 0))]
        args += [mod_l, g]
    in_specs += [
        pl.BlockSpec((None, d, f_dim), lambda j, te, nv: (te[j], 0, 0), pipeline_mode=resident),
        pl.BlockSpec((None, d, f_dim), lambda j, te, nv: (te[j], 0, 0), pipeline_mode=resident),
        pl.BlockSpec((None, f_dim, d), lambda j, te, nv: (te[j], 0, 0), pipeline_mode=resident),
    ]
    args += [wg, wu, wd]
    kern = functools.partial(_ffn_kernel, fuse_norm=fuse_norm, tf=tf)
    return pl.pallas_call(
        kern,
        out_shape=jax.ShapeDtypeStruct((x.shape[0], d), F32),
        grid_spec=pltpu.PrefetchScalarGridSpec(
            num_scalar_prefetch=2,
            grid=(rows // tm,),
            in_specs=in_specs,
            out_specs=pl.BlockSpec((tm, d), lambda j, te, nv: (j, 0)),
        ),
        input_output_aliases=({2: 0} if fuse_norm else {}),
        compiler_params=_cparams(("arbitrary",), V7X_VMEM_LIMIT),
        name="ffn_dense" if fuse_norm else "ffn_grouped",
    )(tile_expert, tile_rows, *args)


def _router_kernel(x_ref, mod_ref, g_ref, wr_ref, h_ref, meta_ref, gw_ref, cnt_ref, carry, *, n_exp):
    tm = x_ref.shape[0]
    i = pl.program_id(0)

    @pl.when(i == 0)
    def _():
        carry[...] = jnp.zeros_like(carry)

    h = _norm_mod(x_ref[...], g_ref[...], mod_ref[4:5, :], mod_ref[3:4, :])
    h_ref[...] = _pack_bf16_pairs(h)
    h_hi = h.astype(BF16)
    h_lo = (h - h_hi.astype(F32)).astype(BF16)
    logits = (jnp.dot(h_hi, wr_ref[0], preferred_element_type=F32)
              + (jnp.dot(h_hi, wr_ref[1], preferred_element_type=F32)
                 + jnp.dot(h_lo, wr_ref[0], preferred_element_type=F32)))
    lane = lax.broadcasted_iota(jnp.int32, (tm, LANES), 1).astype(F32)
    neg = jnp.float32(-jnp.inf)
    lg = jnp.where(lane < n_exp, logits, neg)
    m1 = jnp.max(lg, axis=-1, keepdims=True)
    i1 = jnp.min(jnp.where(lg == m1, lane, float(LANES)), axis=-1, keepdims=True)
    lg2 = jnp.where(lane == i1, neg, lg)
    m2 = jnp.max(lg2, axis=-1, keepdims=True)
    i2 = jnp.min(jnp.where(lg2 == m2, lane, float(LANES)), axis=-1, keepdims=True)
    e2 = jnp.exp(m2 - m1)
    w1 = 1.0 / (1.0 + e2)
    w2 = e2 * w1
    sel1 = lane == i1
    sel2 = lane == i2
    onehot = jnp.where(jnp.logical_or(sel1, sel2), 1.0, 0.0)
    rr = lax.broadcasted_iota(jnp.int32, (tm, tm), 0)
    cc = lax.broadcasted_iota(jnp.int32, (tm, tm), 1)
    tri = jnp.where(rr > cc, 1.0, 0.0).astype(BF16)
    cum = jnp.dot(tri, onehot.astype(BF16), preferred_element_type=F32) + carry[...]
    r1 = jnp.sum(jnp.where(sel1, cum, 0.0), axis=-1, keepdims=True)
    r2 = jnp.sum(jnp.where(sel2, cum, 0.0), axis=-1, keepdims=True)
    carry[...] += jnp.sum(onehot, axis=0, keepdims=True)
    meta = jnp.where(lane == 0, i1, jnp.where(lane == 1, i2, jnp.where(lane == 2, r1,
                     jnp.where(lane == 3, r2, 0.0))))
    meta_ref[...] = meta.T[:SUBLANES, :].astype(jnp.int32)
    gw_ref[...] = jnp.where(lane == 0, w1, jnp.where(lane == 1, w2, 0.0))
    cnt_ref[...] = jnp.broadcast_to(carry[...], cnt_ref.shape)


def _router(xs, mod_l, g, w_router_pad, *, rows, n_lat_rows, seq_len, ctx_row, tm, n_exp):
    d = xs.shape[1]
    midx = _mod_index(tm, n_lat_rows, seq_len, ctx_row)
    kern = functools.partial(_router_kernel, n_exp=n_exp)
    return pl.pallas_call(
        kern,
        out_shape=[jax.ShapeDtypeStruct((rows, d // 2), jnp.int32),
                   jax.ShapeDtypeStruct((SUBLANES, rows), jnp.int32),
                   jax.ShapeDtypeStruct((rows, LANES), F32),
                   jax.ShapeDtypeStruct((8, LANES), F32)],
        grid=(rows // tm,),
        in_specs=[
            pl.BlockSpec((tm, d), lambda i: (i, 0)),
            pl.BlockSpec((None, 6, d), lambda i: (midx(i), 0, 0)),
            pl.BlockSpec((1, d), lambda i: (0, 0)),
            pl.BlockSpec((2, d, LANES), lambda i: (0, 0, 0)),
        ],
        out_specs=[pl.BlockSpec((tm, d // 2), lambda i: (i, 0)),
                   pl.BlockSpec((SUBLANES, tm), lambda i: (0, i)),
                   pl.BlockSpec((tm, LANES), lambda i: (i, 0)),
                   pl.BlockSpec((8, LANES), lambda i: (0, 0))],
        scratch_shapes=[pltpu.VMEM((1, LANES), F32)],
        compiler_params=_cparams(("arbitrary",), V7X_VMEM_LIMIT),
        name="router",
    )(xs, mod_l, g, w_router_pad)


def _sc_rows_per_copy(d, dtype):
    return min(SC_INDEX_WINDOW, SC_ROW_BUFFER_BYTES // (d * jnp.dtype(dtype).itemsize))


def _dispatch_rows(src, dest, n_out):
    t, d = src.shape
    n_slots = dest.shape[0]
    assert t % SC_INDEX_WINDOW == 0
    n_win = t // SC_INDEX_WINDOW
    n_workers = SC_CORES * SC_SUBCORES
    n_rows = _sc_rows_per_copy(d, src.dtype)
    mesh = plsc.VectorSubcoreMesh(core_axis_name="core", subcore_axis_name="subcore")

    @pl.kernel(out_type=jax.ShapeDtypeStruct((n_out, d), src.dtype), mesh=mesh,
               scratch_types=[pltpu.VMEM((n_slots, SC_INDEX_WINDOW), jnp.int32),
                              pltpu.VMEM((n_rows, d), src.dtype)])
    def dispatch(src_hbm, idx_hbm, out_hbm, idx_v, buf):
        wid = lax.axis_index("core") * SC_SUBCORES + lax.axis_index("subcore")

        @pl.loop(0, (n_win - wid + n_workers - 1) // n_workers)
        def _(b):
            base = (b * n_workers + wid) * SC_INDEX_WINDOW
            pltpu.sync_copy(idx_hbm.at[:, pl.ds(base, SC_INDEX_WINDOW)], idx_v)
            for k in range(SC_INDEX_WINDOW // n_rows):
                rows = pl.ds(k * n_rows, n_rows)
                pltpu.sync_copy(src_hbm.at[pl.ds(base + k * n_rows, n_rows)], buf)
                for s in range(n_slots):
                    pltpu.sync_copy(buf, out_hbm.at[idx_v.at[s, rows]])

    return dispatch(src, dest)


def _gather_rows(src, idx):
    n = idx.shape[0]
    d = src.shape[1]
    n_workers = SC_CORES * SC_SUBCORES
    per = n // n_workers
    assert n % (n_workers * SC_INDEX_WINDOW) == 0, (n, n_workers, SC_INDEX_WINDOW)
    n_rows = _sc_rows_per_copy(d, src.dtype) // 2
    n_k = SC_INDEX_WINDOW // n_rows
    mesh = plsc.VectorSubcoreMesh(core_axis_name="core", subcore_axis_name="subcore")

    @pl.kernel(out_type=jax.ShapeDtypeStruct((n, d), src.dtype), mesh=mesh,
               scratch_types=[pltpu.VMEM((1, SC_INDEX_WINDOW), jnp.int32),
                              pltpu.VMEM((2, n_rows, d), src.dtype),
                              pltpu.SemaphoreType.DMA((2,)), pltpu.SemaphoreType.DMA((2,))])
    def gather(src_hbm, idx_hbm, out_hbm, idx_v, buf, gsem, wsem):
        wid = lax.axis_index("core") * SC_SUBCORES + lax.axis_index("subcore")

        @pl.loop(0, per // SC_INDEX_WINDOW)
        def _(b):
            base = wid * per + b * SC_INDEX_WINDOW
            pltpu.sync_copy(idx_hbm.at[:, pl.ds(base, SC_INDEX_WINDOW)], idx_v)

            def fetch(k):
                return pltpu.make_async_copy(src_hbm.at[idx_v.at[0, pl.ds(k * n_rows, n_rows)]],
                                             buf.at[k % 2], gsem.at[k % 2])

            def flush(k):
                return pltpu.make_async_copy(buf.at[k % 2], out_hbm.at[pl.ds(base + k * n_rows, n_rows)],
                                             wsem.at[k % 2])

            fetch(0).start()
            for k in range(n_k):
                fetch(k).wait()
                if k >= 1:
                    flush(k - 1).wait()
                if k + 1 < n_k:
                    fetch(k + 1).start()
                flush(k).start()
            flush(n_k - 1).wait()

    return gather(src, idx.reshape(1, n))


def _combine_kernel(x_ref, y1_ref, y2_ref, gw_ref, mod_ref, *rest, final):
    if final:
        fg_ref, o_ref = rest
    else:
        (o_ref,) = rest
    gw = gw_ref[...]
    y = gw[:, 0:1] * y1_ref[...] + gw[:, 1:2] * y2_ref[...]
    xn = x_ref[...] + mod_ref[5:6, :] * y
    if final:
        ms = jnp.mean(xn * xn, axis=-1, keepdims=True)
        xn = xn * lax.rsqrt(ms + NORM_EPS) * fg_ref[...]
    o_ref[...] = xn


def _combine(xs, ypair, gw, mod_l, *, rows, n_lat_rows, seq_len, ctx_row, tm, final_g=None):
    d = xs.shape[1]
    n_tiles = rows // tm
    midx = _mod_index(tm, n_lat_rows, seq_len, ctx_row)
    final = final_g is not None
    in_specs = [
        pl.BlockSpec((tm, d), lambda i: (i, 0)),
        pl.BlockSpec((tm, d), lambda i: (i, 0)),
        pl.BlockSpec((tm, d), lambda i: (i + n_tiles, 0)),
        pl.BlockSpec((tm, LANES), lambda i: (i, 0)),
        pl.BlockSpec((None, 6, d), lambda i: (midx(i), 0, 0)),
    ]
    args = [xs, ypair, ypair, gw, mod_l]
    if final:
        in_specs.append(pl.BlockSpec((1, d), lambda i: (0, 0)))
        args.append(final_g)
    out_rows = rows if final else xs.shape[0]
    return pl.pallas_call(
        functools.partial(_combine_kernel, final=final),
        out_shape=jax.ShapeDtypeStruct((out_rows, d), F32),
        grid=(n_tiles,),
        in_specs=in_specs,
        out_specs=pl.BlockSpec((tm, d), lambda i: (i, 0)),
        input_output_aliases=({} if final else {0: 0}),
        compiler_params=_cparams(("arbitrary",), V7X_VMEM_LIMIT),
        name="moe_combine",
    )(*args)


def _fnet_a_kernel(x_ref, mod_ref, g_ref, cs_ref, yc_ref, ys_ref, *, n_groups):
    h = _norm_mod(x_ref[...], g_ref[...], mod_ref[1:2, :], mod_ref[0:1, :]).astype(BF16)
    gd = cs_ref.shape[0]
    for gi in range(n_groups):
        y = jnp.dot(h[:, gi * gd:(gi + 1) * gd], cs_ref[...], preferred_element_type=F32)
        yc_ref[:, gi * gd:(gi + 1) * gd] = y[:, :gd].astype(yc_ref.dtype)
        ys_ref[:, gi * gd:(gi + 1) * gd] = y[:, gd:].astype(ys_ref.dtype)


def _fnet_a(xs, mod_l, g, cs, *, rows, n_lat_rows, seq_len, ctx_row, tm):
    d = xs.shape[1]
    gd = cs.shape[0]
    midx = _mod_index(tm, n_lat_rows, seq_len, ctx_row)
    return pl.pallas_call(
        functools.partial(_fnet_a_kernel, n_groups=d // gd),
        out_shape=[jax.ShapeDtypeStruct((rows, d), BF16)] * 2,
        grid=(rows // tm,),
        in_specs=[
            pl.BlockSpec((tm, d), lambda i: (i, 0)),
            pl.BlockSpec((None, 6, d), lambda i: (midx(i), 0, 0)),
            pl.BlockSpec((1, d), lambda i: (0, 0)),
            pl.BlockSpec((gd, 2 * gd), lambda i: (0, 0)),
        ],
        out_specs=[pl.BlockSpec((tm, d), lambda i: (i, 0))] * 2,
        compiler_params=_cparams(("arbitrary",), V7X_VMEM_LIMIT),
        name="fnet_group_dft",
    )(xs, mod_l, g, cs)


def _dft_table_kernel(ac_ref, as_ref, bc_ref, bs_ref, c_ref, sn_ref):
    ac, as_ = ac_ref[...], as_ref[...]
    bc, bs = bc_ref[...], bs_ref[...]
    c_ref[...] = (bc * ac - bs * as_).astype(c_ref.dtype)
    sn_ref[...] = (-(bs * ac + bc * as_)).astype(sn_ref.dtype)


def _dft_tables(n):
    r = DFT_ROWS
    k = jnp.arange(n, dtype=jnp.int32)[None, :]
    j1 = jnp.arange(r, dtype=jnp.int32)[:, None]
    j0 = (jnp.arange(n // r, dtype=jnp.int32) * r)[:, None]
    ang1 = ((j1 * k) % n).astype(F32) * (2.0 * math.pi / n)
    ang0 = ((j0 * k) % n).astype(F32) * (2.0 * math.pi / n)
    scale = 1.0 / math.sqrt(n)
    ac, as_ = jnp.cos(ang1), jnp.sin(ang1)
    bc = (jnp.cos(ang0) * scale).reshape(n // r, 1, n)
    bs = (jnp.sin(ang0) * scale).reshape(n // r, 1, n)
    return pl.pallas_call(
        _dft_table_kernel,
        out_shape=[jax.ShapeDtypeStruct((n, n), BF16)] * 2,
        grid=(n // r,),
        in_specs=[
            pl.BlockSpec((r, n), lambda i: (0, 0)),
            pl.BlockSpec((r, n), lambda i: (0, 0)),
            pl.BlockSpec((None, 1, n), lambda i: (i, 0, 0)),
            pl.BlockSpec((None, 1, n), lambda i: (i, 0, 0)),
        ],
        out_specs=[pl.BlockSpec((r, n), lambda i: (i, 0))] * 2,
        compiler_params=_cparams(("arbitrary",)),
        name="dft_tables",
    )(ac, as_, bc, bs)


def _fnet_b_kernel(c_ref, sn_ref, yc_ref, ys_ref, x_ref, mod_ref, wf_ref, bf_ref, o_ref):
    z = (jnp.dot(c_ref[...], yc_ref[...], preferred_element_type=F32)
         + jnp.dot(sn_ref[...], ys_ref[...], preferred_element_type=F32))
    out = jnp.dot(z.astype(BF16), wf_ref[...], preferred_element_type=F32) + bf_ref[...]
    o_ref[...] = x_ref[...] + mod_ref[2:3, :] * out


def _fnet_b(ctab, stab, yc, ys, xs, mod_l, wf, bf, *, n_batch, seq_len, row_off, mod_ctx_row, tm):
    d = xs.shape[1]
    n_i = seq_len // tm
    off_m = row_off // tm
    off_b = row_off // seq_len
    resident = pl.Buffered(1)

    def mrow(b):
        return b if mod_ctx_row is None else mod_ctx_row

    return pl.pallas_call(
        _fnet_b_kernel,
        out_shape=jax.ShapeDtypeStruct(xs.shape, F32),
        grid=(n_batch, n_i),
        in_specs=[
            pl.BlockSpec((tm, seq_len), lambda b, i: (i, 0)),
            pl.BlockSpec((tm, seq_len), lambda b, i: (i, 0)),
            pl.BlockSpec((seq_len, d), lambda b, i: (off_b + b, 0), pipeline_mode=resident),
            pl.BlockSpec((seq_len, d), lambda b, i: (off_b + b, 0), pipeline_mode=resident),
            pl.BlockSpec((tm, d), lambda b, i: (off_m + b * n_i + i, 0)),
            pl.BlockSpec((None, 6, d), lambda b, i: (mrow(b), 0, 0)),
            pl.BlockSpec((d, d), lambda b, i: (0, 0)),
            pl.BlockSpec((1, d), lambda b, i: (0, 0)),
        ],
        out_specs=pl.BlockSpec((tm, d), lambda b, i: (off_m + b * n_i + i, 0)),
        input_output_aliases={4: 0},
        compiler_params=_cparams(("arbitrary", "arbitrary"), V7X_VMEM_LIMIT),
        name="fnet_seq_dft",
    )(ctab, stab, yc, ys, xs, mod_l, wf, bf)


def _qkv_kernel(x_ref, mod_ref, g_ref, w_ref, cos_ref, sin_ref, q_ref, k_ref, v_ref, *,
                n_lat_tiles, q_scale):
    tm, d = x_ref.shape
    i = pl.program_id(0)
    is_ctx = i >= n_lat_tiles
    h = _norm_mod(x_ref[...], g_ref[...], mod_ref[1:2, :], mod_ref[0:1, :])
    y = jnp.dot(h.astype(BF16), w_ref[...], preferred_element_type=F32)
    tw = cos_ref.shape[1]
    cos = jnp.where(is_ctx, 1.0, cos_ref[...])
    sin = jnp.where(is_ctx, 0.0, sin_ref[...])
    lane = lax.broadcasted_iota(jnp.int32, (tm, tw), 1)
    half = tw // 8
    lo = (lane % (2 * half)) < half

    def rope(t):
        rot = jnp.where(lo, -pltpu.roll(t, tw - half, 1), pltpu.roll(t, half, 1))
        return t * cos + rot * sin

    for hd in range(d // tw):
        cols = slice(hd * tw, (hd + 1) * tw)
        q_ref[:, cols] = (rope(y[:, hd * tw:(hd + 1) * tw]) * q_scale).astype(q_ref.dtype)
        k_ref[:, cols] = rope(y[:, d + hd * tw:d + (hd + 1) * tw]).astype(k_ref.dtype)
    v_ref[...] = y[:, 2 * d:].astype(v_ref.dtype)


def _qkv(xs, mod_l, g, w_qkv, cos_t, sin_t, *, rows, n_lat_rows, seq_len, ctx_row, tm, q_scale):
    d = xs.shape[1]
    midx = _mod_index(tm, n_lat_rows, seq_len, ctx_row)
    per_seq = seq_len // tm
    n_lat_tiles = n_lat_rows // tm
    tw = cos_t.shape[1]

    def pos(i):
        return jnp.where(i < n_lat_tiles, i % per_seq, 0)

    kern = functools.partial(_qkv_kernel, n_lat_tiles=n_lat_tiles, q_scale=q_scale)
    return pl.pallas_call(
        kern,
        out_shape=[jax.ShapeDtypeStruct((rows, d), BF16)] * 3,
        grid=(rows // tm,),
        in_specs=[
            pl.BlockSpec((tm, d), lambda i: (i, 0)),
            pl.BlockSpec((None, 6, d), lambda i: (midx(i), 0, 0)),
            pl.BlockSpec((1, d), lambda i: (0, 0)),
            pl.BlockSpec((d, 3 * d), lambda i: (0, 0)),
            pl.BlockSpec((tm, tw), lambda i: (pos(i), 0)),
            pl.BlockSpec((tm, tw), lambda i: (pos(i), 0)),
        ],
        out_specs=[pl.BlockSpec((tm, d), lambda i: (i, 0))] * 3,
        compiler_params=_cparams(("arbitrary",), V7X_VMEM_LIMIT),
        name="attn_qkv",
    )(xs, mod_l, g, w_qkv, cos_t, sin_t)


def _dot_nt(a, b):
    return lax.dot_general(a, b, (((1,), (1,)), ((), ())), preferred_element_type=F32)


def _flash_kernel(lam_ref, sg_ref, q_ref, kc_ref, vc_ref, *rest, tk, n_kx, lam_init):
    if n_kx:
        kx_ref, vx_ref, o_ref, s_even, s_odd = rest
    else:
        (o_ref,) = rest
    tq, dh = q_ref.shape
    q = q_ref[...]
    lane = lax.broadcasted_iota(jnp.int32, (tq, dh), 1)
    zero = jnp.zeros_like(q)
    qa = jnp.where(lane < dh // 2, q, zero)
    qb = jnp.where(lane >= dh // 2, q, zero)

    def v_ext(v):
        return jnp.concatenate([v, jnp.ones_like(v)], axis=1)

    def init(k, v):
        ve = v_ext(v)
        out = []
        for qm in (qa, qb):
            s = _dot_nt(qm, k)
            m = jnp.max(s, axis=-1, keepdims=True)
            p = jnp.exp2(s - m).astype(BF16)
            out += [m, jnp.dot(p, ve, preferred_element_type=F32)]
        return tuple(out)

    def scores(t, s_ref):
        k = kx_ref[t * tk:(t + 1) * tk, :]
        s_ref[0] = _dot_nt(qa, k)
        s_ref[1] = _dot_nt(qb, k)

    def update(t, s_ref, carry):
        ve = v_ext(vx_ref[t * tk:(t + 1) * tk, :])
        out = []
        for mp, (m, acc) in enumerate((carry[0:2], carry[2:4])):
            s = s_ref[mp]
            m_new = jnp.maximum(m, jnp.max(s, axis=-1, keepdims=True))
            p = jnp.exp2(s - m_new).astype(BF16)
            acc = jnp.exp2(m - m_new) * acc + jnp.dot(p, ve, preferred_element_type=F32)
            out += [m_new, acc]
        return tuple(out)

    if n_kx:
        bufs = (s_even, s_odd)
        scores(0, bufs[0])
    carry = init(kc_ref[...], vc_ref[...])
    if n_kx:
        for t in range(n_kx):
            if t + 1 < n_kx:
                scores(t + 1, bufs[(t + 1) % 2])
            carry = update(t, bufs[t % 2], carry)
    _, a1, _, a2 = carry
    lv = lam_ref[...]
    lam = (jnp.exp(jnp.sum(lv[0:1, :] * lv[1:2, :], axis=-1, keepdims=True))
           - jnp.exp(jnp.sum(lv[2:3, :] * lv[3:4, :], axis=-1, keepdims=True)) + lam_init)
    o = a1[:, :dh] / a1[:, dh:] - lam * (a2[:, :dh] / a2[:, dh:])
    ms = jnp.mean(o * o, axis=-1, keepdims=True)
    o = o * lax.rsqrt(ms + NORM_EPS) * sg_ref[...] * (1.0 - lam_init)
    o_ref[...] = o.astype(o_ref.dtype)


def _flash(q, k, v, lam_vecs, subln_g, out_init, *, n_batch, n_heads, q_len, q_off, kc_len, kc_off,
           kx_len, tq, tk, lam_init):
    dh = q.shape[1] // n_heads
    n_q = q_len // tq
    qo = q_off // tq
    kco = kc_off // kc_len
    n_kx = kx_len // tk if kx_len else 0
    in_specs = [
        pl.BlockSpec(lam_vecs.shape, lambda b, h, i: (0, 0)),
        pl.BlockSpec((1, dh), lambda b, h, i: (0, 0)),
        pl.BlockSpec((tq, dh), lambda b, h, i: (qo + b * n_q + i, h)),
        pl.BlockSpec((kc_len, dh), lambda b, h, i: (kco + b, h)),
        pl.BlockSpec((kc_len, dh), lambda b, h, i: (kco + b, h)),
    ]
    args = [lam_vecs, subln_g, q, k, v]
    if n_kx:
        in_specs += [pl.BlockSpec((kx_len, dh), lambda b, h, i: (b, h)),
                     pl.BlockSpec((kx_len, dh), lambda b, h, i: (b, h))]
        args += [k, v]
    kern = functools.partial(_flash_kernel, tk=tk, n_kx=n_kx, lam_init=lam_init)
    if out_init is not None:
        in_specs.insert(0, pl.BlockSpec(memory_space=pl.ANY))
        args.insert(0, out_init)
        kern = functools.partial(_flash_kernel_alias, tk=tk, n_kx=n_kx, lam_init=lam_init)
    return pl.pallas_call(
        kern,
        out_shape=jax.ShapeDtypeStruct(q.shape, q.dtype),
        grid=(n_batch, n_heads, n_q),
        in_specs=in_specs,
        out_specs=pl.BlockSpec((tq, dh), lambda b, h, i: (qo + b * n_q + i, h)),
        scratch_shapes=([pltpu.VMEM((2, tq, tk), F32)] * 2 if n_kx else []),
        input_output_aliases=({0: 0} if out_init is not None else {}),
        compiler_params=_cparams(("arbitrary", "arbitrary", "arbitrary"), V7X_VMEM_LIMIT),
        name="diff_attn_latent" if n_kx else "diff_attn_ctx",
    )(*args)


def _flash_kernel_alias(out_init_ref, *refs, tk, n_kx, lam_init):
    del out_init_ref
    _flash_kernel(*refs, tk=tk, n_kx=n_kx, lam_init=lam_init)


def _proj_residual_kernel(a_ref, w_ref, x_ref, mod_ref, o_ref):
    out = jnp.dot(a_ref[...], w_ref[...], preferred_element_type=F32)
    o_ref[...] = x_ref[...] + mod_ref[2:3, :] * out


def _proj_residual(a, w, xs, mod_l, *, rows, n_lat_rows, seq_len, ctx_row, tm):
    d = xs.shape[1]
    midx = _mod_index(tm, n_lat_rows, seq_len, ctx_row)
    return pl.pallas_call(
        _proj_residual_kernel,
        out_shape=jax.ShapeDtypeStruct(xs.shape, F32),
        grid=(rows // tm,),
        in_specs=[
            pl.BlockSpec((tm, d), lambda i: (i, 0)),
            pl.BlockSpec((d, d), lambda i: (0, 0)),
            pl.BlockSpec((tm, d), lambda i: (i, 0)),
            pl.BlockSpec((None, 6, d), lambda i: (midx(i), 0, 0)),
        ],
        out_specs=pl.BlockSpec((tm, d), lambda i: (i, 0)),
        input_output_aliases={2: 0},
        compiler_params=_cparams(("arbitrary",), V7X_VMEM_LIMIT),
        name="attn_out_proj",
    )(a, w, xs, mod_l)


def _rope_tables(n_tokens, width):
    rope_axis_dim = 32
    freqs = rope_axis_dim // 2
    rows = n_tokens // GRID_W
    row = jnp.repeat(jnp.arange(rows, dtype=F32), GRID_W)
    col = jnp.tile(jnp.arange(GRID_W, dtype=F32), rows)
    inv_freq = 1.0 / (ROPE_BASE ** (jnp.arange(freqs, dtype=F32) * 2.0 / rope_axis_dim))
    ang = jnp.stack([row[:, None] * inv_freq, col[:, None] * inv_freq], axis=1)
    ang = jnp.stack([ang, ang], axis=2).reshape(n_tokens, 4 * freqs)
    ang = jnp.tile(ang, (1, width // (4 * freqs)))
    return jnp.cos(ang), jnp.sin(ang)


def _dense_ffn_layer(xs, mod_l, g, wg, wu, wd, geo, rows):
    tm = geo["tm_ffn"]
    n_tiles = rows // tm
    te = jnp.zeros((n_tiles,), jnp.int32)
    tr = jnp.full((n_tiles,), tm, jnp.int32)
    midx = _mod_index(tm, geo["n_lat_rows"], geo["seq_len"], geo["ctx_row"])
    return _ffn(xs, te, tr, wg, wu, wd, rows=rows, tm=tm, tf=geo["tf"], mod_l=mod_l, g=g, midx=midx)


def _moe_layer(xs, mod_l, g, w_router, wg, wu, wd, geo, rows, final_g=None):
    d = xs.shape[1]
    n_exp = w_router.shape[1]
    tm = geo["tm_moe"]
    common = dict(rows=rows, n_lat_rows=geo["n_lat_rows"], seq_len=geo["seq_len"], ctx_row=geo["ctx_row"])
    wr = jnp.zeros((d, LANES), F32).at[:, :n_exp].set(w_router)
    wr_hi = wr.astype(BF16)
    wr = jnp.stack([wr_hi, (wr - wr_hi.astype(F32)).astype(BF16)])
    h, meta, gw, cnt = _router(xs, mod_l, g, wr, tm=geo["tm_router"], n_exp=n_exp, **common)
    counts = cnt[0, :n_exp].astype(jnp.int32)
    padded = ((counts + tm - 1) // tm) * tm
    ends = jnp.cumsum(padded)
    offs = ends - padded
    n_sorted_tiles = (TOP_K * rows) // tm + n_exp
    tile_start = jnp.arange(n_sorted_tiles, dtype=jnp.int32) * tm
    tile_expert = jnp.minimum(jnp.sum(tile_start[:, None] >= ends[None, :], axis=1), n_exp - 1).astype(jnp.int32)
    tile_rows = jnp.clip((offs + counts)[tile_expert] - tile_start, 0, tm).astype(jnp.int32)
    tile_rows = jnp.where(tile_start < ends[-1], tile_rows, 0)
    e1, e2, r1, r2 = meta[0], meta[1], meta[2], meta[3]
    dest = jnp.stack([offs[e1] + r1, offs[e2] + r2]).astype(jnp.int32)
    hs = _dispatch_rows(h, dest, n_sorted_tiles * tm)
    ys = _ffn(hs, tile_expert, tile_rows, wg, wu, wd, rows=n_sorted_tiles * tm, tm=tm, tf=geo["tf"])
    ypair = _gather_rows(ys, dest.reshape(TOP_K * rows))
    return _combine(xs, ypair, gw, mod_l, tm=geo["tm_row"], final_g=final_g, **common)


def kernel(x, c, ctx, c_ctx, w_mod, b_mod, norm_g, conv_w_in, conv_b_in, conv_w_dw, conv_b_dw,
           conv_ln_g, conv_ln_b, conv_w_out, conv_b_out, fnet_w, fnet_b, attn_w_qkv, attn_lambda,
           attn_subln_g, attn_w_o, ffn_w_gate, ffn_w_up, ffn_w_down, moe_w_router, moe_w_gate,
           moe_w_up, moe_w_down, final_g):
    b_, n, d = x.shape
    n_ctx = ctx.shape[1]
    depth = w_mod.shape[0]
    n_lat = b_ * n
    n_all = n_lat + b_ * n_ctx
    assert b_ < MOD_ROWS and d % LANES == 0 and n % GRID_W == 0 and n % n_ctx == 0
    assert d // DA_HEADS == LANES
    geo = dict(n_lat_rows=n_lat, seq_len=n, ctx_row=b_,
               tm_row=min(1024, n_ctx * b_, n), tm_dft=min(512, n), tm_ffn=min(1024, n_ctx * b_, n), tm_moe=512,
               tm_router=min(512, n_ctx * b_, n), tf=min(256, ffn_w_gate.shape[2]),
               tm_conv=n_ctx)

    xs = jnp.concatenate([x.reshape(n_lat, d), ctx.reshape(b_ * n_ctx, d)], axis=0)
    c_all = jnp.zeros((MOD_ROWS, d), F32).at[:b_].set(c).at[b_].set(c_ctx)
    mod = _modulation(c_all, w_mod, b_mod).reshape(depth, MOD_ROWS, 6, d)

    out = None
    for i in range(depth):
        need_ctx = i < depth - 1
        rows = n_all if need_ctx else n_lat
        kind = i % 3
        mod_l = mod[i]
        common = dict(rows=rows, n_lat_rows=n_lat, seq_len=n, ctx_row=b_)
        g0 = norm_g[i, 0].reshape(1, d)
        g1 = norm_g[i, 1].reshape(1, d)
        j = i // 3
        if kind == 0:
            u = _conv_in(xs, mod_l, g0, conv_w_in[j].astype(BF16), conv_b_in[j].reshape(1, 2 * d),
                         tm=geo["tm_row"], **common)
            xs = _conv_out(u, xs, mod_l, conv_w_dw[j], conv_b_dw[j].reshape(1, d),
                           conv_ln_g[j].reshape(1, d), conv_ln_b[j].reshape(1, d),
                           conv_w_out[j].astype(BF16), conv_b_out[j].reshape(1, d),
                           tm=geo["tm_conv"], **common)
        elif kind == 1:
            gd = d // F_GROUPS
            kk = jnp.arange(gd, dtype=jnp.int32)
            ang = ((kk[:, None] * kk[None, :]) % gd).astype(F32) * (2.0 * math.pi / gd)
            cs = (jnp.concatenate([jnp.cos(ang), jnp.sin(ang)], axis=1) / math.sqrt(gd)).astype(BF16)
            yc, ys = _fnet_a(xs, mod_l, g0, cs, tm=geo["tm_row"], **common)
            wf = fnet_w[j].astype(BF16)
            bf = fnet_b[j].reshape(1, d)
            ct, st = _dft_tables(n)
            xs = _fnet_b(ct, st, yc, ys, xs, mod_l, wf, bf, n_batch=b_, seq_len=n, row_off=0,
                         mod_ctx_row=None, tm=geo["tm_dft"])
            if need_ctx:
                ct, st = _dft_tables(n_ctx)
                xs = _fnet_b(ct, st, yc, ys, xs, mod_l, wf, bf, n_batch=b_, seq_len=n_ctx,
                             row_off=n_lat, mod_ctx_row=b_, tm=n_ctx)
        else:
            lam_init = 0.8 - 0.6 * math.exp(-0.3 * i)
            dh = d // DA_HEADS
            cos_t, sin_t = _rope_tables(n, LANES)
            q, k, v = _qkv(xs, mod_l, g0, attn_w_qkv[j].astype(BF16), cos_t, sin_t,
                           tm=geo["tm_row"], q_scale=(dh // 2) ** -0.5 * math.log2(math.e), **common)
            sg = attn_subln_g[j].reshape(1, dh)
            fl = dict(n_batch=b_, n_heads=DA_HEADS, kc_len=n_ctx, kc_off=n_lat, lam_init=lam_init)
            o = _flash(q, k, v, attn_lambda[j], sg, None, q_len=n, q_off=0, kx_len=n,
                       tq=min(512, n), tk=min(2048, n), **fl)
            if need_ctx:
                o = _flash(q, k, v, attn_lambda[j], sg, o, q_len=n_ctx, q_off=n_lat, kx_len=0,
                           tq=n_ctx, tk=n_ctx, **fl)
            xs = _proj_residual(o, attn_w_o[j].astype(BF16), xs, mod_l, tm=geo["tm_row"], **common)

        j = i // 2
        if i % 2 == 0:
            wg, wu, wd = (_cast_layer_bf16(w[:, None], j) for w in (ffn_w_gate, ffn_w_up, ffn_w_down))
            xs = _dense_ffn_layer(xs, mod_l, g1, wg, wu, wd, geo, rows)
        else:
            fg = final_g.reshape(1, d) if i == depth - 1 else None
            xs, w32 = lax.optimization_barrier((xs, (moe_w_gate, moe_w_up, moe_w_down)))
            wg, wu, wd = (_cast_layer_bf16(w, j) for w in w32)
            xs, wg, wu, wd = lax.optimization_barrier((xs, wg, wu, wd))
            res = _moe_layer(xs, mod_l, g1, moe_w_router[j], wg, wu, wd, geo, rows, final_g=fg)
            if fg is not None:
                out = res
            else:
                xs = res
    assert out is not None, "the final rms_norm is fused into the last layer's MoE combine (odd last layer index)"
    return out.reshape(b_, n, d)
```
